```python
import jax, jax.numpy as jnp
from jax import lax
import numpy as np

D_MODEL = 1024
BATCH = 32
SEQ = 256
DEPTH = 4
DEC_BATCH = 2
DEC_SEQ = 2048
PAST_LEN = 256

GRID_W = 64
N_EVEN = (DEPTH + 1) // 2
N_ODD = DEPTH // 2
N_MOD = 6
D_FF = ((8 * D_MODEL // 3 + 255) // 256) * 256
DA = D_MODEL // 2
HD_A = 64
H_A = DA // HD_A
W_LORA = 64
A_LORA = 64
G_LORA = 128
H_B = 4
NOPE = 128
ROPE_DIM = 64
V_DIM = DA // H_B
Q_LORA = 384
KV_LORA = 256
ATTN_SCALE = (NOPE + ROPE_DIM) ** -0.5
ROPE_AXIS_FREQS = ROPE_DIM // 4
ROPE_BASE = 10000.0
Q_BLOCK = 128
RWKV_IN = 3 * DA + W_LORA + A_LORA + G_LORA
MLA_IN = Q_LORA + KV_LORA + ROPE_DIM
IN_DIM = RWKV_IN + MLA_IN
POOL_WINDOWS = (2, 4, 8, 16)
N_POOL = len(POOL_WINDOWS)
GP = D_MODEL // N_POOL
EPS = 1e-6
GN_EPS = 64e-5

kernel_name = 'hybrid_rwkv7_mla_pool_diffusion_step'


def rms_norm(x, g):
    x32 = x.astype(jnp.float32)
    y = x32 * lax.rsqrt(jnp.mean(x32 * x32, axis=-1, keepdims=True) + EPS)
    return (y * g.astype(jnp.float32)).astype(x.dtype)


def adaln(cond, w, b):
    mod = jax.nn.silu(cond) @ w + b
    mod = mod.reshape((-1, 1, N_MOD, D_MODEL))
    return tuple(mod[:, :, i] for i in range(N_MOD))


def modulate(x, shift, scale):
    return x * (1 + scale) + shift


def swiglu(h, w1, w3, w2):
    return (jax.nn.silu(h @ w1) * (h @ w3)) @ w2


def centred_shift(h, mu):
    zeros = jnp.zeros_like(h[:, :1])
    prev = jnp.concatenate([zeros, h[:, :-1]], axis=1)
    nxt = jnp.concatenate([h[:, 1:], zeros], axis=1)
    return h + mu * (0.5 * (prev + nxt) - h)


def rot_half(x, ang):
    half = x.shape[-1] // 2
    x1, x2 = x[..., :half], x[..., half:]
    cos, sin = jnp.cos(ang), jnp.sin(ang)
    return jnp.concatenate([x1 * cos - x2 * sin, x2 * cos + x1 * sin], axis=-1)


def rope_axial(x, ang_row, ang_col):
    half = ROPE_DIM // 2
    return jnp.concatenate([rot_half(x[..., :half], ang_row), rot_half(x[..., half:], ang_col)], axis=-1)


def wkv_scan(r, k, v, w, kk, a, s0, reverse):
    xs = tuple(jnp.moveaxis(z, 1, 0) for z in (r, k, v, w, kk, a))

    def step(s, inp):
        r_t, k_t, v_t, w_t, kk_t, a_t = inp
        s_kk = jnp.einsum('bhvk,bhk->bhv', s, kk_t)
        s = (s * w_t[:, :, None, :] - s_kk[..., None] * (kk_t * a_t)[:, :, None, :]
             + v_t[..., None] * k_t[:, :, None, :])
        return s, jnp.einsum('bhvk,bhk->bhv', s, r_t)

    s_fin, ys = lax.scan(step, s0, xs, reverse=reverse)
    return jnp.moveaxis(ys, 0, 1), s_fin


def rwkv_mix(rw, s0, p):
    B, T, _ = rw.shape
    rw = rw.astype(jnp.float32)
    r, k, v, wd, ad, gd = jnp.split(
        rw, [DA, 2 * DA, 3 * DA, 3 * DA + W_LORA, 3 * DA + W_LORA + A_LORA], axis=-1)
    heads = lambda z: z.reshape(B, T, H_A, HD_A)
    g = jax.nn.sigmoid(gd) @ p['g_up']
    kk = heads(k * p['k_k'])
    kk = kk / (jnp.sqrt(jnp.sum(kk * kk, axis=-1, keepdims=True)) + 1e-12)
    rh, vh = heads(r), heads(v)
    ys, states = [], []
    for d in range(2):
        w_log = -jax.nn.softplus(-(p['w0'][d] + jnp.tanh(wd) @ p['w_up'][d])) - 0.5
        w = jnp.exp(-jnp.exp(w_log))
        a = jax.nn.sigmoid(p['a0'][d] + ad @ p['a_up'][d])
        k_d = k * (1 + (a - 1) * p['k_a'])
        y, s = wkv_scan(rh, heads(k_d), vh, heads(w), kk, heads(a),
                        s0[:, d].astype(jnp.float32), reverse=(d == 1))
        ys.append(y)
        states.append(s)
    y = ys[0] + ys[1]
    mu = jnp.mean(y, axis=-1, keepdims=True)
    var = jnp.mean(jnp.square(y - mu), axis=-1, keepdims=True)
    yn = ((y - mu) * lax.rsqrt(var + GN_EPS)).reshape(B, T, DA) * p['ln_g'] + p['ln_b']
    bonus = jnp.sum(rh * heads(k) * p['r_k'], axis=-1, keepdims=True) * vh
    out = (yn + bonus.reshape(B, T, DA)) * g
    return out, jnp.stack(states, axis=1)


def mla_project(hm, p):
    B, T, _ = hm.shape
    c_q, c_kv, k_r = jnp.split(hm, [Q_LORA, Q_LORA + KV_LORA], axis=-1)
    q = (rms_norm(c_q, p['cq_g']) @ p['uq_w']).reshape(B, T, H_B, NOPE + ROPE_DIM)
    q_n = rms_norm(q[..., :NOPE], p['qn_g'])
    q_r = rms_norm(q[..., NOPE:], p['qr_g'])
    return q_n, q_r, rms_norm(c_kv, p['ckv_g']), rms_norm(k_r, p['kr_g'])


def mla_expand(c_kv, p):
    B, T, _ = c_kv.shape
    kv = (c_kv @ p['ukv_w']).reshape(B, T, H_B, NOPE + V_DIM)
    return rms_norm(kv[..., :NOPE], p['kn_g']), kv[..., NOPE:]


def mla_attend(q_n, q_r, k_n, k_r, v):
    s = jnp.einsum('bqhd,bkhd->bhqk', q_n, k_n) + jnp.einsum('bqhd,bkd->bhqk', q_r, k_r)
    pr = jax.nn.softmax(s.astype(jnp.float32) * ATTN_SCALE, axis=-1).astype(v.dtype)
    return jnp.einsum('bhqk,bkhd->bqhd', pr, v)


def mla_attend_blocks(q_n, q_r, k_n, k_r, v):
    B, T = q_n.shape[:2]
    nb = T // Q_BLOCK
    qn_b = q_n.reshape(B, nb, Q_BLOCK, H_B, NOPE).swapaxes(0, 1)
    qr_b = q_r.reshape(B, nb, Q_BLOCK, H_B, ROPE_DIM).swapaxes(0, 1)
    o = lax.map(lambda qs: mla_attend(qs[0], qs[1], k_n, k_r, v), (qn_b, qr_b))
    return o.swapaxes(0, 1).reshape(B, T, H_B * V_DIM)


def even_mixer_context(h, p):
    B, C, _ = h.shape
    proj = h @ p['in_w']
    rw = centred_shift(proj[..., :RWKV_IN], p['shift_mu'])
    s0 = jnp.zeros((B, 2, H_A, HD_A, HD_A), jnp.float32)
    a_out, s_fin = rwkv_mix(rw, s0, p)
    q_n, q_r, c_kv, k_r = mla_project(proj[..., RWKV_IN:], p)
    k_n, v = mla_expand(c_kv, p)
    b_out = mla_attend(q_n, q_r, k_n, k_r, v).reshape(B, C, DA)
    out = jnp.concatenate([a_out.astype(b_out.dtype), b_out], axis=-1) @ p['out_w']
    return out, s_fin, c_kv, k_r


def even_mixer_latent(h, s_ctx, ckv_ctx, kr_ctx, ang_row, ang_col, p):
    proj = h @ p['in_w']
    rw = centred_shift(proj[..., :RWKV_IN], p['shift_mu'])
    a_out, _ = rwkv_mix(rw, s_ctx, p)
    q_n, q_r, c_kv, k_r = mla_project(proj[..., RWKV_IN:], p)
    q_r = rope_axial(q_r, ang_row[:, None], ang_col[:, None])
    k_r = rope_axial(k_r, ang_row, ang_col)
    k_n, v = mla_expand(c_kv, p)
    kn_c, v_c = mla_expand(ckv_ctx, p)
    k_n_all = jnp.concatenate([k_n, kn_c.astype(k_n.dtype)], axis=1)
    k_r_all = jnp.concatenate([k_r, kr_ctx.astype(k_r.dtype)], axis=1)
    v_all = jnp.concatenate([v, v_c.astype(v.dtype)], axis=1)
    b_out = mla_attend_blocks(q_n, q_r, k_n_all, k_r_all, v_all)
    return jnp.concatenate([a_out.astype(b_out.dtype), b_out], axis=-1) @ p['out_w']


def pool_mixer(h, pool_w, pool_scale):
    B, T, _ = h.shape
    h32 = h.astype(jnp.float32)
    cs = jnp.concatenate([jnp.zeros((B, 1, D_MODEL), jnp.float32), jnp.cumsum(h32, axis=1)], axis=1)
    t = jnp.arange(T)
    diffs = []
    for gi, win in enumerate(POOL_WINDOWS):
        lo = jnp.clip(t - win // 2, 0, T)
        hi = jnp.clip(t + win // 2, 0, T)
        cg = cs[..., gi * GP:(gi + 1) * GP]
        mean = (cg[:, hi] - cg[:, lo]) / (hi - lo).astype(jnp.float32)[:, None]
        diffs.append(mean - h32[..., gi * GP:(gi + 1) * GP])
    d = jnp.stack(diffs, axis=2)
    out = jnp.einsum('btgi,gio->btgo', d, pool_w).reshape(B, T, D_MODEL)
    return (out * pool_scale).astype(h.dtype)


def setup_inputs(seed: int = 0) -> dict:
    key = jax.random.key(seed)
    kit = iter(list(jax.random.split(key, 40)))
    f32 = jnp.float32

    def nrm(shape, scale):
        return scale * jax.random.normal(next(kit), shape, f32)

    def gain(shape, s=0.01):
        return 1.0 + nrm(shape, s)

    return {
        'x_prompt': nrm((BATCH, SEQ, D_MODEL), 1.0),
        'x_sample': nrm((DEC_BATCH, DEC_SEQ, D_MODEL), 1.0),
        'c': nrm((DEC_BATCH, D_MODEL), 1.0),
        'state_rwkv': nrm((DEC_BATCH, N_EVEN, 2, H_A, HD_A, HD_A), 0.5),
        'cache_mla_ckv': nrm((DEC_BATCH, N_EVEN, PAST_LEN, KV_LORA), 1.0),
        'cache_mla_krope': nrm((DEC_BATCH, N_EVEN, PAST_LEN, ROPE_DIM), 1.0),
        'c_ctx': nrm((D_MODEL,), 1.0),
        'norm1_g': gain((DEPTH, D_MODEL)),
        'norm2_g': gain((DEPTH, D_MODEL)),
        'ada_w': nrm((DEPTH, D_MODEL, N_MOD * D_MODEL), D_MODEL ** -0.5),
        'ada_b': nrm((DEPTH, N_MOD * D_MODEL), 0.01),
        'ffn_w1': nrm((DEPTH, D_MODEL, D_FF), D_MODEL ** -0.5),
        'ffn_w3': nrm((DEPTH, D_MODEL, D_FF), D_MODEL ** -0.5),
        'ffn_w2': nrm((DEPTH, D_FF, D_MODEL), D_FF ** -0.5),
        'in_w': nrm((N_EVEN, D_MODEL, IN_DIM), D_MODEL ** -0.5),
        'out_w': nrm((N_EVEN, D_MODEL, D_MODEL), D_MODEL ** -0.5),
        'shift_mu': jax.random.uniform(next(kit), (N_EVEN, RWKV_IN), f32),
        'rwkv_w0': jax.random.uniform(next(kit), (N_EVEN, 2, DA), f32, -4.0, 1.0),
        'rwkv_w_up': nrm((N_EVEN, 2, W_LORA, DA), 0.1),
        'rwkv_a0': nrm((N_EVEN, 2, DA), 0.1),
        'rwkv_a_up': nrm((N_EVEN, 2, A_LORA, DA), 0.1),
        'rwkv_g_up': nrm((N_EVEN, G_LORA, DA), G_LORA ** -0.5),
        'rwkv_k_k': gain((N_EVEN, DA), 0.1),
        'rwkv_k_a': gain((N_EVEN, DA), 0.1),
        'rwkv_r_k': nrm((N_EVEN, H_A, HD_A), 0.1),
        'rwkv_ln_g': gain((N_EVEN, DA)),
        'rwkv_ln_b': nrm((N_EVEN, DA), 0.01),
        'mla_cq_g': gain((N_EVEN, Q_LORA)),
        'mla_uq_w': nrm((N_EVEN, Q_LORA, H_B * (NOPE + ROPE_DIM)), Q_LORA ** -0.5),
        'mla_ckv_g': gain((N_EVEN, KV_LORA)),
        'mla_ukv_w': nrm((N_EVEN, KV_LORA, H_B * (NOPE + V_DIM)), KV_LORA ** -0.5),
        'mla_qn_g': gain((N_EVEN, NOPE)),
        'mla_qr_g': gain((N_EVEN, ROPE_DIM)),
        'mla_kn_g': gain((N_EVEN, NOPE)),
        'mla_kr_g': gain((N_EVEN, ROPE_DIM)),
        'pool_w': nrm((N_ODD, N_POOL, GP, GP), GP ** -0.5),
        'pool_scale': gain((N_ODD, D_MODEL), 0.1),
    }


def reference(x_prompt, x_sample, c, state_rwkv, cache_mla_ckv, cache_mla_krope, c_ctx,
              norm1_g, norm2_g, ada_w, ada_b, ffn_w1, ffn_w3, ffn_w2,
              in_w, out_w, shift_mu, rwkv_w0, rwkv_w_up, rwkv_a0, rwkv_a_up, rwkv_g_up,
              rwkv_k_k, rwkv_k_a, rwkv_r_k, rwkv_ln_g, rwkv_ln_b,
              mla_cq_g, mla_uq_w, mla_ckv_g, mla_ukv_w, mla_qn_g, mla_qr_g, mla_kn_g, mla_kr_g,
              pool_w, pool_scale):
    S = x_sample.shape[1]
    n_rows = S // GRID_W
    row = jnp.repeat(jnp.arange(n_rows), GRID_W).astype(jnp.float32)
    col = jnp.tile(jnp.arange(GRID_W), n_rows).astype(jnp.float32)
    freqs = ROPE_BASE ** (-jnp.arange(ROPE_AXIS_FREQS, dtype=jnp.float32) / ROPE_AXIS_FREQS)
    ang_row = row[:, None] * freqs
    ang_col = col[:, None] * freqs

    y_p, y_s = x_prompt, x_sample
    new_s, new_ckv, new_kr = [], [], []
    for l in range(DEPTH):
        sh1, sc1, g1, sh2, sc2, g2 = adaln(c_ctx, ada_w[l], ada_b[l])
        ls1, lc1, lg1, ls2, lc2, lg2 = adaln(c, ada_w[l], ada_b[l])
        hp = modulate(rms_norm(y_p, norm1_g[l]), sh1, sc1)
        hs = modulate(rms_norm(y_s, norm1_g[l]), ls1, lc1)
        if l % 2 == 0:
            i = l // 2
            p = dict(in_w=in_w[i], out_w=out_w[i], shift_mu=shift_mu[i],
                     w0=rwkv_w0[i], w_up=rwkv_w_up[i], a0=rwkv_a0[i], a_up=rwkv_a_up[i],
                     g_up=rwkv_g_up[i], k_k=rwkv_k_k[i], k_a=rwkv_k_a[i], r_k=rwkv_r_k[i],
                     ln_g=rwkv_ln_g[i], ln_b=rwkv_ln_b[i],
                     cq_g=mla_cq_g[i], uq_w=mla_uq_w[i], ckv_g=mla_ckv_g[i], ukv_w=mla_ukv_w[i],
                     qn_g=mla_qn_g[i], qr_g=mla_qr_g[i], kn_g=mla_kn_g[i], kr_g=mla_kr_g[i])
            mp, s_fin, ckv, kr = even_mixer_context(hp, p)
            ms = even_mixer_latent(hs, state_rwkv[:, i], cache_mla_ckv[:, i], cache_mla_krope[:, i],
                                   ang_row, ang_col, p)
            new_s.append(s_fin.astype(x_prompt.dtype))
            new_ckv.append(ckv)
            new_kr.append(kr)
        else:
            j = l // 2
            mp = pool_mixer(hp, pool_w[j], pool_scale[j])
            ms = pool_mixer(hs, pool_w[j], pool_scale[j])
        y_p = y_p + g1 * mp
        y_s = y_s + lg1 * ms
        y_p = y_p + g2 * swiglu(modulate(rms_norm(y_p, norm2_g[l]), sh2, sc2),
                                ffn_w1[l], ffn_w3[l], ffn_w2[l])
        y_s = y_s + lg2 * swiglu(modulate(rms_norm(y_s, norm2_g[l]), ls2, lc2),
                                 ffn_w1[l], ffn_w3[l], ffn_w2[l])

    new_state_rwkv = jnp.stack(new_s, axis=1)
    new_cache_mla_ckv = jnp.stack(new_ckv, axis=1)
    new_cache_mla_krope = jnp.stack(new_kr, axis=1)
    return (y_p, y_s, new_state_rwkv, new_cache_mla_ckv, new_cache_mla_krope)
```

```python
import functools

import jax
import jax.numpy as jnp
from jax import lax
from jax.experimental import pallas as pl
from jax.experimental.pallas import tpu as pltpu

D_MODEL = 1024
BATCH, SEQ = 32, 256
DEC_BATCH, DEC_SEQ = 2, 2048
DEPTH = 4
PAST_LEN = 256
GRID_W = 64
N_MOD = 6
D_FF = 2816
DA = 512
HD_A = 64
H_A = 8
W_LORA, A_LORA, G_LORA = 64, 64, 128
H_B = 4
NOPE, ROPE_DIM, V_DIM = 128, 64, 128
Q_LORA, KV_LORA = 384, 256
ATTN_SCALE = (NOPE + ROPE_DIM) ** -0.5
ROPE_AXIS_FREQS = ROPE_DIM // 4
ROPE_BASE = 10000.0
RWKV_IN = 3 * DA + W_LORA + A_LORA + G_LORA
MLA_IN = Q_LORA + KV_LORA + ROPE_DIM
MLA_PAD = 768
POOL_WINDOWS = (2, 4, 8, 16)
GP = 256
EPS = 1e-6
GN_EPS = 64e-5

N_CTX = BATCH * SEQ
N_LAT = DEC_BATCH * DEC_SEQ
N_TOK = N_CTX + N_LAT

LANES = 128
TM = 256
N_TILES = N_TOK // TM
CTX_TILES = N_CTX // TM
LAT_TILES_PER_SEQ = DEC_SEQ // TM
CHUNK = 64
HALO = 128
TM_FFN = 512
TF_FFN = 1408
VMEM_LIMIT = 56 * 1024 * 1024

F32 = jnp.float32
BF16 = jnp.bfloat16


def _cond_of_tile(t):
    return jnp.where(t < CTX_TILES, 0, 1 + (t - CTX_TILES) // LAT_TILES_PER_SEQ)


def _dot(a, b, dims=((1,), (0,))):
    return lax.dot_general(a, b, (dims, ((), ())), preferred_element_type=F32)


_NT = ((1,), (1,))


def _split2(x):
    hi = x.astype(BF16)
    lo = (x - hi.astype(F32)).astype(BF16)
    return hi, lo


def _split3(x):
    hi = x.astype(BF16)
    r1 = x - hi.astype(F32)
    mid = r1.astype(BF16)
    lo = (r1 - mid.astype(F32)).astype(BF16)
    return hi, mid, lo


def _dot1(a, b, dims=((1,), (0,))):
    return _dot(a.astype(BF16), b.astype(BF16), dims)


def _dot3(a, b, dims=((1,), (0,))):
    ah, al = _split2(a)
    bh, bl = _split2(b)
    return _dot(ah, bh, dims) + (_dot(ah, bl, dims) + _dot(al, bh, dims))


def _dot_exact_lhs(a_bf16, b):
    b0, b1, b2 = _split3(b)
    return _dot(a_bf16, b0) + (_dot(a_bf16, b1) + _dot(a_bf16, b2))


def _rms(x, g):
    return x * lax.rsqrt(jnp.mean(x * x, axis=-1, keepdims=True) + EPS) * g


def _sigmoid(x):
    return 1.0 / (1.0 + jnp.exp(-x))


def _silu(x):
    return x * _sigmoid(x)


def _softplus(x):
    return jnp.maximum(x, 0.0) + jnp.log(1.0 + jnp.exp(-jnp.abs(x)))


def _group_ones(n, g):
    i = lax.broadcasted_iota(jnp.int32, (n, n), 0) // g
    j = lax.broadcasted_iota(jnp.int32, (n, n), 1) // g
    return (i == j).astype(BF16)


def _gsum(x, ones):
    hi, lo = _split2(x)
    return _dot(hi, ones) + _dot(lo, ones)


def _adaln_kernel(c_ref, w_ref, b_ref, o_ref):
    o_ref[0] = _dot3(_silu(c_ref[...]), w_ref[0]) + b_ref[0]


def _adaln(conds, ada_w, ada_b):
    tn = 1536
    return pl.pallas_call(
        _adaln_kernel,
        grid=(DEPTH, N_MOD * D_MODEL // tn),
        in_specs=[
            pl.BlockSpec((8, D_MODEL), lambda l, j: (0, 0)),
            pl.BlockSpec((1, D_MODEL, tn), lambda l, j: (l, 0, j)),
            pl.BlockSpec((1, 1, tn), lambda l, j: (l, 0, j)),
        ],
        out_specs=pl.BlockSpec((1, 8, tn), lambda l, j: (l, 0, j)),
        out_shape=jax.ShapeDtypeStruct((DEPTH, 8, N_MOD * D_MODEL), F32),
        compiler_params=pltpu.CompilerParams(
            dimension_semantics=("arbitrary", "arbitrary"), vmem_limit_bytes=VMEM_LIMIT),
        name="adaln",
    )(conds, ada_w, ada_b.reshape(DEPTH, 1, N_MOD * D_MODEL))


def _mod_spec(l):
    return pl.BlockSpec((1, 1, N_MOD, D_MODEL), lambda t, *_: (l, _cond_of_tile(t), 0, 0))


def _inproj_kernel(x_ref, mod_ref, g_ref, w_ref, or_ref, om_ref):
    m = mod_ref[0, 0]
    h = _rms(x_ref[...], g_ref[...]) * (1.0 + m[1:2]) + m[0:1]
    p = _dot(h.astype(BF16), w_ref[...])
    or_ref[...] = p[:, :RWKV_IN]
    om_ref[...] = p[:, RWKV_IN:]


def _inproj(x, mods, l, g, w):
    return pl.pallas_call(
        _inproj_kernel,
        grid=(N_TILES,),
        in_specs=[
            pl.BlockSpec((TM, D_MODEL), lambda t: (t, 0)),
            _mod_spec(l),
            pl.BlockSpec((1, D_MODEL), lambda t: (0, 0)),
            pl.BlockSpec((D_MODEL, RWKV_IN + MLA_PAD), lambda t: (0, 0)),
        ],
        out_specs=[
            pl.BlockSpec((TM, RWKV_IN), lambda t: (t, 0)),
            pl.BlockSpec((TM, MLA_PAD), lambda t: (t, 0)),
        ],
        out_shape=[
            jax.ShapeDtypeStruct((N_TOK, RWKV_IN), F32),
            jax.ShapeDtypeStruct((N_TOK, MLA_PAD), F32),
        ],
        compiler_params=pltpu.CompilerParams(
            dimension_semantics=("arbitrary",), vmem_limit_bytes=VMEM_LIMIT),
        name="inproj",
    )(x, mods, g, w)


def _rwkv_kernel(pr_ref, pk_ref, pv_ref, pl_ref, mur_ref, muk_ref, muv_ref, mul_ref,
                 w0_ref, wup_ref, a0_ref, aup_ref, gup_ref, kkg_ref, kag_ref, rkg_ref,
                 lng_ref, lnb_ref, *rest, seq_len, zero_init):
    if zero_init:
        h0_ref = None
        out_ref, hfin_ref, m_s, n_s, yq_s, y0_s, g_s, bonus_s, yf_s, yb_s, hf_s, hb_s = rest
    else:
        h0_ref, out_ref, hfin_ref, m_s, n_s, yq_s, y0_s, g_s, bonus_s, yf_s, yb_s, hf_s, hb_s = rest
    C = CHUNK
    n_chunks = seq_len // C
    lane = lax.broadcasted_iota(jnp.int32, (1, LANES), 1)
    m0 = (lane < HD_A).astype(F32)
    m1 = 1.0 - m0
    ones_g = _group_ones(LANES, HD_A)
    row = lax.broadcasted_iota(jnp.int32, (C, 1), 0)
    i2 = lax.broadcasted_iota(jnp.int32, (2 * C, 2 * C), 0)
    j2 = lax.broadcasted_iota(jnp.int32, (2 * C, 2 * C), 1)
    same_head = (i2 // C) == (j2 // C)
    eye2 = (i2 == j2).astype(F32)
    ic = lax.broadcasted_iota(jnp.int32, (C, C), 0)
    jc = lax.broadcasted_iota(jnp.int32, (C, C), 1)
    incl_c = ((jc <= ic).astype(BF16), (jc >= ic).astype(BF16))
    incl_blk = (same_head & (j2 <= i2), same_head & (j2 >= i2))
    strict_blk = (same_head & (j2 < i2), same_head & (j2 > i2))

    def pair_masks(d):
        masks = []
        s = 1
        while s < C:
            later, earlier = ((i2 // s) % 2, (j2 // s) % 2) if d == 0 else ((j2 // s) % 2, (i2 // s) % 2)
            masks.append(((i2 // (2 * s)) == (j2 // (2 * s))) & (later == 1) & (earlier == 0))
            s *= 2
        return masks

    pair_blk = (pair_masks(0), pair_masks(1))

    def stack(x):
        return jnp.concatenate([x * m0, x * m1], axis=0)

    def shifted(ref, mu, c):
        x = ref[pl.ds(pl.multiple_of(c * C, C), C), :]
        pstart = pl.multiple_of(jnp.maximum(c * C - 8, 0), 8)
        nstart = pl.multiple_of(jnp.minimum((c + 1) * C, seq_len - 8), 8)
        prev_row = jnp.where(c > 0, ref[pl.ds(pstart, 8), :][7:8], 0.0)
        next_row = jnp.where(c < n_chunks - 1, ref[pl.ds(nstart, 8), :][0:1], 0.0)
        prev = jnp.where(row == 0, prev_row, pltpu.roll(x, 1, 0))
        nxt = jnp.where(row == C - 1, next_row, pltpu.roll(x, C - 1, 0))
        return x + mu * (0.5 * (prev + nxt) - x)

    def chunk_mats(d, r, kd, v, kk, b, logw):
        lp = _dot_exact_lhs(incl_c[d], logw)
        lpc = lp[C - 1:C] if d == 0 else lp[0:1]
        p_in = jnp.exp(lp)
        p_ex = jnp.exp(lp - logw)
        p_inv = jnp.exp(-lp)
        p_end = jnp.exp(lpc - lp)
        qs = stack(r * p_in)
        a_s = stack(kk * p_ex)
        khs = stack(kd * p_inv)
        bs = stack(b * p_inv)
        kbs = stack(kd * p_end)
        bbs = stack(b * p_end)
        vs = stack(v)
        lab = jnp.where(strict_blk[d], _dot1(a_s, bs, _NT), 0.0)
        lak = jnp.where(strict_blk[d], _dot1(a_s, khs, _NT), 0.0)
        dqb = jnp.where(incl_blk[d], _dot1(qs, bs, _NT), 0.0)
        dqk = jnp.where(incl_blk[d], _dot1(qs, khs, _NT), 0.0)
        tinv = eye2 - jnp.where(pair_blk[d][0], lab, 0.0)
        for lvl in range(1, len(pair_blk[d])):
            cs = jnp.where(pair_blk[d][lvl], lab, 0.0)
            tinv = tinv - _dot1(_dot1(tinv, cs), tinv)
        x2 = _dot1(lak, vs)
        w12 = _dot1(tinv, jnp.concatenate([a_s, x2], axis=1))
        mw = _dot1(bbs.T, w12)
        dw = _dot1(dqb, w12)
        m_blk = eye2 * jnp.exp(lpc) - mw[:, :LANES]
        n_blk = _dot1(kbs.T, vs) - mw[:, LANES:]
        yq = qs - dw[:, :LANES]
        y0 = _dot1(dqk, vs) - dw[:, LANES:]
        return m_blk, n_blk, yq, y0

    def phase_a(c, carry):
        r = shifted(pr_ref, mur_ref[...], c)
        k = shifted(pk_ref, muk_ref[...], c)
        v = shifted(pv_ref, muv_ref[...], c)
        lo = shifted(pl_ref, mul_ref[...], c)
        wa = lo[:, :LANES]
        rows = pl.ds(pl.multiple_of(c * C, C), C)
        g_s[rows, :] = _dot3(_sigmoid(lo[:, LANES:]), gup_ref[...])
        kkr = k * kkg_ref[...]
        kk = kkr / (jnp.sqrt(_gsum(kkr * kkr, ones_g)) + 1e-12)
        bonus_s[rows, :] = _gsum(r * k * rkg_ref[...], ones_g) * v
        twa = jnp.tanh(wa)
        for d in range(2):
            wl = w0_ref[d:d + 1, :] + _dot3(twa, wup_ref[d])
            logw = -jnp.exp(-_softplus(-wl) - 0.5)
            a = _sigmoid(a0_ref[d:d + 1, :] + _dot3(wa, aup_ref[d]))
            kd = k * (1.0 + (a - 1.0) * kag_ref[...])
            m_blk, n_blk, yq, y0 = chunk_mats(d, r, kd, v, kk, kk * a, logw)
            m_s[d, c] = m_blk
            n_s[d, c] = n_blk
            yq_s[d, c] = yq
            y0_s[d, c] = y0
        return carry

    lax.fori_loop(0, n_chunks, phase_a, 0)

    if zero_init:
        hf_s[...] = jnp.zeros((LANES, LANES), F32)
        hb_s[...] = jnp.zeros((LANES, LANES), F32)
    else:
        hf_s[...] = h0_ref[0, 0, 0]
        hb_s[...] = h0_ref[0, 0, 1]

    def phase_b(i, carry):
        for d, h_s, y_s, c in ((0, hf_s, yf_s, i), (1, hb_s, yb_s, n_chunks - 1 - i)):
            h = h_s[...]
            ys = _dot3(yq_s[d, c], h) + y0_s[d, c]
            y_s[pl.ds(pl.multiple_of(c * C, C), C), :] = ys[:C] + ys[C:]
            h_s[...] = _dot3(m_s[d, c], h) + n_s[d, c]
        return carry

    lax.fori_loop(0, n_chunks, phase_b, 0)
    hfin_ref[0, 0, 0] = hf_s[...]
    hfin_ref[0, 0, 1] = hb_s[...]

    def phase_c(c, carry):
        rows = pl.ds(pl.multiple_of(c * C, C), C)
        y = yf_s[rows, :] + yb_s[rows, :]
        mu = _gsum(y, ones_g) * (1.0 / HD_A)
        yc = y - mu
        var = _gsum(yc * yc, ones_g) * (1.0 / HD_A)
        yn = yc * lax.rsqrt(var + GN_EPS) * lng_ref[...] + lnb_ref[...]
        out_ref[rows, :] = ((yn + bonus_s[rows, :]) * g_s[rows, :]).astype(out_ref.dtype)
        return carry

    lax.fori_loop(0, n_chunks, phase_c, 0)


def _rwkv(proj_r, p, h0, *, seq_len, n_seq, row_off):
    n_chunks = seq_len // CHUNK
    off = row_off // seq_len
    n_pairs = DA // LANES
    zero_init = h0 is None
    col = lambda cb: (lambda s, hp: (off + s, cb + hp))
    vec = lambda: pl.BlockSpec((1, LANES), lambda s, hp: (0, hp))
    in_specs = [
        pl.BlockSpec((seq_len, LANES), col(0)),
        pl.BlockSpec((seq_len, LANES), col(n_pairs)),
        pl.BlockSpec((seq_len, LANES), col(2 * n_pairs)),
        pl.BlockSpec((seq_len, 2 * LANES), lambda s, hp: (off + s, 3 * n_pairs // 2)),
        vec(), vec(), vec(),
        pl.BlockSpec((1, 2 * LANES), lambda s, hp: (0, 0)),
        pl.BlockSpec((2, LANES), lambda s, hp: (0, hp)),
        pl.BlockSpec((2, LANES, LANES), lambda s, hp: (0, 0, hp)),
        pl.BlockSpec((2, LANES), lambda s, hp: (0, hp)),
        pl.BlockSpec((2, LANES, LANES), lambda s, hp: (0, 0, hp)),
        pl.BlockSpec((G_LORA, LANES), lambda s, hp: (0, hp)),
        vec(), vec(), vec(), vec(), vec(),
    ]
    args = [proj_r, proj_r, proj_r, proj_r, p["mu_r"], p["mu_k"], p["mu_v"], p["mu_l"],
            p["w0"], p["w_up"], p["a0"], p["a_up"], p["g_up"], p["k_k"], p["k_a"], p["r_k"],
            p["ln_g"], p["ln_b"]]
    if not zero_init:
        in_specs.append(pl.BlockSpec((1, 1, 2, LANES, LANES), lambda s, hp: (s, hp, 0, 0, 0)))
        args.append(h0)
    blk = (2 * CHUNK, LANES)
    return pl.pallas_call(
        functools.partial(_rwkv_kernel, seq_len=seq_len, zero_init=zero_init),
        grid=(n_seq, n_pairs),
        in_specs=in_specs,
        out_specs=[
            pl.BlockSpec((seq_len, LANES), lambda s, hp: (s, hp)),
            pl.BlockSpec((1, 1, 2, LANES, LANES), lambda s, hp: (s, hp, 0, 0, 0)),
        ],
        out_shape=[
            jax.ShapeDtypeStruct((n_seq * seq_len, DA), BF16),
            jax.ShapeDtypeStruct((n_seq, n_pairs, 2, LANES, LANES), F32),
        ],
        scratch_shapes=[
            pltpu.VMEM((2, n_chunks) + blk, F32),
            pltpu.VMEM((2, n_chunks) + blk, F32),
            pltpu.VMEM((2, n_chunks) + blk, F32),
            pltpu.VMEM((2, n_chunks) + blk, F32),
            pltpu.VMEM((seq_len, LANES), F32),
            pltpu.VMEM((seq_len, LANES), F32),
            pltpu.VMEM((seq_len, LANES), F32),
            pltpu.VMEM((seq_len, LANES), F32),
            pltpu.VMEM((LANES, LANES), F32),
            pltpu.VMEM((LANES, LANES), F32),
        ],
        compiler_params=pltpu.CompilerParams(
            dimension_semantics=("arbitrary", "arbitrary"), vmem_limit_bytes=VMEM_LIMIT),
        name="rwkv_ctx" if zero_init else "rwkv_lat",
    )(*args)


def _rope(x, cos, sin):
    lane = lax.broadcasted_iota(jnp.int32, x.shape, 1)
    swapped = jnp.where((lane % 32) < 16, pltpu.roll(x, LANES - 16, 1), pltpu.roll(x, 16, 1))
    return x * cos + swapped * sin


def _slab_rms(x, g):
    return x * lax.rsqrt(jnp.sum(x * x, axis=-1, keepdims=True) * (1.0 / ROPE_DIM) + EPS) * g


def _kv_expand(ckv_n, kr_att, ukv_ref, kng_ref, k_ref, v_ref):
    kv = _dot(ckv_n.astype(BF16), ukv_ref[...])
    kr_b = kr_att.astype(BF16)
    for h in range(H_B):
        kn = _rms(kv[:, h * NOPE:(h + 1) * NOPE], kng_ref[...])
        k_ref[:, 2 * h * LANES:(2 * h + 1) * LANES] = kn.astype(BF16)
        k_ref[:, (2 * h + 1) * LANES:(2 * h + 2) * LANES] = kr_b
    v_ref[...] = kv[:, H_B * NOPE:].astype(BF16)


def _mla_proj_kernel(pm_ref, cqg_ref, uq_ref, qng_ref, qrg_ref, ckvg_ref, ukv_ref, kng_ref, krg_ref,
                     *rest, rope):
    if rope:
        cos_ref, sin_ref, q_ref, k_ref, v_ref, ckv_ref, kr_ref = rest
    else:
        q_ref, k_ref, v_ref, ckv_ref, kr_ref = rest
    pm = pm_ref[...]
    cq = _rms(pm[:, :Q_LORA], cqg_ref[...])
    q = _dot(cq.astype(BF16), uq_ref[...])
    for h in range(H_B):
        qn = _rms(q[:, 2 * h * LANES:(2 * h + 1) * LANES], qng_ref[...])
        qr = _slab_rms(q[:, (2 * h + 1) * LANES:(2 * h + 2) * LANES], qrg_ref[...])
        if rope:
            qr = _rope(qr, cos_ref[...], sin_ref[...])
        q_ref[:, 2 * h * LANES:(2 * h + 1) * LANES] = (qn * ATTN_SCALE).astype(BF16)
        q_ref[:, (2 * h + 1) * LANES:(2 * h + 2) * LANES] = (qr * ATTN_SCALE).astype(BF16)
    ckv_n = _rms(pm[:, Q_LORA:Q_LORA + KV_LORA], ckvg_ref[...])
    ckv_ref[...] = ckv_n
    kr = _slab_rms(pm[:, Q_LORA + KV_LORA:], krg_ref[...])
    kr_ref[...] = kr[:, :ROPE_DIM]
    kr_att = _rope(kr, cos_ref[...], sin_ref[...]) if rope else kr
    _kv_expand(ckv_n, kr_att, ukv_ref, kng_ref, k_ref, v_ref)


def _mla_proj(proj_m, p, rope_tabs, *, n_rows, row_off, rope):
    t0 = row_off // TM
    full = lambda shape: pl.BlockSpec(shape, lambda t: tuple(0 for _ in shape))
    in_specs = [
        pl.BlockSpec((TM, MLA_PAD), lambda t: (t0 + t, 0)),
        full((1, Q_LORA)), full((Q_LORA, 2 * H_B * LANES)), full((1, LANES)), full((1, LANES)),
        full((1, KV_LORA)), full((KV_LORA, 2 * H_B * NOPE)), full((1, LANES)), full((1, LANES)),
    ]
    args = [proj_m, p["cq_g"], p["uq_w"], p["qn_g"], p["qr_g"], p["ckv_g"], p["ukv_w"], p["kn_g"], p["kr_g"]]
    if rope:
        per_seq = DEC_SEQ // TM
        in_specs += [pl.BlockSpec((TM, LANES), lambda t: (t % per_seq, 0))] * 2
        args += list(rope_tabs)
    rows = lambda w: pl.BlockSpec((TM, w), lambda t: (t, 0))
    return pl.pallas_call(
        functools.partial(_mla_proj_kernel, rope=rope),
        grid=(n_rows // TM,),
        in_specs=in_specs,
        out_specs=[rows(2 * H_B * LANES), rows(2 * H_B * LANES), rows(H_B * V_DIM), rows(KV_LORA), rows(ROPE_DIM)],
        out_shape=[
            jax.ShapeDtypeStruct((n_rows, 2 * H_B * LANES), BF16),
            jax.ShapeDtypeStruct((n_rows, 2 * H_B * LANES), BF16),
            jax.ShapeDtypeStruct((n_rows, H_B * V_DIM), BF16),
            jax.ShapeDtypeStruct((n_rows, KV_LORA), F32),
            jax.ShapeDtypeStruct((n_rows, ROPE_DIM), F32),
        ],
        compiler_params=pltpu.CompilerParams(
            dimension_semantics=("arbitrary",), vmem_limit_bytes=VMEM_LIMIT),
        name="mla_proj_lat" if rope else "mla_proj_ctx",
    )(*args)


def _mla_cache_kernel(ckv_ref, kr_ref, ukv_ref, kng_ref, k_ref, v_ref):
    _kv_expand(ckv_ref[...], kr_ref[...], ukv_ref, kng_ref, k_ref, v_ref)


def _mla_cache_expand(ckv, kr_slab, p):
    n = ckv.shape[0]
    full = lambda shape: pl.BlockSpec(shape, lambda t: tuple(0 for _ in shape))
    return pl.pallas_call(
        _mla_cache_kernel,
        grid=(1,),
        in_specs=[full((n, KV_LORA)), full((n, LANES)), full((KV_LORA, 2 * H_B * NOPE)), full((1, LANES))],
        out_specs=[full((n, 2 * H_B * LANES)), full((n, H_B * V_DIM))],
        out_shape=[
            jax.ShapeDtypeStruct((n, 2 * H_B * LANES), BF16),
            jax.ShapeDtypeStruct((n, H_B * V_DIM), BF16),
        ],
        compiler_params=pltpu.CompilerParams(
            dimension_semantics=("arbitrary",), vmem_limit_bytes=VMEM_LIMIT),
        name="mla_cache_expand",
    )(ckv, kr_slab, p["ukv_w"], p["kn_g"])


def _attn_kernel(q_ref, k_ref, v_ref, *rest, with_ctx):
    if with_ctx:
        k2_ref, v2_ref, o_ref = rest
    else:
        (o_ref,) = rest
    for h in range(H_B):
        hs = slice(2 * h * LANES, (2 * h + 2) * LANES)
        vs = slice(h * V_DIM, (h + 1) * V_DIM)
        q = q_ref[:, hs]
        s = _dot(q, k_ref[:, hs], _NT)
        m = jnp.max(s, axis=-1, keepdims=True)
        if with_ctx:
            s2 = _dot(q, k2_ref[:, hs], _NT)
            m = jnp.maximum(m, jnp.max(s2, axis=-1, keepdims=True))
        pr = jnp.exp(s - m)
        den = jnp.sum(pr, axis=-1, keepdims=True)
        o = _dot(pr.astype(BF16), v_ref[:, vs])
        if with_ctx:
            pr2 = jnp.exp(s2 - m)
            den = den + jnp.sum(pr2, axis=-1, keepdims=True)
            o = o + _dot(pr2.astype(BF16), v2_ref[:, vs])
        o_ref[:, vs] = (o / den).astype(o_ref.dtype)


def _attention(q, k, v, k2=None, v2=None, *, n_batch, seq_len):
    with_ctx = k2 is not None
    q_tiles = seq_len // TM
    kw = 2 * H_B * LANES
    in_specs = [
        pl.BlockSpec((TM, kw), lambda b, t: (b * q_tiles + t, 0)),
        pl.BlockSpec((seq_len, kw), lambda b, t: (b, 0)),
        pl.BlockSpec((seq_len, H_B * V_DIM), lambda b, t: (b, 0)),
    ]
    args = [q, k, v]
    if with_ctx:
        in_specs += [
            pl.BlockSpec((PAST_LEN, kw), lambda b, t: (b, 0)),
            pl.BlockSpec((PAST_LEN, H_B * V_DIM), lambda b, t: (b, 0)),
        ]
        args += [k2, v2]
    return pl.pallas_call(
        functools.partial(_attn_kernel, with_ctx=with_ctx),
        grid=(n_batch, q_tiles),
        in_specs=in_specs,
        out_specs=pl.BlockSpec((TM, H_B * V_DIM), lambda b, t: (b * q_tiles + t, 0)),
        out_shape=jax.ShapeDtypeStruct((n_batch * seq_len, H_B * V_DIM), BF16),
        compiler_params=pltpu.CompilerParams(
            dimension_semantics=("arbitrary", "arbitrary"), vmem_limit_bytes=VMEM_LIMIT),
        name="attn_lat" if with_ctx else "attn_ctx",
    )(*args)


def _outproj_kernel(x_ref, a_ref, b_ref, w_ref, mod_ref, o_ref):
    m = mod_ref[0, 0]
    mix = _dot(a_ref[...], w_ref[:DA, :]) + _dot(b_ref[...], w_ref[DA:, :])
    o_ref[...] = x_ref[...] + m[2:3] * mix


def _outproj(x, a_out, b_out, w, mods, l):
    return pl.pallas_call(
        _outproj_kernel,
        grid=(N_TILES,),
        in_specs=[
            pl.BlockSpec((TM, D_MODEL), lambda t: (t, 0)),
            pl.BlockSpec((TM, DA), lambda t: (t, 0)),
            pl.BlockSpec((TM, DA), lambda t: (t, 0)),
            pl.BlockSpec((D_MODEL, D_MODEL), lambda t: (0, 0)),
            _mod_spec(l),
        ],
        out_specs=pl.BlockSpec((TM, D_MODEL), lambda t: (t, 0)),
        out_shape=jax.ShapeDtypeStruct((N_TOK, D_MODEL), F32),
        compiler_params=pltpu.CompilerParams(
            dimension_semantics=("arbitrary",), vmem_limit_bytes=VMEM_LIMIT),
        name="outproj",
    )(x, a_out, b_out, w, mods)


def _pool_kernel(x_ref, xp_ref, xn_ref, mod_ref, g_ref, pw_ref, ps_ref, o_ref):
    t = pl.program_id(0)
    is_lat = t >= CTX_TILES
    base = jnp.where(is_lat, ((t - CTX_TILES) % LAT_TILES_PER_SEQ) * TM, 0)
    seq_len = jnp.where(is_lat, DEC_SEQ, SEQ)
    m = mod_ref[0, 0]
    norm_mod = lambda x: _rms(x, g_ref[...]) * (1.0 + m[1:2]) + m[0:1]
    x = x_ref[...]
    h = norm_mod(x)
    hp = jnp.where(base > 0, norm_mod(xp_ref[...]), 0.0)
    hn = jnp.where(base + TM < seq_len, norm_mod(xn_ref[...]), 0.0)
    ext = jnp.concatenate([hp, h, hn], axis=0)
    i = lax.broadcasted_iota(jnp.int32, (TM, TM + 2 * HALO), 0) + HALO
    e = lax.broadcasted_iota(jnp.int32, (TM, TM + 2 * HALO), 1)
    pos = base + lax.broadcasted_iota(jnp.int32, (TM, 1), 0)
    for gi, win in enumerate(POOL_WINDOWS):
        half = win // 2
        cols = slice(gi * GP, (gi + 1) * GP)
        band = ((e >= i - half) & (e < i + half)).astype(BF16)
        cnt = (jnp.minimum(pos + half, seq_len) - jnp.maximum(pos - half, 0)).astype(F32)
        d = _dot_exact_lhs(band, ext[:, cols]) / cnt - h[:, cols]
        mp = _dot(d.astype(BF16), pw_ref[gi]) * ps_ref[:, cols]
        o_ref[:, cols] = x[:, cols] + m[2:3, cols] * mp


def _pool(x, mods, l, g, pw, ps):
    r = TM // HALO
    n_halo_blocks = N_TOK // HALO
    return pl.pallas_call(
        _pool_kernel,
        grid=(N_TILES,),
        in_specs=[
            pl.BlockSpec((TM, D_MODEL), lambda t: (t, 0)),
            pl.BlockSpec((HALO, D_MODEL), lambda t: (jnp.maximum(t * r - 1, 0), 0)),
            pl.BlockSpec((HALO, D_MODEL), lambda t: (jnp.minimum((t + 1) * r, n_halo_blocks - 1), 0)),
            _mod_spec(l),
            pl.BlockSpec((1, D_MODEL), lambda t: (0, 0)),
            pl.BlockSpec((len(POOL_WINDOWS), GP, GP), lambda t: (0, 0, 0)),
            pl.BlockSpec((1, D_MODEL), lambda t: (0, 0)),
        ],
        out_specs=pl.BlockSpec((TM, D_MODEL), lambda t: (t, 0)),
        out_shape=jax.ShapeDtypeStruct((N_TOK, D_MODEL), F32),
        compiler_params=pltpu.CompilerParams(
            dimension_semantics=("arbitrary",), vmem_limit_bytes=VMEM_LIMIT),
        name="pool",
    )(x, x, x, mods, g, pw, ps)


def _ffn_kernel(x_ref, mod_ref, g_ref, w1_ref, w3_ref, w2_ref, o_ref, h_s, acc_s):
    j = pl.program_id(1)
    m = mod_ref[0, 0]

    @pl.when(j == 0)
    def _():
        h = _rms(x_ref[...], g_ref[...]) * (1.0 + m[4:5]) + m[3:4]
        h_s[...] = h.astype(BF16)
        acc_s[...] = jnp.zeros_like(acc_s)

    h = h_s[...]
    u = _silu(_dot(h, w1_ref[...])) * _dot(h, w3_ref[...])
    acc_s[...] += _dot(u.astype(BF16), w2_ref[...])

    @pl.when(j == pl.num_programs(1) - 1)
    def _():
        o_ref[...] = x_ref[...] + m[5:6] * acc_s[...]


def _ffn(x, mods, l, g, w1, w3, w2):
    per_tile = TM_FFN // TM
    return pl.pallas_call(
        _ffn_kernel,
        grid=(N_TOK // TM_FFN, D_FF // TF_FFN),
        in_specs=[
            pl.BlockSpec((TM_FFN, D_MODEL), lambda i, j: (i, 0)),
            pl.BlockSpec((1, 1, N_MOD, D_MODEL), lambda i, j: (l, _cond_of_tile(i * per_tile), 0, 0)),
            pl.BlockSpec((1, D_MODEL), lambda i, j: (0, 0)),
            pl.BlockSpec((D_MODEL, TF_FFN), lambda i, j: (0, j)),
            pl.BlockSpec((D_MODEL, TF_FFN), lambda i, j: (0, j)),
            pl.BlockSpec((TF_FFN, D_MODEL), lambda i, j: (j, 0)),
        ],
        out_specs=pl.BlockSpec((TM_FFN, D_MODEL), lambda i, j: (i, 0)),
        out_shape=jax.ShapeDtypeStruct((N_TOK, D_MODEL), F32),
        scratch_shapes=[pltpu.VMEM((TM_FFN, D_MODEL), BF16), pltpu.VMEM((TM_FFN, D_MODEL), F32)],
        compiler_params=pltpu.CompilerParams(
            dimension_semantics=("arbitrary", "arbitrary"), vmem_limit_bytes=VMEM_LIMIT),
        name="ffn",
    )(x, mods, g, w1, w3, w2)


def _rope_tables():
    pos = jnp.arange(DEC_SEQ)
    freqs = ROPE_BASE ** (-jnp.arange(ROPE_AXIS_FREQS, dtype=F32) / ROPE_AXIS_FREQS)
    ang_row = (pos // GRID_W).astype(F32)[:, None] * freqs
    ang_col = (pos % GRID_W).astype(F32)[:, None] * freqs
    zeros = jnp.zeros((DEC_SEQ, LANES - ROPE_DIM), F32)
    cos = jnp.concatenate([jnp.cos(ang_row)] * 2 + [jnp.cos(ang_col)] * 2 + [zeros], axis=1)
    sin = jnp.concatenate([-jnp.sin(ang_row), jnp.sin(ang_row), -jnp.sin(ang_col), jnp.sin(ang_col), zeros], axis=1)
    return cos, sin


def _slab(v):
    return jnp.pad(v, (0, LANES - v.shape[0]))[None]


def _even_params(i, in_w, out_w, shift_mu, w0, w_up, a0, a_up, g_up, k_k, k_a, r_k, ln_g, ln_b,
                 cq_g, uq_w, ckv_g, ukv_w, qn_g, qr_g, kn_g, kr_g):
    row = lambda v: v[None]
    in_w_pad = jnp.pad(in_w[i], ((0, 0), (0, MLA_PAD - MLA_IN))).astype(BF16)
    mu = shift_mu[i]
    w_up_pad = jnp.pad(w_up[i], ((0, 0), (0, A_LORA), (0, 0)))
    a_up_pad = jnp.pad(a_up[i], ((0, 0), (W_LORA, 0), (0, 0)))
    uq = uq_w[i].reshape(Q_LORA, H_B, NOPE + ROPE_DIM)
    uq = jnp.pad(uq, ((0, 0), (0, 0), (0, 2 * LANES - NOPE - ROPE_DIM))).reshape(Q_LORA, 2 * H_B * LANES)
    ukv = ukv_w[i].reshape(KV_LORA, H_B, NOPE + V_DIM)
    ukv = jnp.concatenate([ukv[:, :, :NOPE].reshape(KV_LORA, -1), ukv[:, :, NOPE:].reshape(KV_LORA, -1)], axis=1)
    rw = dict(mu_r=row(mu[:DA]), mu_k=row(mu[DA:2 * DA]), mu_v=row(mu[2 * DA:3 * DA]), mu_l=row(mu[3 * DA:]),
              w0=w0[i], w_up=w_up_pad, a0=a0[i], a_up=a_up_pad, g_up=g_up[i],
              k_k=row(k_k[i]), k_a=row(k_a[i]), r_k=row(r_k[i].reshape(DA)), ln_g=row(ln_g[i]), ln_b=row(ln_b[i]))
    mla = dict(cq_g=row(cq_g[i]), uq_w=uq.astype(BF16), qn_g=row(qn_g[i]), qr_g=_slab(qr_g[i]),
               ckv_g=row(ckv_g[i]), ukv_w=ukv.astype(BF16), kn_g=row(kn_g[i]), kr_g=_slab(kr_g[i]))
    return in_w_pad, out_w[i].astype(BF16), rw, mla


def _state_to_blocks(s):
    b = s.shape[0]
    h = jnp.swapaxes(s, -1, -2).reshape(b, 2, H_A // 2, 2, HD_A, HD_A)
    eye = jnp.eye(2, dtype=s.dtype)
    blk = h[:, :, :, :, :, None, :] * eye[None, None, None, :, None, :, None]
    return jnp.transpose(blk.reshape(b, 2, H_A // 2, LANES, LANES), (0, 2, 1, 3, 4))


def _blocks_to_state(hb):
    b = hb.shape[0]
    x = jnp.transpose(hb, (0, 2, 1, 3, 4)).reshape(b, 2, H_A // 2, 2, HD_A, 2, HD_A)
    diag = jnp.stack([x[:, :, :, 0, :, 0, :], x[:, :, :, 1, :, 1, :]], axis=3)
    return jnp.swapaxes(diag.reshape(b, 2, H_A, HD_A, HD_A), -1, -2)


def kernel(x_prompt, x_sample, c, state_rwkv, cache_mla_ckv, cache_mla_krope, c_ctx,
           norm1_g, norm2_g, ada_w, ada_b, ffn_w1, ffn_w3, ffn_w2,
           in_w, out_w, shift_mu, rwkv_w0, rwkv_w_up, rwkv_a0, rwkv_a_up, rwkv_g_up,
           rwkv_k_k, rwkv_k_a, rwkv_r_k, rwkv_ln_g, rwkv_ln_b,
           mla_cq_g, mla_uq_w, mla_ckv_g, mla_ukv_w, mla_qn_g, mla_qr_g, mla_kn_g, mla_kr_g,
           pool_w, pool_scale):
    x = jnp.concatenate([x_prompt.reshape(N_CTX, D_MODEL), x_sample.reshape(N_LAT, D_MODEL)], axis=0)
    conds = jnp.concatenate([c_ctx[None], c, jnp.zeros((8 - 1 - DEC_BATCH, D_MODEL), F32)], axis=0)
    mods = _adaln(conds, ada_w, ada_b).reshape(DEPTH, 8, N_MOD, D_MODEL)
    rope_tabs = _rope_tables()

    new_s, new_ckv, new_kr = [], [], []
    for l in range(DEPTH):
        n1 = norm1_g[l][None]
        if l % 2 == 0:
            i = l // 2
            in_w_pad, out_w_b, rw, mla = _even_params(
                i, in_w, out_w, shift_mu, rwkv_w0, rwkv_w_up, rwkv_a0, rwkv_a_up, rwkv_g_up,
                rwkv_k_k, rwkv_k_a, rwkv_r_k, rwkv_ln_g, rwkv_ln_b,
                mla_cq_g, mla_uq_w, mla_ckv_g, mla_ukv_w, mla_qn_g, mla_qr_g, mla_kn_g, mla_kr_g)
            proj_r, proj_m = _inproj(x, mods, l, n1, in_w_pad)
            a_ctx, h_ctx = _rwkv(proj_r, rw, None, seq_len=SEQ, n_seq=BATCH, row_off=0)
            a_lat, _ = _rwkv(proj_r, rw, _state_to_blocks(state_rwkv[:, i]),
                             seq_len=DEC_SEQ, n_seq=DEC_BATCH, row_off=N_CTX)
            q_c, k_c, v_c, ckv_c, kr_c = _mla_proj(proj_m, mla, None, n_rows=N_CTX, row_off=0, rope=False)
            q_l, k_l, v_l, _, _ = _mla_proj(proj_m, mla, rope_tabs, n_rows=N_LAT, row_off=N_CTX, rope=True)
            kr_cache = jnp.pad(cache_mla_krope[:, i].reshape(DEC_BATCH * PAST_LEN, ROPE_DIM),
                               ((0, 0), (0, LANES - ROPE_DIM)))
            k_p, v_p = _mla_cache_expand(cache_mla_ckv[:, i].reshape(DEC_BATCH * PAST_LEN, KV_LORA), kr_cache, mla)
            b_ctx = _attention(q_c, k_c, v_c, n_batch=BATCH, seq_len=SEQ)
            b_lat = _attention(q_l, k_l, v_l, k_p, v_p, n_batch=DEC_BATCH, seq_len=DEC_SEQ)
            a_out = jnp.concatenate([a_ctx, a_lat], axis=0)
            b_out = jnp.concatenate([b_ctx, b_lat], axis=0)
            x = _outproj(x, a_out, b_out, out_w_b, mods, l)
            new_s.append(_blocks_to_state(h_ctx))
            new_ckv.append(ckv_c.reshape(BATCH, SEQ, KV_LORA))
            new_kr.append(kr_c.reshape(BATCH, SEQ, ROPE_DIM))
        else:
            j = l // 2
            x = _pool(x, mods, l, n1, pool_w[j].astype(BF16), pool_scale[j][None])
        x = _ffn(x, mods, l, norm2_g[l][None], ffn_w1[l].astype(BF16), ffn_w3[l].astype(BF16),
                 ffn_w2[l].astype(BF16))

    y_p = x[:N_CTX].reshape(BATCH, SEQ, D_MODEL)
    y_s = x[N_CTX:].reshape(DEC_BATCH, DEC_SEQ, D_MODEL)
    return (y_p, y_s, jnp.stack(new_s, axis=1), jnp.stack(new_ckv, axis=1), jnp.stack(new_kr, axis=1))
```

```python
import functools

import jax
import jax.numpy as jnp
from jax import lax
from jax.experimental import pallas as pl
from jax.experimental.pallas import tpu as pltpu

D_MODEL = 1024
BATCH, SEQ = 32, 256
DEC_BATCH, DEC_SEQ = 2, 2048
DEPTH = 4
PAST_LEN = 256
GRID_W = 64
N_MOD = 6
D_FF = 2816
DA = 512
HD_A = 64
H_A = 8
W_LORA, A_LORA, G_LORA = 64, 64, 128
H_B = 4
NOPE, ROPE_DIM, V_DIM = 128, 64, 128
Q_LORA, KV_LORA = 384, 256
ATTN_SCALE = (NOPE + ROPE_DIM) ** -0.5
ROPE_AXIS_FREQS = ROPE_DIM // 4
ROPE_BASE = 10000.0
RWKV_IN = 3 * DA + W_LORA + A_LORA + G_LORA
MLA_IN = Q_LORA + KV_LORA + ROPE_DIM
MLA_PAD = 768
POOL_WINDOWS = (2, 4, 8, 16)
GP = 256
EPS = 1e-6
GN_EPS = 64e-5

N_CTX = BATCH * SEQ
N_LAT = DEC_BATCH * DEC_SEQ
N_TOK = N_CTX + N_LAT

LANES = 128
TM = 256
N_TILES = N_TOK // TM
CTX_TILES = N_CTX // TM
LAT_TILES_PER_SEQ = DEC_SEQ // TM
CHUNK = 64
CHUNKS_PER_TRIP = 4
HALO = 128
TM_FFN = 512
TF_FFN = 1408
VMEM_LIMIT = 56 * 1024 * 1024

F32 = jnp.float32
BF16 = jnp.bfloat16


def _cond_of_tile(t):
    return jnp.where(t < CTX_TILES, 0, 1 + (t - CTX_TILES) // LAT_TILES_PER_SEQ)


def _dot(a, b, dims=((1,), (0,))):
    return lax.dot_general(a, b, (dims, ((), ())), preferred_element_type=F32)


_NT = ((1,), (1,))


def _split2(x):
    hi = x.astype(BF16)
    lo = (x - hi.astype(F32)).astype(BF16)
    return hi, lo


def _split3(x):
    hi = x.astype(BF16)
    r1 = x - hi.astype(F32)
    mid = r1.astype(BF16)
    lo = (r1 - mid.astype(F32)).astype(BF16)
    return hi, mid, lo


def _dot1(a, b, dims=((1,), (0,))):
    return _dot(a.astype(BF16), b.astype(BF16), dims)


def _dot3(a, b, dims=((1,), (0,))):
    ah, al = _split2(a)
    bh, bl = _split2(b)
    return _dot(ah, bh, dims) + (_dot(ah, bl, dims) + _dot(al, bh, dims))


def _dot_exact_lhs(a_bf16, b):
    b0, b1, b2 = _split3(b)
    return _dot(a_bf16, b0) + (_dot(a_bf16, b1) + _dot(a_bf16, b2))


def _rms(x, g):
    return x * lax.rsqrt(jnp.mean(x * x, axis=-1, keepdims=True) + EPS) * g


def _sigmoid(x):
    return 1.0 / (1.0 + jnp.exp(-x))


def _silu(x):
    return x * _sigmoid(x)


def _softplus(x):
    return jnp.maximum(x, 0.0) + jnp.log(1.0 + jnp.exp(-jnp.abs(x)))


def _group_ones(n, g):
    i = lax.broadcasted_iota(jnp.int32, (n, n), 0) // g
    j = lax.broadcasted_iota(jnp.int32, (n, n), 1) // g
    return (i == j).astype(BF16)


def _gsum(x, ones):
    hi, lo = _split2(x)
    return _dot(hi, ones) + _dot(lo, ones)


def _adaln_kernel(c_ref, w_ref, b_ref, o_ref):
    o_ref[0] = _dot3(_silu(c_ref[...]), w_ref[0]) + b_ref[0]


def _adaln(conds, ada_w, ada_b):
    tn = 1536
    return pl.pallas_call(
        _adaln_kernel,
        grid=(DEPTH, N_MOD * D_MODEL // tn),
        in_specs=[
            pl.BlockSpec((8, D_MODEL), lambda l, j: (0, 0)),
            pl.BlockSpec((1, D_MODEL, tn), lambda l, j: (l, 0, j)),
            pl.BlockSpec((1, 1, tn), lambda l, j: (l, 0, j)),
        ],
        out_specs=pl.BlockSpec((1, 8, tn), lambda l, j: (l, 0, j)),
        out_shape=jax.ShapeDtypeStruct((DEPTH, 8, N_MOD * D_MODEL), F32),
        compiler_params=pltpu.CompilerParams(
            dimension_semantics=("arbitrary", "arbitrary"), vmem_limit_bytes=VMEM_LIMIT),
        name="adaln",
    )(conds, ada_w, ada_b.reshape(DEPTH, 1, N_MOD * D_MODEL))


def _mod_spec(l):
    return pl.BlockSpec((1, 1, N_MOD, D_MODEL), lambda t, *_: (l, _cond_of_tile(t), 0, 0))


def _inproj_kernel(x_ref, mod_ref, g_ref, w_ref, or_ref, om_ref):
    m = mod_ref[0, 0]
    h = _rms(x_ref[...], g_ref[...]) * (1.0 + m[1:2]) + m[0:1]
    p = _dot(h.astype(BF16), w_ref[...])
    or_ref[...] = p[:, :RWKV_IN]
    om_ref[...] = p[:, RWKV_IN:]


def _inproj(x, mods, l, g, w):
    return pl.pallas_call(
        _inproj_kernel,
        grid=(N_TILES,),
        in_specs=[
            pl.BlockSpec((TM, D_MODEL), lambda t: (t, 0)),
            _mod_spec(l),
            pl.BlockSpec((1, D_MODEL), lambda t: (0, 0)),
            pl.BlockSpec((D_MODEL, RWKV_IN + MLA_PAD), lambda t: (0, 0)),
        ],
        out_specs=[
            pl.BlockSpec((TM, RWKV_IN), lambda t: (t, 0)),
            pl.BlockSpec((TM, MLA_PAD), lambda t: (t, 0)),
        ],
        out_shape=[
            jax.ShapeDtypeStruct((N_TOK, RWKV_IN), F32),
            jax.ShapeDtypeStruct((N_TOK, MLA_PAD), F32),
        ],
        compiler_params=pltpu.CompilerParams(
            dimension_semantics=("arbitrary",), vmem_limit_bytes=VMEM_LIMIT),
        name="inproj",
    )(x, mods, g, w)


def _rwkv_kernel(pr_ref, pk_ref, pv_ref, pl_ref, mur_ref, muk_ref, muv_ref, mul_ref,
                 w0_ref, wup_ref, a0_ref, aup_ref, gup_ref, kkg_ref, kag_ref, rkg_ref,
                 lng_ref, lnb_ref, *rest, seq_len, zero_init):
    if zero_init:
        h0_ref = None
        out_ref, hfin_ref, m_s, n_s, yq_s, y0_s, g_s, bonus_s = rest
    else:
        h0_ref, out_ref, hfin_ref, m_s, n_s, yq_s, y0_s, g_s, bonus_s = rest
    C = CHUNK
    n_chunks = seq_len // C
    lane = lax.broadcasted_iota(jnp.int32, (1, LANES), 1)
    m0 = (lane < HD_A).astype(F32)
    m1 = 1.0 - m0
    ones_g = _group_ones(LANES, HD_A)
    row = lax.broadcasted_iota(jnp.int32, (C, 1), 0)
    i2 = lax.broadcasted_iota(jnp.int32, (2 * C, 2 * C), 0)
    j2 = lax.broadcasted_iota(jnp.int32, (2 * C, 2 * C), 1)
    same_head = (i2 // C) == (j2 // C)
    eye2 = (i2 == j2).astype(F32)
    ic = lax.broadcasted_iota(jnp.int32, (C, C), 0)
    jc = lax.broadcasted_iota(jnp.int32, (C, C), 1)
    incl_c = ((jc <= ic).astype(BF16), (jc >= ic).astype(BF16))
    incl_blk = (same_head & (j2 <= i2), same_head & (j2 >= i2))
    strict_blk = (same_head & (j2 < i2), same_head & (j2 > i2))

    def pair_masks(d):
        masks = []
        s = 1
        while s < C:
            later, earlier = ((i2 // s) % 2, (j2 // s) % 2) if d == 0 else ((j2 // s) % 2, (i2 // s) % 2)
            masks.append(((i2 // (2 * s)) == (j2 // (2 * s))) & (later == 1) & (earlier == 0))
            s *= 2
        return masks

    pair_blk = (pair_masks(0), pair_masks(1))

    def stack(x):
        return jnp.concatenate([x * m0, x * m1], axis=0)

    def chunk_rows(c):
        return pl.ds(c * C, C) if isinstance(c, int) else pl.ds(pl.multiple_of(c * C, C), C)

    def shifted(ref, mu, c):
        x = ref[chunk_rows(c), :]
        if isinstance(c, int):
            prev_row = ref[c * C - 1:c * C, :] if c > 0 else jnp.zeros((1, x.shape[1]), F32)
            next_row = ref[(c + 1) * C:(c + 1) * C + 1, :] if c < n_chunks - 1 else jnp.zeros((1, x.shape[1]), F32)
        else:
            pstart = pl.multiple_of(jnp.maximum(c * C - 8, 0), 8)
            nstart = pl.multiple_of(jnp.minimum((c + 1) * C, seq_len - 8), 8)
            prev_row = jnp.where(c > 0, ref[pl.ds(pstart, 8), :][7:8], 0.0)
            next_row = jnp.where(c < n_chunks - 1, ref[pl.ds(nstart, 8), :][0:1], 0.0)
        prev = jnp.where(row == 0, prev_row, pltpu.roll(x, 1, 0))
        nxt = jnp.where(row == C - 1, next_row, pltpu.roll(x, C - 1, 0))
        return x + mu * (0.5 * (prev + nxt) - x)

    def chain_mats(chains):
        ds = [ch[0] for ch in chains]
        lp = [_dot_exact_lhs(incl_c[d], logw) for d, *_, logw in chains]
        lpc = [x[C - 1:C] if d == 0 else x[0:1] for d, x in zip(ds, lp)]
        st = []
        for (d, r, kd, v, kk, b, logw), x, xc in zip(chains, lp, lpc):
            p_in, p_ex, p_inv, p_end = jnp.exp(x), jnp.exp(x - logw), jnp.exp(-x), jnp.exp(xc - x)
            st.append(dict(qs=stack(r * p_in), a_s=stack(kk * p_ex), khs=stack(kd * p_inv),
                           bs=stack(b * p_inv), kbs=stack(kd * p_end), bbs=stack(b * p_end), vs=stack(v)))
        lab = [jnp.where(strict_blk[d], _dot1(s["a_s"], s["bs"], _NT), 0.0) for d, s in zip(ds, st)]
        lak = [jnp.where(strict_blk[d], _dot1(s["a_s"], s["khs"], _NT), 0.0) for d, s in zip(ds, st)]
        dqb = [jnp.where(incl_blk[d], _dot1(s["qs"], s["bs"], _NT), 0.0) for d, s in zip(ds, st)]
        dqk = [jnp.where(incl_blk[d], _dot1(s["qs"], s["khs"], _NT), 0.0) for d, s in zip(ds, st)]
        x2 = [_dot1(l, s["vs"]) for l, s in zip(lak, st)]
        tinv = [eye2 - jnp.where(pair_blk[d][0], l, 0.0) for d, l in zip(ds, lab)]
        for lvl in range(1, len(pair_blk[0])):
            half = [_dot1(t, jnp.where(pair_blk[d][lvl], l, 0.0)) for d, t, l in zip(ds, tinv, lab)]
            tinv = [t - _dot1(h, t) for h, t in zip(half, tinv)]
        w12 = [_dot1(t, jnp.concatenate([s["a_s"], x], axis=1)) for t, s, x in zip(tinv, st, x2)]
        mw = [_dot1(s["bbs"].T, w) for s, w in zip(st, w12)]
        dw = [_dot1(q, w) for q, w in zip(dqb, w12)]
        kv = [_dot1(s["kbs"].T, s["vs"]) for s in st]
        qkv = [_dot1(q, s["vs"]) for q, s in zip(dqk, st)]
        return [(eye2 * jnp.exp(xc) - m[:, :LANES], n - m[:, LANES:], s["qs"] - w[:, :LANES], y - w[:, LANES:])
                for xc, m, n, s, w, y in zip(lpc, mw, kv, st, dw, qkv)]

    def load_chunk(c):
        return (shifted(pr_ref, mur_ref[...], c), shifted(pk_ref, muk_ref[...], c),
                shifted(pv_ref, muv_ref[...], c), shifted(pl_ref, mul_ref[...], c))

    def compute_chunks(loaded):
        was = [lo[:, :LANES] for _, _, _, lo in loaded]
        gates = [_dot3(_sigmoid(lo[:, LANES:]), gup_ref[...]) for _, _, _, lo in loaded]
        kkrs = [k * kkg_ref[...] for _, k, _, _ in loaded]
        kks = [kkr / (jnp.sqrt(_gsum(kkr * kkr, ones_g)) + 1e-12) for kkr in kkrs]
        bonuses = [_gsum(r * k * rkg_ref[...], ones_g) * v for r, k, v, _ in loaded]
        twas = [jnp.tanh(wa) for wa in was]
        wls = [[w0_ref[d:d + 1, :] + _dot3(twa, wup_ref[d]) for d in range(2)] for twa in twas]
        als = [[a0_ref[d:d + 1, :] + _dot3(wa, aup_ref[d]) for d in range(2)] for wa in was]
        chains = []
        for (r, k, v, _), kk, wl2, al2 in zip(loaded, kks, wls, als):
            for d in range(2):
                logw = -jnp.exp(-_softplus(-wl2[d]) - 0.5)
                a = _sigmoid(al2[d])
                chains.append((d, r, k * (1.0 + (a - 1.0) * kag_ref[...]), v, kk, kk * a, logw))
        mats = chain_mats(chains)
        return [(g, bo, mats[2 * i:2 * i + 2]) for i, (g, bo) in enumerate(zip(gates, bonuses))]

    def store_chunk(c, gate, bonus, mats):
        g_s[chunk_rows(c), :] = gate
        bonus_s[chunk_rows(c), :] = bonus
        for d in range(2):
            m_s[d, c], n_s[d, c], yq_s[d, c], y0_s[d, c] = mats[d]

    def phase_a_trip(i, carry):
        chunks = [i * CHUNKS_PER_TRIP + u for u in range(CHUNKS_PER_TRIP)]
        results = compute_chunks([load_chunk(c) for c in chunks])
        for c, res in zip(chunks, results):
            store_chunk(c, *res)
        return carry

    if n_chunks == CHUNKS_PER_TRIP:
        phase_a_trip(0, 0)
    else:
        lax.fori_loop(0, n_chunks // CHUNKS_PER_TRIP, phase_a_trip, 0)

    def phase_b(i, states):
        new_states = []
        for d, h, c in ((0, states[0], i), (1, states[1], n_chunks - 1 - i)):
            new_states.append(_dot3(m_s[d, c], h) + n_s[d, c])
            m_s[d, c] = h
        return tuple(new_states)

    if zero_init:
        states = (jnp.zeros((LANES, LANES), F32), jnp.zeros((LANES, LANES), F32))
    else:
        states = (h0_ref[0, 0, 0], h0_ref[0, 0, 1])
    if n_chunks == CHUNKS_PER_TRIP:
        for i in range(n_chunks):
            states = phase_b(i, states)
    else:
        states = lax.fori_loop(0, n_chunks, phase_b, states)
    hfin_ref[0, 0, 0] = states[0]
    hfin_ref[0, 0, 1] = states[1]

    def phase_c_trip(i, carry):
        chunks = [i * CHUNKS_PER_TRIP + u for u in range(CHUNKS_PER_TRIP)]
        y0s = [y0_s[0, c] + y0_s[1, c] for c in chunks]
        extras = [(bonus_s[chunk_rows(c), :], g_s[chunk_rows(c), :]) for c in chunks]
        yfs = [_dot3(yq_s[0, c], m_s[0, c]) for c in chunks]
        ybs = [_dot3(yq_s[1, c], m_s[1, c]) for c in chunks]
        ys = [(f + b + y0)[:C] + (f + b + y0)[C:] for f, b, y0 in zip(yfs, ybs, y0s)]
        mus = [_gsum(y, ones_g) * (1.0 / HD_A) for y in ys]
        ycs = [y - mu for y, mu in zip(ys, mus)]
        vars_ = [_gsum(yc * yc, ones_g) * (1.0 / HD_A) for yc in ycs]
        for c, yc, var, (bonus, gate) in zip(chunks, ycs, vars_, extras):
            yn = yc * lax.rsqrt(var + GN_EPS) * lng_ref[...] + lnb_ref[...]
            out_ref[chunk_rows(c), :] = ((yn + bonus) * gate).astype(out_ref.dtype)
        return carry

    if n_chunks == CHUNKS_PER_TRIP:
        phase_c_trip(0, 0)
    else:
        lax.fori_loop(0, n_chunks // CHUNKS_PER_TRIP, phase_c_trip, 0)


def _rwkv(proj_r, p, h0, *, seq_len, n_seq, row_off):
    n_chunks = seq_len // CHUNK
    off = row_off // seq_len
    n_pairs = DA // LANES
    zero_init = h0 is None
    col = lambda cb: (lambda s, hp: (off + s, cb + hp))
    vec = lambda: pl.BlockSpec((1, LANES), lambda s, hp: (0, hp))
    in_specs = [
        pl.BlockSpec((seq_len, LANES), col(0)),
        pl.BlockSpec((seq_len, LANES), col(n_pairs)),
        pl.BlockSpec((seq_len, LANES), col(2 * n_pairs)),
        pl.BlockSpec((seq_len, 2 * LANES), lambda s, hp: (off + s, 3 * n_pairs // 2)),
        vec(), vec(), vec(),
        pl.BlockSpec((1, 2 * LANES), lambda s, hp: (0, 0)),
        pl.BlockSpec((2, LANES), lambda s, hp: (0, hp)),
        pl.BlockSpec((2, LANES, LANES), lambda s, hp: (0, 0, hp)),
        pl.BlockSpec((2, LANES), lambda s, hp: (0, hp)),
        pl.BlockSpec((2, LANES, LANES), lambda s, hp: (0, 0, hp)),
        pl.BlockSpec((G_LORA, LANES), lambda s, hp: (0, hp)),
        vec(), vec(), vec(), vec(), vec(),
    ]
    args = [proj_r, proj_r, proj_r, proj_r, p["mu_r"], p["mu_k"], p["mu_v"], p["mu_l"],
            p["w0"], p["w_up"], p["a0"], p["a_up"], p["g_up"], p["k_k"], p["k_a"], p["r_k"],
            p["ln_g"], p["ln_b"]]
    if not zero_init:
        in_specs.append(pl.BlockSpec((1, 1, 2, LANES, LANES), lambda s, hp: (s, hp, 0, 0, 0)))
        args.append(h0)
    blk = (2 * CHUNK, LANES)
    return pl.pallas_call(
        functools.partial(_rwkv_kernel, seq_len=seq_len, zero_init=zero_init),
        grid=(n_seq, n_pairs),
        in_specs=in_specs,
        out_specs=[
            pl.BlockSpec((seq_len, LANES), lambda s, hp: (s, hp)),
            pl.BlockSpec((1, 1, 2, LANES, LANES), lambda s, hp: (s, hp, 0, 0, 0)),
        ],
        out_shape=[
            jax.ShapeDtypeStruct((n_seq * seq_len, DA), BF16),
            jax.ShapeDtypeStruct((n_seq, n_pairs, 2, LANES, LANES), F32),
        ],
        scratch_shapes=[
            pltpu.VMEM((2, n_chunks) + blk, F32),
            pltpu.VMEM((2, n_chunks) + blk, F32),
            pltpu.VMEM((2, n_chunks) + blk, F32),
            pltpu.VMEM((2, n_chunks) + blk, F32),
            pltpu.VMEM((seq_len, LANES), F32),
            pltpu.VMEM((seq_len, LANES), F32),
        ],
        compiler_params=pltpu.CompilerParams(
            dimension_semantics=("arbitrary", "arbitrary"), vmem_limit_bytes=VMEM_LIMIT),
        name="rwkv_ctx" if zero_init else "rwkv_lat",
    )(*args)


def _rope(x, cos, sin):
    lane = lax.broadcasted_iota(jnp.int32, x.shape, 1)
    swapped = jnp.where((lane % 32) < 16, pltpu.roll(x, LANES - 16, 1), pltpu.roll(x, 16, 1))
    return x * cos + swapped * sin


def _slab_rms(x, g):
    return x * lax.rsqrt(jnp.sum(x * x, axis=-1, keepdims=True) * (1.0 / ROPE_DIM) + EPS) * g


def _kv_expand(ckv_n, kr_att, ukv_ref, kng_ref, k_ref, v_ref):
    kv = _dot(ckv_n.astype(BF16), ukv_ref[...])
    kr_b = kr_att.astype(BF16)
    for h in range(H_B):
        kn = _rms(kv[:, h * NOPE:(h + 1) * NOPE], kng_ref[...])
        k_ref[:, 2 * h * LANES:(2 * h + 1) * LANES] = kn.astype(BF16)
        k_ref[:, (2 * h + 1) * LANES:(2 * h + 2) * LANES] = kr_b
    v_ref[...] = kv[:, H_B * NOPE:].astype(BF16)


def _mla_proj_kernel(pm_ref, cqg_ref, uq_ref, qng_ref, qrg_ref, ckvg_ref, ukv_ref, kng_ref, krg_ref,
                     *rest, rope):
    if rope:
        cos_ref, sin_ref, q_ref, k_ref, v_ref, ckv_ref, kr_ref = rest
    else:
        q_ref, k_ref, v_ref, ckv_ref, kr_ref = rest
    pm = pm_ref[...]
    cq = _rms(pm[:, :Q_LORA], cqg_ref[...])
    q = _dot(cq.astype(BF16), uq_ref[...])
    for h in range(H_B):
        qn = _rms(q[:, 2 * h * LANES:(2 * h + 1) * LANES], qng_ref[...])
        qr = _slab_rms(q[:, (2 * h + 1) * LANES:(2 * h + 2) * LANES], qrg_ref[...])
        if rope:
            qr = _rope(qr, cos_ref[...], sin_ref[...])
        q_ref[:, 2 * h * LANES:(2 * h + 1) * LANES] = (qn * ATTN_SCALE).astype(BF16)
        q_ref[:, (2 * h + 1) * LANES:(2 * h + 2) * LANES] = (qr * ATTN_SCALE).astype(BF16)
    ckv_n = _rms(pm[:, Q_LORA:Q_LORA + KV_LORA], ckvg_ref[...])
    ckv_ref[...] = ckv_n
    kr = _slab_rms(pm[:, Q_LORA + KV_LORA:], krg_ref[...])
    kr_ref[...] = kr[:, :ROPE_DIM]
    kr_att = _rope(kr, cos_ref[...], sin_ref[...]) if rope else kr
    _kv_expand(ckv_n, kr_att, ukv_ref, kng_ref, k_ref, v_ref)


def _mla_proj(proj_m, p, rope_tabs, *, n_rows, row_off, rope):
    t0 = row_off // TM
    full = lambda shape: pl.BlockSpec(shape, lambda t: tuple(0 for _ in shape))
    in_specs = [
        pl.BlockSpec((TM, MLA_PAD), lambda t: (t0 + t, 0)),
        full((1, Q_LORA)), full((Q_LORA, 2 * H_B * LANES)), full((1, LANES)), full((1, LANES)),
        full((1, KV_LORA)), full((KV_LORA, 2 * H_B * NOPE)), full((1, LANES)), full((1, LANES)),
    ]
    args = [proj_m, p["cq_g"], p["uq_w"], p["qn_g"], p["qr_g"], p["ckv_g"], p["ukv_w"], p["kn_g"], p["kr_g"]]
    if rope:
        per_seq = DEC_SEQ // TM
        in_specs += [pl.BlockSpec((TM, LANES), lambda t: (t % per_seq, 0))] * 2
        args += list(rope_tabs)
    rows = lambda w: pl.BlockSpec((TM, w), lambda t: (t, 0))
    return pl.pallas_call(
        functools.partial(_mla_proj_kernel, rope=rope),
        grid=(n_rows // TM,),
        in_specs=in_specs,
        out_specs=[rows(2 * H_B * LANES), rows(2 * H_B * LANES), rows(H_B * V_DIM), rows(KV_LORA), rows(ROPE_DIM)],
        out_shape=[
            jax.ShapeDtypeStruct((n_rows, 2 * H_B * LANES), BF16),
            jax.ShapeDtypeStruct((n_rows, 2 * H_B * LANES), BF16),
            jax.ShapeDtypeStruct((n_rows, H_B * V_DIM), BF16),
            jax.ShapeDtypeStruct((n_rows, KV_LORA), F32),
            jax.ShapeDtypeStruct((n_rows, ROPE_DIM), F32),
        ],
        compiler_params=pltpu.CompilerParams(
            dimension_semantics=("arbitrary",), vmem_limit_bytes=VMEM_LIMIT),
        name="mla_proj_lat" if rope else "mla_proj_ctx",
    )(*args)


def _mla_cache_kernel(ckv_ref, kr_ref, ukv_ref, kng_ref, k_ref, v_ref):
    _kv_expand(ckv_ref[...], kr_ref[...], ukv_ref, kng_ref, k_ref, v_ref)


def _mla_cache_expand(ckv, kr_slab, p):
    n = ckv.shape[0]
    full = lambda shape: pl.BlockSpec(shape, lambda t: tuple(0 for _ in shape))
    return pl.pallas_call(
        _mla_cache_kernel,
        grid=(1,),
        in_specs=[full((n, KV_LORA)), full((n, LANES)), full((KV_LORA, 2 * H_B * NOPE)), full((1, LANES))],
        out_specs=[full((n, 2 * H_B * LANES)), full((n, H_B * V_DIM))],
        out_shape=[
            jax.ShapeDtypeStruct((n, 2 * H_B * LANES), BF16),
            jax.ShapeDtypeStruct((n, H_B * V_DIM), BF16),
        ],
        compiler_params=pltpu.CompilerParams(
            dimension_semantics=("arbitrary",), vmem_limit_bytes=VMEM_LIMIT),
        name="mla_cache_expand",
    )(ckv, kr_slab, p["ukv_w"], p["kn_g"])


def _attn_kernel(q_ref, k_ref, v_ref, *rest, with_ctx):
    if with_ctx:
        k2_ref, v2_ref, o_ref = rest
    else:
        (o_ref,) = rest
    for h in range(H_B):
        hs = slice(2 * h * LANES, (2 * h + 2) * LANES)
        vs = slice(h * V_DIM, (h + 1) * V_DIM)
        q = q_ref[:, hs]
        s = _dot(q, k_ref[:, hs], _NT)
        m = jnp.max(s, axis=-1, keepdims=True)
        if with_ctx:
            s2 = _dot(q, k2_ref[:, hs], _NT)
            m = jnp.maximum(m, jnp.max(s2, axis=-1, keepdims=True))
        pr = jnp.exp(s - m)
        den = jnp.sum(pr, axis=-1, keepdims=True)
        o = _dot(pr.astype(BF16), v_ref[:, vs])
        if with_ctx:
            pr2 = jnp.exp(s2 - m)
            den = den + jnp.sum(pr2, axis=-1, keepdims=True)
            o = o + _dot(pr2.astype(BF16), v2_ref[:, vs])
        o_ref[:, vs] = (o / den).astype(o_ref.dtype)


def _attention(q, k, v, k2=None, v2=None, *, n_batch, seq_len):
    with_ctx = k2 is not None
    q_tiles = seq_len // TM
    kw = 2 * H_B * LANES
    in_specs = [
        pl.BlockSpec((TM, kw), lambda b, t: (b * q_tiles + t, 0)),
        pl.BlockSpec((seq_len, kw), lambda b, t: (b, 0)),
        pl.BlockSpec((seq_len, H_B * V_DIM), lambda b, t: (b, 0)),
    ]
    args = [q, k, v]
    if with_ctx:
        in_specs += [
            pl.BlockSpec((PAST_LEN, kw), lambda b, t: (b, 0)),
            pl.BlockSpec((PAST_LEN, H_B * V_DIM), lambda b, t: (b, 0)),
        ]
        args += [k2, v2]
    return pl.pallas_call(
        functools.partial(_attn_kernel, with_ctx=with_ctx),
        grid=(n_batch, q_tiles),
        in_specs=in_specs,
        out_specs=pl.BlockSpec((TM, H_B * V_DIM), lambda b, t: (b * q_tiles + t, 0)),
        out_shape=jax.ShapeDtypeStruct((n_batch * seq_len, H_B * V_DIM), BF16),
        compiler_params=pltpu.CompilerParams(
            dimension_semantics=("arbitrary", "arbitrary"), vmem_limit_bytes=VMEM_LIMIT),
        name="attn_lat" if with_ctx else "attn_ctx",
    )(*args)


def _outproj_kernel(x_ref, a_ref, b_ref, w_ref, mod_ref, o_ref):
    m = mod_ref[0, 0]
    mix = _dot(a_ref[...], w_ref[:DA, :]) + _dot(b_ref[...], w_ref[DA:, :])
    o_ref[...] = x_ref[...] + m[2:3] * mix


def _outproj(x, a_out, b_out, w, mods, l):
    return pl.pallas_call(
        _outproj_kernel,
        grid=(N_TILES,),
        in_specs=[
            pl.BlockSpec((TM, D_MODEL), lambda t: (t, 0)),
            pl.BlockSpec((TM, DA), lambda t: (t, 0)),
            pl.BlockSpec((TM, DA), lambda t: (t, 0)),
            pl.BlockSpec((D_MODEL, D_MODEL), lambda t: (0, 0)),
            _mod_spec(l),
        ],
        out_specs=pl.BlockSpec((TM, D_MODEL), lambda t: (t, 0)),
        out_shape=jax.ShapeDtypeStruct((N_TOK, D_MODEL), F32),
        compiler_params=pltpu.CompilerParams(
            dimension_semantics=("arbitrary",), vmem_limit_bytes=VMEM_LIMIT),
        name="outproj",
    )(x, a_out, b_out, w, mods)


def _pool_kernel(x_ref, xp_ref, xn_ref, mod_ref, g_ref, pw_ref, ps_ref, o_ref):
    t = pl.program_id(0)
    is_lat = t >= CTX_TILES
    base = jnp.where(is_lat, ((t - CTX_TILES) % LAT_TILES_PER_SEQ) * TM, 0)
    seq_len = jnp.where(is_lat, DEC_SEQ, SEQ)
    m = mod_ref[0, 0]
    norm_mod = lambda x: _rms(x, g_ref[...]) * (1.0 + m[1:2]) + m[0:1]
    x = x_ref[...]
    h = norm_mod(x)
    hp = jnp.where(base > 0, norm_mod(xp_ref[...]), 0.0)
    hn = jnp.where(base + TM < seq_len, norm_mod(xn_ref[...]), 0.0)
    ext = jnp.concatenate([hp, h, hn], axis=0)
    i = lax.broadcasted_iota(jnp.int32, (TM, TM + 2 * HALO), 0) + HALO
    e = lax.broadcasted_iota(jnp.int32, (TM, TM + 2 * HALO), 1)
    pos = base + lax.broadcasted_iota(jnp.int32, (TM, 1), 0)
    for gi, win in enumerate(POOL_WINDOWS):
        half = win // 2
        cols = slice(gi * GP, (gi + 1) * GP)
        band = ((e >= i - half) & (e < i + half)).astype(BF16)
        cnt = (jnp.minimum(pos + half, seq_len) - jnp.maximum(pos - half, 0)).astype(F32)
        d = _dot_exact_lhs(band, ext[:, cols]) / cnt - h[:, cols]
        mp = _dot(d.astype(BF16), pw_ref[gi]) * ps_ref[:, cols]
        o_ref[:, cols] = x[:, cols] + m[2:3, cols] * mp


def _pool(x, mods, l, g, pw, ps):
    r = TM // HALO
    n_halo_blocks = N_TOK // HALO
    return pl.pallas_call(
        _pool_kernel,
        grid=(N_TILES,),
        in_specs=[
            pl.BlockSpec((TM, D_MODEL), lambda t: (t, 0)),
            pl.BlockSpec((HALO, D_MODEL), lambda t: (jnp.maximum(t * r - 1, 0), 0)),
            pl.BlockSpec((HALO, D_MODEL), lambda t: (jnp.minimum((t + 1) * r, n_halo_blocks - 1), 0)),
            _mod_spec(l),
            pl.BlockSpec((1, D_MODEL), lambda t: (0, 0)),
            pl.BlockSpec((len(POOL_WINDOWS), GP, GP), lambda t: (0, 0, 0)),
            pl.BlockSpec((1, D_MODEL), lambda t: (0, 0)),
        ],
        out_specs=pl.BlockSpec((TM, D_MODEL), lambda t: (t, 0)),
        out_shape=jax.ShapeDtypeStruct((N_TOK, D_MODEL), F32),
        compiler_params=pltpu.CompilerParams(
            dimension_semantics=("arbitrary",), vmem_limit_bytes=VMEM_LIMIT),
        name="pool",
    )(x, x, x, mods, g, pw, ps)


def _ffn_kernel(x_ref, mod_ref, g_ref, w1_ref, w3_ref, w2_ref, o_ref, h_s, acc_s):
    j = pl.program_id(1)
    m = mod_ref[0, 0]

    @pl.when(j == 0)
    def _():
        h = _rms(x_ref[...], g_ref[...]) * (1.0 + m[4:5]) + m[3:4]
        h_s[...] = h.astype(BF16)
        acc_s[...] = jnp.zeros_like(acc_s)

    h = h_s[...]
    u = _silu(_dot(h, w1_ref[...])) * _dot(h, w3_ref[...])
    acc_s[...] += _dot(u.astype(BF16), w2_ref[...])

    @pl.when(j == pl.num_programs(1) - 1)
    def _():
        o_ref[...] = x_ref[...] + m[5:6] * acc_s[...]


def _ffn(x, mods, l, g, w1, w3, w2):
    per_tile = TM_FFN // TM
    return pl.pallas_call(
        _ffn_kernel,
        grid=(N_TOK // TM_FFN, D_FF // TF_FFN),
        in_specs=[
            pl.BlockSpec((TM_FFN, D_MODEL), lambda i, j: (i, 0)),
            pl.BlockSpec((1, 1, N_MOD, D_MODEL), lambda i, j: (l, _cond_of_tile(i * per_tile), 0, 0)),
            pl.BlockSpec((1, D_MODEL), lambda i, j: (0, 0)),
            pl.BlockSpec((D_MODEL, TF_FFN), lambda i, j: (0, j)),
            pl.BlockSpec((D_MODEL, TF_FFN), lambda i, j: (0, j)),
            pl.BlockSpec((TF_FFN, D_MODEL), lambda i, j: (j, 0)),
        ],
        out_specs=pl.BlockSpec((TM_FFN, D_MODEL), lambda i, j: (i, 0)),
        out_shape=jax.ShapeDtypeStruct((N_TOK, D_MODEL), F32),
        scratch_shapes=[pltpu.VMEM((TM_FFN, D_MODEL), BF16), pltpu.VMEM((TM_FFN, D_MODEL), F32)],
        compiler_params=pltpu.CompilerParams(
            dimension_semantics=("arbitrary", "arbitrary"), vmem_limit_bytes=VMEM_LIMIT),
        name="ffn",
    )(x, mods, g, w1, w3, w2)


def _rope_tables():
    pos = jnp.arange(DEC_SEQ)
    freqs = ROPE_BASE ** (-jnp.arange(ROPE_AXIS_FREQS, dtype=F32) / ROPE_AXIS_FREQS)
    ang_row = (pos // GRID_W).astype(F32)[:, None] * freqs
    ang_col = (pos % GRID_W).astype(F32)[:, None] * freqs
    zeros = jnp.zeros((DEC_SEQ, LANES - ROPE_DIM), F32)
    cos = jnp.concatenate([jnp.cos(ang_row)] * 2 + [jnp.cos(ang_col)] * 2 + [zeros], axis=1)
    sin = jnp.concatenate([-jnp.sin(ang_row), jnp.sin(ang_row), -jnp.sin(ang_col), jnp.sin(ang_col), zeros], axis=1)
    return cos, sin


def _slab(v):
    return jnp.pad(v, (0, LANES - v.shape[0]))[None]


def _even_params(i, in_w, out_w, shift_mu, w0, w_up, a0, a_up, g_up, k_k, k_a, r_k, ln_g, ln_b,
                 cq_g, uq_w, ckv_g, ukv_w, qn_g, qr_g, kn_g, kr_g):
    row = lambda v: v[None]
    in_w_pad = jnp.pad(in_w[i], ((0, 0), (0, MLA_PAD - MLA_IN))).astype(BF16)
    mu = shift_mu[i]
    w_up_pad = jnp.pad(w_up[i], ((0, 0), (0, A_LORA), (0, 0)))
    a_up_pad = jnp.pad(a_up[i], ((0, 0), (W_LORA, 0), (0, 0)))
    uq = uq_w[i].reshape(Q_LORA, H_B, NOPE + ROPE_DIM)
    uq = jnp.pad(uq, ((0, 0), (0, 0), (0, 2 * LANES - NOPE - ROPE_DIM))).reshape(Q_LORA, 2 * H_B * LANES)
    ukv = ukv_w[i].reshape(KV_LORA, H_B, NOPE + V_DIM)
    ukv = jnp.concatenate([ukv[:, :, :NOPE].reshape(KV_LORA, -1), ukv[:, :, NOPE:].reshape(KV_LORA, -1)], axis=1)
    rw = dict(mu_r=row(mu[:DA]), mu_k=row(mu[DA:2 * DA]), mu_v=row(mu[2 * DA:3 * DA]), mu_l=row(mu[3 * DA:]),
              w0=w0[i], w_up=w_up_pad, a0=a0[i], a_up=a_up_pad, g_up=g_up[i],
              k_k=row(k_k[i]), k_a=row(k_a[i]), r_k=row(r_k[i].reshape(DA)), ln_g=row(ln_g[i]), ln_b=row(ln_b[i]))
    mla = dict(cq_g=row(cq_g[i]), uq_w=uq.astype(BF16), qn_g=row(qn_g[i]), qr_g=_slab(qr_g[i]),
               ckv_g=row(ckv_g[i]), ukv_w=ukv.astype(BF16), kn_g=row(kn_g[i]), kr_g=_slab(kr_g[i]))
    return in_w_pad, out_w[i].astype(BF16), rw, mla


def _state_to_blocks(s):
    b = s.shape[0]
    h = jnp.swapaxes(s, -1, -2).reshape(b, 2, H_A // 2, 2, HD_A, HD_A)
    eye = jnp.eye(2, dtype=s.dtype)
    blk = h[:, :, :, :, :, None, :] * eye[None, None, None, :, None, :, None]
    return jnp.transpose(blk.reshape(b, 2, H_A // 2, LANES, LANES), (0, 2, 1, 3, 4))


def _blocks_to_state(hb):
    b = hb.shape[0]
    x = jnp.transpose(hb, (0, 2, 1, 3, 4)).reshape(b, 2, H_A // 2, 2, HD_A, 2, HD_A)
    diag = jnp.stack([x[:, :, :, 0, :, 0, :], x[:, :, :, 1, :, 1, :]], axis=3)
    return jnp.swapaxes(diag.reshape(b, 2, H_A, HD_A, HD_A), -1, -2)


def kernel(x_prompt, x_sample, c, state_rwkv, cache_mla_ckv, cache_mla_krope, c_ctx,
           norm1_g, norm2_g, ada_w, ada_b, ffn_w1, ffn_w3, ffn_w2,
           in_w, out_w, shift_mu, rwkv_w0, rwkv_w_up, rwkv_a0, rwkv_a_up, rwkv_g_up,
           rwkv_k_k, rwkv_k_a, rwkv_r_k, rwkv_ln_g, rwkv_ln_b,
           mla_cq_g, mla_uq_w, mla_ckv_g, mla_ukv_w, mla_qn_g, mla_qr_g, mla_kn_g, mla_kr_g,
           pool_w, pool_scale):
    x = jnp.concatenate([x_prompt.reshape(N_CTX, D_MODEL), x_sample.reshape(N_LAT, D_MODEL)], axis=0)
    conds = jnp.concatenate([c_ctx[None], c, jnp.zeros((8 - 1 - DEC_BATCH, D_MODEL), F32)], axis=0)
    mods = _adaln(conds, ada_w, ada_b).reshape(DEPTH, 8, N_MOD, D_MODEL)
    rope_tabs = _rope_tables()

    new_s, new_ckv, new_kr = [], [], []
    for l in range(DEPTH):
        n1 = norm1_g[l][None]
        if l % 2 == 0:
            i = l // 2
            in_w_pad, out_w_b, rw, mla = _even_params(
                i, in_w, out_w, shift_mu, rwkv_w0, rwkv_w_up, rwkv_a0, rwkv_a_up, rwkv_g_up,
                rwkv_k_k, rwkv_k_a, rwkv_r_k, rwkv_ln_g, rwkv_ln_b,
                mla_cq_g, mla_uq_w, mla_ckv_g, mla_ukv_w, mla_qn_g, mla_qr_g, mla_kn_g, mla_kr_g)
            proj_r, proj_m = _inproj(x, mods, l, n1, in_w_pad)
            a_ctx, h_ctx = _rwkv(proj_r, rw, None, seq_len=SEQ, n_seq=BATCH, row_off=0)
            a_lat, _ = _rwkv(proj_r, rw, _state_to_blocks(state_rwkv[:, i]),
                             seq_len=DEC_SEQ, n_seq=DEC_BATCH, row_off=N_CTX)
            q_c, k_c, v_c, ckv_c, kr_c = _mla_proj(proj_m, mla, None, n_rows=N_CTX, row_off=0, rope=False)
            q_l, k_l, v_l, _, _ = _mla_proj(proj_m, mla, rope_tabs, n_rows=N_LAT, row_off=N_CTX, rope=True)
            kr_cache = jnp.pad(cache_mla_krope[:, i].reshape(DEC_BATCH * PAST_LEN, ROPE_DIM),
                               ((0, 0), (0, LANES - ROPE_DIM)))
            k_p, v_p = _mla_cache_expand(cache_mla_ckv[:, i].reshape(DEC_BATCH * PAST_LEN, KV_LORA), kr_cache, mla)
            b_ctx = _attention(q_c, k_c, v_c, n_batch=BATCH, seq_len=SEQ)
            b_lat = _attention(q_l, k_l, v_l, k_p, v_p, n_batch=DEC_BATCH, seq_len=DEC_SEQ)
            a_out = jnp.concatenate([a_ctx, a_lat], axis=0)
            b_out = jnp.concatenate([b_ctx, b_lat], axis=0)
            x = _outproj(x, a_out, b_out, out_w_b, mods, l)
            new_s.append(_blocks_to_state(h_ctx))
            new_ckv.append(ckv_c.reshape(BATCH, SEQ, KV_LORA))
            new_kr.append(kr_c.reshape(BATCH, SEQ, ROPE_DIM))
        else:
            j = l // 2
            x = _pool(x, mods, l, n1, pool_w[j].astype(BF16), pool_scale[j][None])
        x = _ffn(x, mods, l, norm2_g[l][None], ffn_w1[l].astype(BF16), ffn_w3[l].astype(BF16),
                 ffn_w2[l].astype(BF16))

    y_p = x[:N_CTX].reshape(BATCH, SEQ, D_MODEL)
    y_s = x[N_CTX:].reshape(DEC_BATCH, DEC_SEQ, D_MODEL)
    return (y_p, y_s, jnp.stack(new_s, axis=1), jnp.stack(new_ckv, axis=1), jnp.stack(new_kr, axis=1))
```

```python
import functools

import jax
import jax.numpy as jnp
from jax import lax
from jax.experimental import pallas as pl
from jax.experimental.pallas import tpu as pltpu

D_MODEL = 1024
BATCH, SEQ = 32, 256
DEC_BATCH, DEC_SEQ = 2, 2048
DEPTH = 4
PAST_LEN = 256
GRID_W = 64
N_MOD = 6
D_FF = 2816
DA = 512
HD_A = 64
H_A = 8
W_LORA, A_LORA, G_LORA = 64, 64, 128
H_B = 4
NOPE, ROPE_DIM, V_DIM = 128, 64, 128
Q_LORA, KV_LORA = 384, 256
ATTN_SCALE = (NOPE + ROPE_DIM) ** -0.5
ROPE_AXIS_FREQS = ROPE_DIM // 4
ROPE_BASE = 10000.0
RWKV_IN = 3 * DA + W_LORA + A_LORA + G_LORA
MLA_IN = Q_LORA + KV_LORA + ROPE_DIM
MLA_PAD = 768
POOL_WINDOWS = (2, 4, 8, 16)
GP = 256
EPS = 1e-6
GN_EPS = 64e-5

N_CTX = BATCH * SEQ
N_LAT = DEC_BATCH * DEC_SEQ
N_TOK = N_CTX + N_LAT

LANES = 128
TM = 256
N_TILES = N_TOK // TM
CTX_TILES = N_CTX // TM
LAT_TILES_PER_SEQ = DEC_SEQ // TM
CHUNK = 64
CHUNKS_PER_TRIP = 8
HALO = 8
POOL_K = 384
TM_FFN = 512
MXU_WIDTH = 256
FF_SPLITS = ((0, 6 * MXU_WIDTH), (6 * MXU_WIDTH, D_FF))
VMEM_LIMIT = 56 * 1024 * 1024

F32 = jnp.float32
BF16 = jnp.bfloat16


def _cond_of_tile(t):
    return jnp.where(t < CTX_TILES, 0, 1 + (t - CTX_TILES) // LAT_TILES_PER_SEQ)


def _dot(a, b, dims=((1,), (0,))):
    return lax.dot_general(a, b, (dims, ((), ())), preferred_element_type=F32)


_NT = ((1,), (1,))


def _split2(x):
    hi = x.astype(BF16)
    lo = (x - hi.astype(F32)).astype(BF16)
    return hi, lo


def _dot1(a, b, dims=((1,), (0,))):
    return _dot(a.astype(BF16), b.astype(BF16), dims)


def _dot3(a, b, dims=((1,), (0,))):
    ah, al = _split2(a)
    bh, bl = _split2(b)
    return _dot(ah, bh, dims) + (_dot(ah, bl, dims) + _dot(al, bh, dims))


def _dot_exact_lhs(a_bf16, b):
    b0, b1 = _split2(b)
    return _dot(a_bf16, b0) + _dot(a_bf16, b1)


def _rms(x, g):
    return x * lax.rsqrt(jnp.mean(x * x, axis=-1, keepdims=True) + EPS) * g


def _sigmoid(x):
    return 1.0 / (1.0 + jnp.exp(-x))


def _silu(x):
    return x * _sigmoid(x)


def _softplus(x):
    return jnp.maximum(x, 0.0) + jnp.log(1.0 + jnp.exp(-jnp.abs(x)))


def _group_ones(n, g):
    i = lax.broadcasted_iota(jnp.int32, (n, n), 0) // g
    j = lax.broadcasted_iota(jnp.int32, (n, n), 1) // g
    return (i == j).astype(BF16)


def _gsum(x, ones):
    hi, lo = _split2(x)
    return _dot(hi, ones) + _dot(lo, ones)


def _adaln_kernel(c_ref, w_ref, b_ref, o_ref):
    o_ref[0] = _dot3(_silu(c_ref[...]), w_ref[0]) + b_ref[0]


def _adaln(conds, ada_w, ada_b):
    tn = 1536
    return pl.pallas_call(
        _adaln_kernel,
        grid=(DEPTH, N_MOD * D_MODEL // tn),
        in_specs=[
            pl.BlockSpec((8, D_MODEL), lambda l, j: (0, 0)),
            pl.BlockSpec((1, D_MODEL, tn), lambda l, j: (l, 0, j)),
            pl.BlockSpec((1, 1, tn), lambda l, j: (l, 0, j)),
        ],
        out_specs=pl.BlockSpec((1, 8, tn), lambda l, j: (l, 0, j)),
        out_shape=jax.ShapeDtypeStruct((DEPTH, 8, N_MOD * D_MODEL), F32),
        compiler_params=pltpu.CompilerParams(
            dimension_semantics=("arbitrary", "arbitrary"), vmem_limit_bytes=VMEM_LIMIT),
        name="adaln",
    )(conds, ada_w, ada_b.reshape(DEPTH, 1, N_MOD * D_MODEL))


def _mod_spec(l):
    return pl.BlockSpec((1, 1, N_MOD, D_MODEL), lambda t, *_: (l, _cond_of_tile(t), 0, 0))


def _row_specs(x, width):
    if not isinstance(x, tuple):
        return [pl.BlockSpec((TM, width), lambda t, *_: (t, 0))], [x]
    return [pl.BlockSpec((TM, width), lambda t, *_: (jnp.minimum(t, CTX_TILES - 1), 0)),
            pl.BlockSpec((TM, width), lambda t, *_: (jnp.maximum(t - CTX_TILES, 0), 0))], list(x)


def _row_tile(refs):
    if len(refs) == 1:
        return refs[0][...]
    return jnp.where(pl.program_id(0) < CTX_TILES, refs[0][...], refs[1][...])


def _inproj_kernel(*refs):
    *x_refs, mod_ref, g_ref, w_ref, or_ref, om_ref = refs
    m = mod_ref[0, 0]
    h = _rms(_row_tile(x_refs), g_ref[...]) * (1.0 + m[1:2]) + m[0:1]
    p = _dot(h.astype(BF16), w_ref[...])
    or_ref[...] = p[:, :RWKV_IN]
    om_ref[...] = p[:, RWKV_IN:]


def _inproj(x, mods, l, g, w):
    x_specs, x_args = _row_specs(x, D_MODEL)
    return pl.pallas_call(
        _inproj_kernel,
        grid=(N_TILES,),
        in_specs=x_specs + [
            _mod_spec(l),
            pl.BlockSpec((1, D_MODEL), lambda t: (0, 0)),
            pl.BlockSpec((D_MODEL, RWKV_IN + MLA_PAD), lambda t: (0, 0)),
        ],
        out_specs=[
            pl.BlockSpec((TM, RWKV_IN), lambda t: (t, 0)),
            pl.BlockSpec((TM, MLA_PAD), lambda t: (t, 0)),
        ],
        out_shape=[
            jax.ShapeDtypeStruct((N_TOK, RWKV_IN), F32),
            jax.ShapeDtypeStruct((N_TOK, MLA_PAD), F32),
        ],
        compiler_params=pltpu.CompilerParams(
            dimension_semantics=("arbitrary",), vmem_limit_bytes=VMEM_LIMIT),
        name="inproj",
    )(*x_args, mods, g, w)


def _rwkv_kernel(pr_ref, pk_ref, pv_ref, pl_ref, mur_ref, muk_ref, muv_ref, mul_ref,
                 w0_ref, wup_ref, a0_ref, aup_ref, gup_ref, kkg_ref, kag_ref, rkg_ref,
                 lng_ref, lnb_ref, *rest, seq_len, n_seqs, trip, zero_init):
    if zero_init:
        h0_ref = None
        out_ref, hfin_ref, m_s, n_s, yq_s, y0_s, g_s, bonus_s = rest
    else:
        h0_ref, out_ref, hfin_ref, m_s, n_s, yq_s, y0_s, g_s, bonus_s = rest
    C = CHUNK
    n_chunks = seq_len // C
    all_chunks = n_seqs * n_chunks
    static_trip = all_chunks == trip
    lane = lax.broadcasted_iota(jnp.int32, (1, LANES), 1)
    m0 = (lane < HD_A).astype(F32)
    m1 = 1.0 - m0
    ones_g = _group_ones(LANES, HD_A)
    row = lax.broadcasted_iota(jnp.int32, (C, 1), 0)
    i2 = lax.broadcasted_iota(jnp.int32, (2 * C, 2 * C), 0)
    j2 = lax.broadcasted_iota(jnp.int32, (2 * C, 2 * C), 1)
    same_head = (i2 // C) == (j2 // C)
    eye2 = (i2 == j2).astype(F32)
    ic = lax.broadcasted_iota(jnp.int32, (C, C), 0)
    jc = lax.broadcasted_iota(jnp.int32, (C, C), 1)
    incl_c = ((jc <= ic).astype(BF16), (jc >= ic).astype(BF16))
    incl_blk = (same_head & (j2 <= i2), same_head & (j2 >= i2))
    strict_blk = (same_head & (j2 < i2), same_head & (j2 > i2))

    def pair_masks(d):
        masks = []
        s = 1
        while s < C:
            later, earlier = ((i2 // s) % 2, (j2 // s) % 2) if d == 0 else ((j2 // s) % 2, (i2 // s) % 2)
            masks.append(((i2 // (2 * s)) == (j2 // (2 * s))) & (later == 1) & (earlier == 0))
            s *= 2
        return masks

    pair_blk = (pair_masks(0), pair_masks(1))

    def stack(x):
        return jnp.concatenate([x * m0, x * m1], axis=0)

    def chunk_rows(c):
        return pl.ds(c * C, C) if isinstance(c, int) else pl.ds(pl.multiple_of(c * C, C), C)

    def shifted(ref, mu, c):
        x = ref[chunk_rows(c), :]
        if isinstance(c, int):
            first, last = c % n_chunks == 0, c % n_chunks == n_chunks - 1
            prev_row = jnp.zeros((1, x.shape[1]), F32) if first else ref[c * C - 1:c * C, :]
            next_row = jnp.zeros((1, x.shape[1]), F32) if last else ref[(c + 1) * C:(c + 1) * C + 1, :]
        else:
            assert n_seqs == 1
            pstart = pl.multiple_of(jnp.maximum(c * C - 8, 0), 8)
            nstart = pl.multiple_of(jnp.minimum((c + 1) * C, seq_len - 8), 8)
            prev_row = jnp.where(c > 0, ref[pl.ds(pstart, 8), :][7:8], 0.0)
            next_row = jnp.where(c < n_chunks - 1, ref[pl.ds(nstart, 8), :][0:1], 0.0)
        prev = jnp.where(row == 0, prev_row, pltpu.roll(x, 1, 0))
        nxt = jnp.where(row == C - 1, next_row, pltpu.roll(x, C - 1, 0))
        return x + mu * (0.5 * (prev + nxt) - x)

    def chain_mats(chains):
        ds = [ch[0] for ch in chains]
        lp = [_dot_exact_lhs(incl_c[d], logw) for d, *_, logw in chains]
        lpc = [x[C - 1:C] if d == 0 else x[0:1] for d, x in zip(ds, lp)]
        st = []
        for (d, r, kd, v, kk, b, logw), x, xc in zip(chains, lp, lpc):
            p_in, p_ex, p_inv, p_end = jnp.exp(x), jnp.exp(x - logw), jnp.exp(-x), jnp.exp(xc - x)
            st.append(dict(qs=stack(r * p_in), a_s=stack(kk * p_ex), khs=stack(kd * p_inv),
                           bs=stack(b * p_inv), kbs=stack(kd * p_end), bbs=stack(b * p_end), vs=stack(v)))
        lab = [jnp.where(strict_blk[d], _dot1(s["a_s"], s["bs"], _NT), 0.0) for d, s in zip(ds, st)]
        lak = [jnp.where(strict_blk[d], _dot1(s["a_s"], s["khs"], _NT), 0.0) for d, s in zip(ds, st)]
        dqb = [jnp.where(incl_blk[d], _dot1(s["qs"], s["bs"], _NT), 0.0) for d, s in zip(ds, st)]
        dqk = [jnp.where(incl_blk[d], _dot1(s["qs"], s["khs"], _NT), 0.0) for d, s in zip(ds, st)]
        x2 = [_dot1(l, s["vs"]) for l, s in zip(lak, st)]
        tinv = [eye2 - jnp.where(pair_blk[d][0], l, 0.0) for d, l in zip(ds, lab)]
        for lvl in range(1, len(pair_blk[0])):
            half = [_dot1(t, jnp.where(pair_blk[d][lvl], l, 0.0)) for d, t, l in zip(ds, tinv, lab)]
            tinv = [t - _dot1(h, t) for h, t in zip(half, tinv)]
        w12 = [_dot1(t, jnp.concatenate([s["a_s"], x], axis=1)) for t, s, x in zip(tinv, st, x2)]
        mw = [_dot1(s["bbs"].T, w) for s, w in zip(st, w12)]
        dw = [_dot1(q, w) for q, w in zip(dqb, w12)]
        kv = [_dot1(s["kbs"].T, s["vs"]) for s in st]
        qkv = [_dot1(q, s["vs"]) for q, s in zip(dqk, st)]
        return [(eye2 * jnp.exp(xc) - m[:, :LANES], n - m[:, LANES:], s["qs"] - w[:, :LANES], y - w[:, LANES:])
                for xc, m, n, s, w, y in zip(lpc, mw, kv, st, dw, qkv)]

    def load_chunk(c):
        return (shifted(pr_ref, mur_ref[...], c), shifted(pk_ref, muk_ref[...], c),
                shifted(pv_ref, muv_ref[...], c), shifted(pl_ref, mul_ref[...], c))

    def compute_chunks(loaded):
        was = [lo[:, :LANES] for _, _, _, lo in loaded]
        gates = [_dot1(_sigmoid(lo[:, LANES:]), gup_ref[...]) for _, _, _, lo in loaded]
        kkrs = [k * kkg_ref[...] for _, k, _, _ in loaded]
        kks = [kkr / (jnp.sqrt(_gsum(kkr * kkr, ones_g)) + 1e-12) for kkr in kkrs]
        bonuses = [_gsum(r * k * rkg_ref[...], ones_g) * v for r, k, v, _ in loaded]
        twas = [jnp.tanh(wa) for wa in was]
        wls = [[w0_ref[d:d + 1, :] + _dot3(twa, wup_ref[d]) for d in range(2)] for twa in twas]
        als = [[a0_ref[d:d + 1, :] + _dot3(wa, aup_ref[d]) for d in range(2)] for wa in was]
        chains = []
        for (r, k, v, _), kk, wl2, al2 in zip(loaded, kks, wls, als):
            for d in range(2):
                logw = -jnp.exp(-_softplus(-wl2[d]) - 0.5)
                a = _sigmoid(al2[d])
                chains.append((d, r, k * (1.0 + (a - 1.0) * kag_ref[...]), v, kk, kk * a, logw))
        mats = chain_mats(chains)
        return [(g, bo, mats[2 * i:2 * i + 2]) for i, (g, bo) in enumerate(zip(gates, bonuses))]

    def store_chunk(c, gate, bonus, mats):
        g_s[chunk_rows(c), :] = gate
        bonus_s[chunk_rows(c), :] = bonus
        for d in range(2):
            m_s[d, c], n_s[d, c], yq_s[d, c], y0_s[d, c] = mats[d]

    def phase_a_trip(i, carry):
        chunks = [i * trip + u for u in range(trip)]
        results = compute_chunks([load_chunk(c) for c in chunks])
        for c, res in zip(chunks, results):
            store_chunk(c, *res)
        return carry

    if static_trip:
        phase_a_trip(0, 0)
    else:
        lax.fori_loop(0, all_chunks // trip, phase_a_trip, 0)

    def phase_b(i, states):
        new_states = []
        for q in range(n_seqs):
            for d, c in ((0, q * n_chunks + i), (1, q * n_chunks + n_chunks - 1 - i)):
                h = states[2 * q + d]
                new_states.append(_dot3(m_s[d, c], h) + n_s[d, c])
                m_s[d, c] = h
        return tuple(new_states)

    zeros_half = jnp.zeros((HD_A, HD_A), F32)
    states = []
    for q in range(n_seqs):
        for d in range(2):
            if zero_init:
                states.append(jnp.zeros((LANES, LANES), F32))
            else:
                top = jnp.concatenate([h0_ref[q, 0, d, 0], zeros_half], axis=1)
                bot = jnp.concatenate([zeros_half, h0_ref[q, 0, d, 1]], axis=1)
                states.append(jnp.concatenate([top, bot], axis=0).T)
    states = tuple(states)
    if static_trip:
        for i in range(n_chunks):
            states = phase_b(i, states)
    else:
        states = lax.fori_loop(0, n_chunks, phase_b, states)
    for q in range(n_seqs):
        for d in range(2):
            s_vk = states[2 * q + d].T
            hfin_ref[q, d, 0] = s_vk[:HD_A, :HD_A]
            hfin_ref[q, d, 1] = s_vk[HD_A:, HD_A:]

    def phase_c_trip(i, carry):
        chunks = [i * trip + u for u in range(trip)]
        y0s = [y0_s[0, c] + y0_s[1, c] for c in chunks]
        extras = [(bonus_s[chunk_rows(c), :], g_s[chunk_rows(c), :]) for c in chunks]
        yfs = [_dot3(yq_s[0, c], m_s[0, c]) for c in chunks]
        ybs = [_dot3(yq_s[1, c], m_s[1, c]) for c in chunks]
        ys = [(f + b + y0)[:C] + (f + b + y0)[C:] for f, b, y0 in zip(yfs, ybs, y0s)]
        mus = [_gsum(y, ones_g) * (1.0 / HD_A) for y in ys]
        ycs = [y - mu for y, mu in zip(ys, mus)]
        vars_ = [_gsum(yc * yc, ones_g) * (1.0 / HD_A) for yc in ycs]
        for c, yc, var, (bonus, gate) in zip(chunks, ycs, vars_, extras):
            yn = yc * lax.rsqrt(var + GN_EPS) * lng_ref[...] + lnb_ref[...]
            out_ref[chunk_rows(c), :] = ((yn + bonus) * gate).astype(out_ref.dtype)
        return carry

    if static_trip:
        phase_c_trip(0, 0)
    else:
        lax.fori_loop(0, all_chunks // trip, phase_c_trip, 0)


def _rwkv(proj_r, p, h0, layer, *, seq_len, n_seq, row_off):
    zero_init = h0 is None
    seqs = max(1, CHUNKS_PER_TRIP // (seq_len // CHUNK))
    blk_rows = seqs * seq_len
    n_chunks = blk_rows // CHUNK
    off = row_off // blk_rows
    n_pairs = DA // LANES
    col = lambda cb: (lambda s, hp: (off + s, cb + hp))
    vec = lambda: pl.BlockSpec((1, LANES), lambda s, hp: (0, hp))
    in_specs = [
        pl.BlockSpec((blk_rows, LANES), col(0)),
        pl.BlockSpec((blk_rows, LANES), col(n_pairs)),
        pl.BlockSpec((blk_rows, LANES), col(2 * n_pairs)),
        pl.BlockSpec((blk_rows, 2 * LANES), lambda s, hp: (off + s, 3 * n_pairs // 2)),
        vec(), vec(), vec(),
        pl.BlockSpec((1, 2 * LANES), lambda s, hp: (0, 0)),
        pl.BlockSpec((2, LANES), lambda s, hp: (0, hp)),
        pl.BlockSpec((2, LANES, LANES), lambda s, hp: (0, 0, hp)),
        pl.BlockSpec((2, LANES), lambda s, hp: (0, hp)),
        pl.BlockSpec((2, LANES, LANES), lambda s, hp: (0, 0, hp)),
        pl.BlockSpec((G_LORA, LANES), lambda s, hp: (0, hp)),
        vec(), vec(), vec(), vec(), vec(),
    ]
    args = [proj_r, proj_r, proj_r, proj_r, p["mu_r"], p["mu_k"], p["mu_v"], p["mu_l"],
            p["w0"], p["w_up"], p["a0"], p["a_up"], p["g_up"], p["k_k"], p["k_a"], p["r_k"],
            p["ln_g"], p["ln_b"]]
    if not zero_init:
        in_specs.append(pl.BlockSpec((seqs, 1, 2, 2, HD_A, HD_A), lambda s, hp: (s, layer, 0, hp, 0, 0)))
        args.append(h0)
    blk = (2 * CHUNK, LANES)
    return pl.pallas_call(
        functools.partial(_rwkv_kernel, seq_len=seq_len, n_seqs=seqs, trip=CHUNKS_PER_TRIP,
                          zero_init=zero_init),
        grid=(n_seq // seqs, n_pairs),
        in_specs=in_specs,
        out_specs=[
            pl.BlockSpec((blk_rows, LANES), lambda s, hp: (s, hp)),
            pl.BlockSpec((seqs, 2, 2, HD_A, HD_A), lambda s, hp: (s, 0, hp, 0, 0)),
        ],
        out_shape=[
            jax.ShapeDtypeStruct((n_seq * seq_len, DA), BF16),
            jax.ShapeDtypeStruct((n_seq, 2, H_A, HD_A, HD_A), F32),
        ],
        scratch_shapes=[
            pltpu.VMEM((2, n_chunks) + blk, F32),
            pltpu.VMEM((2, n_chunks) + blk, F32),
            pltpu.VMEM((2, n_chunks) + blk, F32),
            pltpu.VMEM((2, n_chunks) + blk, F32),
            pltpu.VMEM((blk_rows, LANES), F32),
            pltpu.VMEM((blk_rows, LANES), F32),
        ],
        compiler_params=pltpu.CompilerParams(
            dimension_semantics=("arbitrary", "arbitrary"), vmem_limit_bytes=VMEM_LIMIT),
        name="rwkv_ctx" if zero_init else "rwkv_lat",
    )(*args)


def _rope(x, cos, sin):
    lane = lax.broadcasted_iota(jnp.int32, x.shape, 1)
    swapped = jnp.where((lane % 32) < 16, pltpu.roll(x, LANES - 16, 1), pltpu.roll(x, 16, 1))
    return x * cos + swapped * sin


def _slab_rms(x, g):
    return x * lax.rsqrt(jnp.sum(x * x, axis=-1, keepdims=True) * (1.0 / ROPE_DIM) + EPS) * g


def _kv_expand(ckv_n, kr_att, ukv_ref, kng_ref, k_ref, v_ref):
    kv = _dot(ckv_n.astype(BF16), ukv_ref[...])
    kr_b = kr_att.astype(BF16)
    for h in range(H_B):
        kn = _rms(kv[:, h * NOPE:(h + 1) * NOPE], kng_ref[...])
        k_ref[:, 2 * h * LANES:(2 * h + 1) * LANES] = kn.astype(BF16)
        k_ref[:, (2 * h + 1) * LANES:(2 * h + 2) * LANES] = kr_b
    v_ref[...] = kv[:, H_B * NOPE:].astype(BF16)


def _mla_proj_kernel(pm_ref, cqg_ref, uq_ref, qng_ref, qrg_ref, ckvg_ref, ukv_ref, kng_ref, krg_ref,
                     *rest, rope):
    if rope:
        cos_ref, sin_ref, q_ref, k_ref, v_ref, ckv_ref, kr_ref = rest
    else:
        q_ref, k_ref, v_ref, ckv_ref, kr_ref = rest
    pm = pm_ref[...]
    cq = _rms(pm[:, :Q_LORA], cqg_ref[...])
    q = _dot(cq.astype(BF16), uq_ref[...])
    for h in range(H_B):
        qn = _rms(q[:, 2 * h * LANES:(2 * h + 1) * LANES], qng_ref[...])
        qr = _slab_rms(q[:, (2 * h + 1) * LANES:(2 * h + 2) * LANES], qrg_ref[...])
        if rope:
            qr = _rope(qr, cos_ref[...], sin_ref[...])
        q_ref[:, 2 * h * LANES:(2 * h + 1) * LANES] = (qn * ATTN_SCALE).astype(BF16)
        q_ref[:, (2 * h + 1) * LANES:(2 * h + 2) * LANES] = (qr * ATTN_SCALE).astype(BF16)
    ckv_n = _rms(pm[:, Q_LORA:Q_LORA + KV_LORA], ckvg_ref[...])
    ckv_ref[...] = ckv_n
    kr = _slab_rms(pm[:, Q_LORA + KV_LORA:], krg_ref[...])
    kr_ref[...] = kr[:, :ROPE_DIM]
    kr_att = _rope(kr, cos_ref[...], sin_ref[...]) if rope else kr
    _kv_expand(ckv_n, kr_att, ukv_ref, kng_ref, k_ref, v_ref)


def _mla_proj(proj_m, p, rope_tabs, *, n_rows, row_off, rope):
    t0 = row_off // TM
    full = lambda shape: pl.BlockSpec(shape, lambda t: tuple(0 for _ in shape))
    in_specs = [
        pl.BlockSpec((TM, MLA_PAD), lambda t: (t0 + t, 0)),
        full((1, Q_LORA)), full((Q_LORA, 2 * H_B * LANES)), full((1, LANES)), full((1, LANES)),
        full((1, KV_LORA)), full((KV_LORA, 2 * H_B * NOPE)), full((1, LANES)), full((1, LANES)),
    ]
    args = [proj_m, p["cq_g"], p["uq_w"], p["qn_g"], p["qr_g"], p["ckv_g"], p["ukv_w"], p["kn_g"], p["kr_g"]]
    if rope:
        per_seq = DEC_SEQ // TM
        in_specs += [pl.BlockSpec((TM, LANES), lambda t: (t % per_seq, 0))] * 2
        args += list(rope_tabs)
    rows = lambda w: pl.BlockSpec((TM, w), lambda t: (t, 0))
    return pl.pallas_call(
        functools.partial(_mla_proj_kernel, rope=rope),
        grid=(n_rows // TM,),
        in_specs=in_specs,
        out_specs=[rows(2 * H_B * LANES), rows(2 * H_B * LANES), rows(H_B * V_DIM), rows(KV_LORA), rows(ROPE_DIM)],
        out_shape=[
            jax.ShapeDtypeStruct((n_rows, 2 * H_B * LANES), BF16),
            jax.ShapeDtypeStruct((n_rows, 2 * H_B * LANES), BF16),
            jax.ShapeDtypeStruct((n_rows, H_B * V_DIM), BF16),
            jax.ShapeDtypeStruct((n_rows, KV_LORA), F32),
            jax.ShapeDtypeStruct((n_rows, ROPE_DIM), F32),
        ],
        compiler_params=pltpu.CompilerParams(
            dimension_semantics=("arbitrary",), vmem_limit_bytes=VMEM_LIMIT),
        name="mla_proj_lat" if rope else "mla_proj_ctx",
    )(*args)


def _mla_cache_kernel(ckv_ref, kr_ref, ukv_ref, kng_ref, k_ref, v_ref):
    _kv_expand(ckv_ref[...], kr_ref[...], ukv_ref, kng_ref, k_ref, v_ref)


def _mla_cache_expand(ckv, kr_slab, p):
    n = ckv.shape[0]
    full = lambda shape: pl.BlockSpec(shape, lambda t: tuple(0 for _ in shape))
    return pl.pallas_call(
        _mla_cache_kernel,
        grid=(1,),
        in_specs=[full((n, KV_LORA)), full((n, LANES)), full((KV_LORA, 2 * H_B * NOPE)), full((1, LANES))],
        out_specs=[full((n, 2 * H_B * LANES)), full((n, H_B * V_DIM))],
        out_shape=[
            jax.ShapeDtypeStruct((n, 2 * H_B * LANES), BF16),
            jax.ShapeDtypeStruct((n, H_B * V_DIM), BF16),
        ],
        compiler_params=pltpu.CompilerParams(
            dimension_semantics=("arbitrary",), vmem_limit_bytes=VMEM_LIMIT),
        name="mla_cache_expand",
    )(ckv, kr_slab, p["ukv_w"], p["kn_g"])


def _attn_kernel(q_ref, k_ref, v_ref, *rest, with_ctx):
    if with_ctx:
        k2_ref, v2_ref, o_ref = rest
    else:
        (o_ref,) = rest
    for h in range(H_B):
        hs = slice(2 * h * LANES, (2 * h + 2) * LANES)
        vs = slice(h * V_DIM, (h + 1) * V_DIM)
        q = q_ref[:, hs]
        s = _dot(q, k_ref[:, hs], _NT)
        m = jnp.max(s, axis=-1, keepdims=True)
        if with_ctx:
            s2 = _dot(q, k2_ref[:, hs], _NT)
            m = jnp.maximum(m, jnp.max(s2, axis=-1, keepdims=True))
        pr = jnp.exp(s - m)
        den = jnp.sum(pr, axis=-1, keepdims=True)
        o = _dot(pr.astype(BF16), v_ref[:, vs])
        if with_ctx:
            pr2 = jnp.exp(s2 - m)
            den = den + jnp.sum(pr2, axis=-1, keepdims=True)
            o = o + _dot(pr2.astype(BF16), v2_ref[:, vs])
        o_ref[:, vs] = (o / den).astype(o_ref.dtype)


def _attention(q, k, v, k2=None, v2=None, *, n_batch, seq_len):
    with_ctx = k2 is not None
    q_tiles = seq_len // TM
    kw = 2 * H_B * LANES
    in_specs = [
        pl.BlockSpec((TM, kw), lambda b, t: (b * q_tiles + t, 0)),
        pl.BlockSpec((seq_len, kw), lambda b, t: (b, 0)),
        pl.BlockSpec((seq_len, H_B * V_DIM), lambda b, t: (b, 0)),
    ]
    args = [q, k, v]
    if with_ctx:
        in_specs += [
            pl.BlockSpec((PAST_LEN, kw), lambda b, t: (b, 0)),
            pl.BlockSpec((PAST_LEN, H_B * V_DIM), lambda b, t: (b, 0)),
        ]
        args += [k2, v2]
    return pl.pallas_call(
        functools.partial(_attn_kernel, with_ctx=with_ctx),
        grid=(n_batch, q_tiles),
        in_specs=in_specs,
        out_specs=pl.BlockSpec((TM, H_B * V_DIM), lambda b, t: (b * q_tiles + t, 0)),
        out_shape=jax.ShapeDtypeStruct((n_batch * seq_len, H_B * V_DIM), BF16),
        compiler_params=pltpu.CompilerParams(
            dimension_semantics=("arbitrary", "arbitrary"), vmem_limit_bytes=VMEM_LIMIT),
        name="attn_lat" if with_ctx else "attn_ctx",
    )(*args)


def _outproj_kernel(*refs, n_x):
    x_refs, (a0_ref, a1_ref, b0_ref, b1_ref, w_ref, mod_ref, o_ref) = refs[:n_x], refs[n_x:]
    m = mod_ref[0, 0]
    mix = (_dot(_row_tile((a0_ref, a1_ref)), w_ref[:DA, :])
           + _dot(_row_tile((b0_ref, b1_ref)), w_ref[DA:, :]))
    o_ref[...] = _row_tile(x_refs) + m[2:3] * mix


def _outproj(x, a_out, b_out, w, mods, l):
    x_specs, x_args = _row_specs(x, D_MODEL)
    a_specs, a_args = _row_specs(a_out, DA)
    b_specs, b_args = _row_specs(b_out, DA)
    return pl.pallas_call(
        functools.partial(_outproj_kernel, n_x=len(x_args)),
        grid=(N_TILES,),
        in_specs=x_specs + a_specs + b_specs + [
            pl.BlockSpec((D_MODEL, D_MODEL), lambda t: (0, 0)),
            _mod_spec(l),
        ],
        out_specs=pl.BlockSpec((TM, D_MODEL), lambda t: (t, 0)),
        out_shape=jax.ShapeDtypeStruct((N_TOK, D_MODEL), F32),
        compiler_params=pltpu.CompilerParams(
            dimension_semantics=("arbitrary",), vmem_limit_bytes=VMEM_LIMIT),
        name="outproj",
    )(*x_args, *a_args, *b_args, w, mods)


def _pool_kernel(x_ref, xp_ref, xn_ref, mod_ref, g_ref, pw_ref, ps_ref, o_ref):
    t = pl.program_id(0)
    is_lat = t >= CTX_TILES
    base = jnp.where(is_lat, ((t - CTX_TILES) % LAT_TILES_PER_SEQ) * TM, 0)
    seq_len = jnp.where(is_lat, DEC_SEQ, SEQ)
    m = mod_ref[0, 0]
    norm_mod = lambda x: _rms(x, g_ref[...]) * (1.0 + m[1:2]) + m[0:1]
    x = x_ref[...]
    h = norm_mod(x)
    hp = jnp.where(base > 0, norm_mod(xp_ref[...]), 0.0)
    hn = jnp.where(base + TM < seq_len, norm_mod(xn_ref[...]), 0.0)
    ext = jnp.concatenate([h, hp, hn, jnp.zeros((POOL_K - TM - 2 * HALO, D_MODEL), F32)], axis=0)
    i = lax.broadcasted_iota(jnp.int32, (TM, POOL_K), 0)
    e = lax.broadcasted_iota(jnp.int32, (TM, POOL_K), 1)
    e = jnp.where(e < TM, e, jnp.where(e < TM + HALO, e - TM - HALO, e - HALO))
    pos = base + lax.broadcasted_iota(jnp.int32, (TM, 1), 0)
    for gi, win in enumerate(POOL_WINDOWS):
        half = win // 2
        cols = slice(gi * GP, (gi + 1) * GP)
        band = ((e >= i - half) & (e < i + half)).astype(BF16)
        cnt = (jnp.minimum(pos + half, seq_len) - jnp.maximum(pos - half, 0)).astype(F32)
        d = _dot_exact_lhs(band, ext[:, cols]) / cnt - h[:, cols]
        mp = _dot(d.astype(BF16), pw_ref[gi]) * ps_ref[:, cols]
        o_ref[:, cols] = x[:, cols] + m[2:3, cols] * mp


def _pool(x, mods, l, g, pw, ps):
    r = TM // HALO
    n_halo_blocks = N_TOK // HALO
    return pl.pallas_call(
        _pool_kernel,
        grid=(N_TILES,),
        in_specs=[
            pl.BlockSpec((TM, D_MODEL), lambda t: (t, 0)),
            pl.BlockSpec((HALO, D_MODEL), lambda t: (jnp.maximum(t * r - 1, 0), 0)),
            pl.BlockSpec((HALO, D_MODEL), lambda t: (jnp.minimum((t + 1) * r, n_halo_blocks - 1), 0)),
            _mod_spec(l),
            pl.BlockSpec((1, D_MODEL), lambda t: (0, 0)),
            pl.BlockSpec((len(POOL_WINDOWS), GP, GP), lambda t: (0, 0, 0)),
            pl.BlockSpec((1, D_MODEL), lambda t: (0, 0)),
        ],
        out_specs=pl.BlockSpec((TM, D_MODEL), lambda t: (t, 0)),
        out_shape=jax.ShapeDtypeStruct((N_TOK, D_MODEL), F32),
        compiler_params=pltpu.CompilerParams(
            dimension_semantics=("arbitrary",), vmem_limit_bytes=VMEM_LIMIT),
        name="pool",
    )(x, x, x, mods, g, pw, ps)


def _ffn_kernel(x_ref, mod_ref, g_ref, w1_ref, w3_ref, w2_ref, o_ref):
    m = mod_ref[0, 0]
    x = x_ref[...]
    h = (_rms(x, g_ref[...]) * (1.0 + m[4:5]) + m[3:4]).astype(BF16)
    acc = None
    for lo, hi in FF_SPLITS:
        u = _silu(_dot(h, w1_ref[:, lo:hi])) * _dot(h, w3_ref[:, lo:hi])
        part = _dot(u.astype(BF16), w2_ref[lo:hi, :])
        acc = part if acc is None else acc + part
    o_ref[...] = x + m[5:6] * acc


def _ffn(x, mods, l, g, w1, w3, w2, row_off=0, n_rows=N_TOK):
    per_tile = TM_FFN // TM
    t0 = row_off // TM_FFN
    resident = lambda shape: pl.BlockSpec(shape, lambda i: (0, 0), pipeline_mode=pl.Buffered(1))
    return pl.pallas_call(
        _ffn_kernel,
        grid=(n_rows // TM_FFN,),
        in_specs=[
            pl.BlockSpec((TM_FFN, D_MODEL), lambda i: (t0 + i, 0)),
            pl.BlockSpec((1, 1, N_MOD, D_MODEL), lambda i: (l, _cond_of_tile((t0 + i) * per_tile), 0, 0)),
            pl.BlockSpec((1, D_MODEL), lambda i: (0, 0)),
            resident((D_MODEL, D_FF)), resident((D_MODEL, D_FF)), resident((D_FF, D_MODEL)),
        ],
        out_specs=pl.BlockSpec((TM_FFN, D_MODEL), lambda i: (i, 0)),
        out_shape=jax.ShapeDtypeStruct((n_rows, D_MODEL), F32),
        compiler_params=pltpu.CompilerParams(
            dimension_semantics=("arbitrary",), vmem_limit_bytes=VMEM_LIMIT),
        name="ffn",
    )(x, mods, g, w1, w3, w2)


def _rope_tables():
    pos = jnp.arange(DEC_SEQ)
    freqs = ROPE_BASE ** (-jnp.arange(ROPE_AXIS_FREQS, dtype=F32) / ROPE_AXIS_FREQS)
    ang_row = (pos // GRID_W).astype(F32)[:, None] * freqs
    ang_col = (pos % GRID_W).astype(F32)[:, None] * freqs
    zeros = jnp.zeros((DEC_SEQ, LANES - ROPE_DIM), F32)
    cos = jnp.concatenate([jnp.cos(ang_row)] * 2 + [jnp.cos(ang_col)] * 2 + [zeros], axis=1)
    sin = jnp.concatenate([-jnp.sin(ang_row), jnp.sin(ang_row), -jnp.sin(ang_col), jnp.sin(ang_col), zeros], axis=1)
    return cos, sin


def _slab(v):
    return jnp.pad(v, (0, LANES - v.shape[0]))[None]


def _even_params(i, in_w, out_w, shift_mu, w0, w_up, a0, a_up, g_up, k_k, k_a, r_k, ln_g, ln_b,
                 cq_g, uq_w, ckv_g, ukv_w, qn_g, qr_g, kn_g, kr_g):
    row = lambda v: v[None]
    in_w_pad = jnp.pad(in_w[i], ((0, 0), (0, MLA_PAD - MLA_IN))).astype(BF16)
    mu = shift_mu[i]
    w_up_pad = jnp.pad(w_up[i], ((0, 0), (0, A_LORA), (0, 0)))
    a_up_pad = jnp.pad(a_up[i], ((0, 0), (W_LORA, 0), (0, 0)))
    uq = uq_w[i].reshape(Q_LORA, H_B, NOPE + ROPE_DIM)
    uq = jnp.pad(uq, ((0, 0), (0, 0), (0, 2 * LANES - NOPE - ROPE_DIM))).reshape(Q_LORA, 2 * H_B * LANES)
    ukv = ukv_w[i].reshape(KV_LORA, H_B, NOPE + V_DIM)
    ukv = jnp.concatenate([ukv[:, :, :NOPE].reshape(KV_LORA, -1), ukv[:, :, NOPE:].reshape(KV_LORA, -1)], axis=1)
    rw = dict(mu_r=row(mu[:DA]), mu_k=row(mu[DA:2 * DA]), mu_v=row(mu[2 * DA:3 * DA]), mu_l=row(mu[3 * DA:]),
              w0=w0[i], w_up=w_up_pad, a0=a0[i], a_up=a_up_pad, g_up=g_up[i],
              k_k=row(k_k[i]), k_a=row(k_a[i]), r_k=row(r_k[i].reshape(DA)), ln_g=row(ln_g[i]), ln_b=row(ln_b[i]))
    mla = dict(cq_g=row(cq_g[i]), uq_w=uq.astype(BF16), qn_g=row(qn_g[i]), qr_g=_slab(qr_g[i]),
               ckv_g=row(ckv_g[i]), ukv_w=ukv.astype(BF16), kn_g=row(kn_g[i]), kr_g=_slab(kr_g[i]))
    return in_w_pad, out_w[i].astype(BF16), rw, mla


def kernel(x_prompt, x_sample, c, state_rwkv, cache_mla_ckv, cache_mla_krope, c_ctx,
           norm1_g, norm2_g, ada_w, ada_b, ffn_w1, ffn_w3, ffn_w2,
           in_w, out_w, shift_mu, rwkv_w0, rwkv_w_up, rwkv_a0, rwkv_a_up, rwkv_g_up,
           rwkv_k_k, rwkv_k_a, rwkv_r_k, rwkv_ln_g, rwkv_ln_b,
           mla_cq_g, mla_uq_w, mla_ckv_g, mla_ukv_w, mla_qn_g, mla_qr_g, mla_kn_g, mla_kr_g,
           pool_w, pool_scale):
    x = (x_prompt.reshape(N_CTX, D_MODEL), x_sample.reshape(N_LAT, D_MODEL))
    conds = jnp.concatenate([c_ctx[None], c, jnp.zeros((8 - 1 - DEC_BATCH, D_MODEL), F32)], axis=0)
    mods = _adaln(conds, ada_w, ada_b).reshape(DEPTH, 8, N_MOD, D_MODEL)
    rope_tabs = _rope_tables()

    new_s, new_ckv, new_kr = [], [], []
    for l in range(DEPTH):
        n1 = norm1_g[l][None]
        if l % 2 == 0:
            i = l // 2
            in_w_pad, out_w_b, rw, mla = _even_params(
                i, in_w, out_w, shift_mu, rwkv_w0, rwkv_w_up, rwkv_a0, rwkv_a_up, rwkv_g_up,
                rwkv_k_k, rwkv_k_a, rwkv_r_k, rwkv_ln_g, rwkv_ln_b,
                mla_cq_g, mla_uq_w, mla_ckv_g, mla_ukv_w, mla_qn_g, mla_qr_g, mla_kn_g, mla_kr_g)
            proj_r, proj_m = _inproj(x, mods, l, n1, in_w_pad)
            a_ctx, s_ctx = _rwkv(proj_r, rw, None, i, seq_len=SEQ, n_seq=BATCH, row_off=0)
            a_lat, _ = _rwkv(proj_r, rw, state_rwkv, i, seq_len=DEC_SEQ, n_seq=DEC_BATCH, row_off=N_CTX)
            q_c, k_c, v_c, ckv_c, kr_c = _mla_proj(proj_m, mla, None, n_rows=N_CTX, row_off=0, rope=False)
            q_l, k_l, v_l, _, _ = _mla_proj(proj_m, mla, rope_tabs, n_rows=N_LAT, row_off=N_CTX, rope=True)
            kr_cache = jnp.pad(cache_mla_krope[:, i].reshape(DEC_BATCH * PAST_LEN, ROPE_DIM),
                               ((0, 0), (0, LANES - ROPE_DIM)))
            k_p, v_p = _mla_cache_expand(cache_mla_ckv[:, i].reshape(DEC_BATCH * PAST_LEN, KV_LORA), kr_cache, mla)
            b_ctx = _attention(q_c, k_c, v_c, n_batch=BATCH, seq_len=SEQ)
            b_lat = _attention(q_l, k_l, v_l, k_p, v_p, n_batch=DEC_BATCH, seq_len=DEC_SEQ)
            x = _outproj(x, (a_ctx, a_lat), (b_ctx, b_lat), out_w_b, mods, l)
            new_s.append(s_ctx)
            new_ckv.append(ckv_c.reshape(BATCH, SEQ, KV_LORA))
            new_kr.append(kr_c.reshape(BATCH, SEQ, ROPE_DIM))
        else:
            j = l // 2
            x = _pool(x, mods, l, n1, pool_w[j].astype(BF16), pool_scale[j][None])
        ffn_args = (mods, l, norm2_g[l][None], ffn_w1[l].astype(BF16), ffn_w3[l].astype(BF16),
                    ffn_w2[l].astype(BF16))
        if l < DEPTH - 1:
            x = _ffn(x, *ffn_args)
        else:
            y_p = _ffn(x, *ffn_args, row_off=0, n_rows=N_CTX).reshape(BATCH, SEQ, D_MODEL)
            y_s = _ffn(x, *ffn_args, row_off=N_CTX, n_rows=N_LAT).reshape(DEC_BATCH, DEC_SEQ, D_MODEL)
    return (y_p, y_s, jnp.stack(new_s, axis=1), jnp.stack(new_ckv, axis=1), jnp.stack(new_kr, axis=1))
```

```python
import functools

import jax
import jax.numpy as jnp
from jax import lax
from jax.experimental import pallas as pl
from jax.experimental.pallas import tpu as pltpu

D_MODEL = 1024
BATCH, SEQ = 32, 256
DEC_BATCH, DEC_SEQ = 2, 2048
DEPTH = 4
PAST_LEN = 256
GRID_W = 64
N_MOD = 6
D_FF = 2816
DA = 512
HD_A = 64
H_A = 8
W_LORA, A_LORA, G_LORA = 64, 64, 128
H_B = 4
NOPE, ROPE_DIM, V_DIM = 128, 64, 128
Q_LORA, KV_LORA = 384, 256
ATTN_SCALE = (NOPE + ROPE_DIM) ** -0.5
ROPE_AXIS_FREQS = ROPE_DIM // 4
ROPE_BASE = 10000.0
RWKV_IN = 3 * DA + W_LORA + A_LORA + G_LORA
MLA_IN = Q_LORA + KV_LORA + ROPE_DIM
MLA_PAD = 768
POOL_WINDOWS = (2, 4, 8, 16)
GP = 256
EPS = 1e-6
GN_EPS = 64e-5

N_CTX = BATCH * SEQ
N_LAT = DEC_BATCH * DEC_SEQ
N_TOK = N_CTX + N_LAT

LANES = 128
TM = 256
N_TILES = N_TOK // TM
CTX_TILES = N_CTX // TM
LAT_TILES_PER_SEQ = DEC_SEQ // TM
CHUNK = 64
CHUNKS_PER_TRIP = 8
HALO = 8
POOL_K = 384
TM_FFN = 512
MXU_WIDTH = 256
FF_SPLITS = ((0, 6 * MXU_WIDTH), (6 * MXU_WIDTH, D_FF))
VMEM_LIMIT = 56 * 1024 * 1024

F32 = jnp.float32
BF16 = jnp.bfloat16


def _cond_of_tile(t):
    return jnp.where(t < CTX_TILES, 0, 1 + (t - CTX_TILES) // LAT_TILES_PER_SEQ)


def _dot(a, b, dims=((1,), (0,))):
    return lax.dot_general(a, b, (dims, ((), ())), preferred_element_type=F32)


_NT = ((1,), (1,))


def _split2(x):
    hi = x.astype(BF16)
    lo = (x - hi.astype(F32)).astype(BF16)
    return hi, lo


def _dot1(a, b, dims=((1,), (0,))):
    return _dot(a.astype(BF16), b.astype(BF16), dims)


def _dot3(a, b, dims=((1,), (0,))):
    ah, al = _split2(a)
    bh, bl = _split2(b)
    return _dot(ah, bh, dims) + (_dot(ah, bl, dims) + _dot(al, bh, dims))


def _dot_exact_lhs(a_bf16, b):
    b0, b1 = _split2(b)
    return _dot(a_bf16, b0) + _dot(a_bf16, b1)


def _rms(x, g):
    return x * lax.rsqrt(jnp.mean(x * x, axis=-1, keepdims=True) + EPS) * g


def _sigmoid(x):
    return 1.0 / (1.0 + jnp.exp(-x))


def _silu(x):
    return x * _sigmoid(x)


def _softplus(x):
    return jnp.maximum(x, 0.0) + jnp.log(1.0 + jnp.exp(-jnp.abs(x)))


def _group_ones(n, g):
    i = lax.broadcasted_iota(jnp.int32, (n, n), 0) // g
    j = lax.broadcasted_iota(jnp.int32, (n, n), 1) // g
    return (i == j).astype(BF16)


def _gsum(x, ones):
    hi, lo = _split2(x)
    return _dot(hi, ones) + _dot(lo, ones)


def _adaln_kernel(c_ref, w_ref, b_ref, o_ref):
    o_ref[0] = _dot3(_silu(c_ref[...]), w_ref[0]) + b_ref[0]


def _adaln(conds, ada_w, ada_b):
    tn = 1536
    return pl.pallas_call(
        _adaln_kernel,
        grid=(DEPTH, N_MOD * D_MODEL // tn),
        in_specs=[
            pl.BlockSpec((8, D_MODEL), lambda l, j: (0, 0)),
            pl.BlockSpec((1, D_MODEL, tn), lambda l, j: (l, 0, j)),
            pl.BlockSpec((1, 1, tn), lambda l, j: (l, 0, j)),
        ],
        out_specs=pl.BlockSpec((1, 8, tn), lambda l, j: (l, 0, j)),
        out_shape=jax.ShapeDtypeStruct((DEPTH, 8, N_MOD * D_MODEL), F32),
        compiler_params=pltpu.CompilerParams(
            dimension_semantics=("arbitrary", "arbitrary"), vmem_limit_bytes=VMEM_LIMIT),
        name="adaln",
    )(conds, ada_w, ada_b.reshape(DEPTH, 1, N_MOD * D_MODEL))


def _mod_spec(l):
    return pl.BlockSpec((1, 1, N_MOD, D_MODEL), lambda t, *_: (l, _cond_of_tile(t), 0, 0))


def _row_specs(x, width, tm=TM):
    if not isinstance(x, tuple):
        return [pl.BlockSpec((tm, width), lambda t, *_: (t, 0))], [x]
    ctx_tiles = N_CTX // tm
    return [pl.BlockSpec((tm, width), lambda t, *_: (jnp.minimum(t, ctx_tiles - 1), 0)),
            pl.BlockSpec((tm, width), lambda t, *_: (jnp.maximum(t - ctx_tiles, 0), 0))], list(x)


def _row_tile(refs, tm=TM):
    if len(refs) == 1:
        return refs[0][...]
    return jnp.where(pl.program_id(0) < N_CTX // tm, refs[0][...], refs[1][...])


def _inproj_kernel(*refs):
    *x_refs, mod_ref, g_ref, w_ref, or_ref, om_ref = refs
    m = mod_ref[0, 0]
    h = _rms(_row_tile(x_refs), g_ref[...]) * (1.0 + m[1:2]) + m[0:1]
    p = _dot(h.astype(BF16), w_ref[...])
    or_ref[...] = p[:, :RWKV_IN]
    om_ref[...] = p[:, RWKV_IN:]


def _inproj(x, mods, l, g, w):
    x_specs, x_args = _row_specs(x, D_MODEL)
    return pl.pallas_call(
        _inproj_kernel,
        grid=(N_TILES,),
        in_specs=x_specs + [
            _mod_spec(l),
            pl.BlockSpec((1, D_MODEL), lambda t: (0, 0)),
            pl.BlockSpec((D_MODEL, RWKV_IN + MLA_PAD), lambda t: (0, 0)),
        ],
        out_specs=[
            pl.BlockSpec((TM, RWKV_IN), lambda t: (t, 0)),
            pl.BlockSpec((TM, MLA_PAD), lambda t: (t, 0)),
        ],
        out_shape=[
            jax.ShapeDtypeStruct((N_TOK, RWKV_IN), F32),
            jax.ShapeDtypeStruct((N_TOK, MLA_PAD), F32),
        ],
        compiler_params=pltpu.CompilerParams(
            dimension_semantics=("arbitrary",), vmem_limit_bytes=VMEM_LIMIT),
        name="inproj",
    )(*x_args, mods, g, w)


def _rwkv_kernel(pr_ref, pk_ref, pv_ref, pl_ref, mur_ref, muk_ref, muv_ref, mul_ref,
                 w0_ref, wup_ref, a0_ref, aup_ref, gup_ref, kkg_ref, kag_ref, rkg_ref,
                 lng_ref, lnb_ref, *rest, seq_len, n_seqs, trip, zero_init):
    if zero_init:
        h0_ref = None
        out_ref, hfin_ref, m_s, n_s, yq_s, y0_s, g_s, bonus_s = rest
    else:
        h0_ref, out_ref, hfin_ref, m_s, n_s, yq_s, y0_s, g_s, bonus_s = rest
    C = CHUNK
    n_chunks = seq_len // C
    all_chunks = n_seqs * n_chunks
    static_trip = all_chunks == trip
    lane = lax.broadcasted_iota(jnp.int32, (1, LANES), 1)
    m0 = (lane < HD_A).astype(F32)
    m1 = 1.0 - m0
    ones_g = _group_ones(LANES, HD_A)
    row = lax.broadcasted_iota(jnp.int32, (C, 1), 0)
    i2 = lax.broadcasted_iota(jnp.int32, (2 * C, 2 * C), 0)
    j2 = lax.broadcasted_iota(jnp.int32, (2 * C, 2 * C), 1)
    same_head = (i2 // C) == (j2 // C)
    eye2 = (i2 == j2).astype(F32)
    ic = lax.broadcasted_iota(jnp.int32, (C, C), 0)
    jc = lax.broadcasted_iota(jnp.int32, (C, C), 1)
    incl_c = ((jc <= ic).astype(BF16), (jc >= ic).astype(BF16))
    incl_blk = (same_head & (j2 <= i2), same_head & (j2 >= i2))
    strict_blk = (same_head & (j2 < i2), same_head & (j2 > i2))

    def pair_masks(d):
        masks = []
        s = 1
        while s < C:
            later, earlier = ((i2 // s) % 2, (j2 // s) % 2) if d == 0 else ((j2 // s) % 2, (i2 // s) % 2)
            masks.append(((i2 // (2 * s)) == (j2 // (2 * s))) & (later == 1) & (earlier == 0))
            s *= 2
        return masks

    pair_blk = (pair_masks(0), pair_masks(1))

    def stack(x):
        return jnp.concatenate([x * m0, x * m1], axis=0)

    def chunk_rows(c):
        return pl.ds(c * C, C) if isinstance(c, int) else pl.ds(pl.multiple_of(c * C, C), C)

    def shifted(ref, mu, c):
        x = ref[chunk_rows(c), :]
        if isinstance(c, int):
            first, last = c % n_chunks == 0, c % n_chunks == n_chunks - 1
            prev_row = jnp.zeros((1, x.shape[1]), F32) if first else ref[c * C - 1:c * C, :]
            next_row = jnp.zeros((1, x.shape[1]), F32) if last else ref[(c + 1) * C:(c + 1) * C + 1, :]
        else:
            assert n_seqs == 1
            pstart = pl.multiple_of(jnp.maximum(c * C - 8, 0), 8)
            nstart = pl.multiple_of(jnp.minimum((c + 1) * C, seq_len - 8), 8)
            prev_row = jnp.where(c > 0, ref[pl.ds(pstart, 8), :][7:8], 0.0)
            next_row = jnp.where(c < n_chunks - 1, ref[pl.ds(nstart, 8), :][0:1], 0.0)
        prev = jnp.where(row == 0, prev_row, pltpu.roll(x, 1, 0))
        nxt = jnp.where(row == C - 1, next_row, pltpu.roll(x, C - 1, 0))
        return x + mu * (0.5 * (prev + nxt) - x)

    def chain_mats(chains):
        ds = [ch[0] for ch in chains]
        lp = [_dot_exact_lhs(incl_c[d], logw) for d, *_, logw in chains]
        lpc = [x[C - 1:C] if d == 0 else x[0:1] for d, x in zip(ds, lp)]
        st = []
        for (d, r, kd, v, kk, b, logw), x, xc in zip(chains, lp, lpc):
            p_in, p_ex, p_inv, p_end = jnp.exp(x), jnp.exp(x - logw), jnp.exp(-x), jnp.exp(xc - x)
            st.append(dict(qs=stack(r * p_in), a_s=stack(kk * p_ex), khs=stack(kd * p_inv),
                           bs=stack(b * p_inv), kbs=stack(kd * p_end), bbs=stack(b * p_end), vs=stack(v)))
        R = 2 * C
        quad = [_dot1(jnp.concatenate([s["a_s"], s["qs"]], axis=0),
                      jnp.concatenate([s["bs"], s["khs"]], axis=0), _NT) for s in st]
        lab = [jnp.where(strict_blk[d], q[:R, :R], 0.0) for d, q in zip(ds, quad)]
        lak = [jnp.where(strict_blk[d], q[:R, R:], 0.0) for d, q in zip(ds, quad)]
        dqb = [jnp.where(incl_blk[d], q[R:, :R], 0.0) for d, q in zip(ds, quad)]
        dqk = [jnp.where(incl_blk[d], q[R:, R:], 0.0) for d, q in zip(ds, quad)]
        x2 = [_dot1(l, s["vs"]) for l, s in zip(lak, st)]
        tinv = [eye2 - jnp.where(pair_blk[d][0], l, 0.0) for d, l in zip(ds, lab)]
        for lvl in range(1, len(pair_blk[0])):
            half = [_dot1(t, jnp.where(pair_blk[d][lvl], l, 0.0)) for d, t, l in zip(ds, tinv, lab)]
            tinv = [t - _dot1(h, t) for h, t in zip(half, tinv)]
        w12 = [_dot1(t, jnp.concatenate([s["a_s"], x], axis=1)) for t, s, x in zip(tinv, st, x2)]
        zero_blk = jnp.zeros((R, R), F32)
        fin = [_dot1(jnp.concatenate([jnp.concatenate([s["bbs"].T, -s["kbs"].T], axis=1),
                                      jnp.concatenate([qb, -qk], axis=1)], axis=0),
                     jnp.concatenate([w, jnp.concatenate([zero_blk, s["vs"]], axis=1)], axis=0))
               for s, qb, qk, w in zip(st, dqb, dqk, w12)]
        return [(eye2 * jnp.exp(xc) - f[:R, :R], -f[:R, R:], s["qs"] - f[R:, :R], -f[R:, R:])
                for xc, f, s in zip(lpc, fin, st)]

    def load_chunk(c):
        return (shifted(pr_ref, mur_ref[...], c), shifted(pk_ref, muk_ref[...], c),
                shifted(pv_ref, muv_ref[...], c), shifted(pl_ref, mul_ref[...], c))

    def compute_chunks(loaded):
        was = [lo[:, :LANES] for _, _, _, lo in loaded]
        gates = [_dot1(_sigmoid(lo[:, LANES:]), gup_ref[...]) for _, _, _, lo in loaded]
        kkrs = [k * kkg_ref[...] for _, k, _, _ in loaded]
        kks = [kkr / (jnp.sqrt(_gsum(kkr * kkr, ones_g)) + 1e-12) for kkr in kkrs]
        bonuses = [_gsum(r * k * rkg_ref[...], ones_g) * v for r, k, v, _ in loaded]
        twas = [jnp.tanh(wa) for wa in was]
        wls = [[w0_ref[d:d + 1, :] + _dot3(twa, wup_ref[d]) for d in range(2)] for twa in twas]
        als = [[a0_ref[d:d + 1, :] + _dot3(wa, aup_ref[d]) for d in range(2)] for wa in was]
        chains = []
        for (r, k, v, _), kk, wl2, al2 in zip(loaded, kks, wls, als):
            for d in range(2):
                logw = -jnp.exp(-_softplus(-wl2[d]) - 0.5)
                a = _sigmoid(al2[d])
                chains.append((d, r, k * (1.0 + (a - 1.0) * kag_ref[...]), v, kk, kk * a, logw))
        mats = chain_mats(chains)
        return [(g, bo, mats[2 * i:2 * i + 2]) for i, (g, bo) in enumerate(zip(gates, bonuses))]

    def store_chunk(c, gate, bonus, mats):
        g_s[chunk_rows(c), :] = gate
        bonus_s[chunk_rows(c), :] = bonus
        for d in range(2):
            m_s[d, c], n_s[d, c], yq_s[d, c], y0_s[d, c] = mats[d]

    def phase_a_trip(i, carry):
        chunks = [i * trip + u for u in range(trip)]
        results = compute_chunks([load_chunk(c) for c in chunks])
        for c, res in zip(chunks, results):
            store_chunk(c, *res)
        return carry

    if static_trip:
        phase_a_trip(0, 0)
    else:
        lax.fori_loop(0, all_chunks // trip, phase_a_trip, 0)

    def phase_b(i, states):
        new_states = []
        for q in range(n_seqs):
            for d, c in ((0, q * n_chunks + i), (1, q * n_chunks + n_chunks - 1 - i)):
                h = states[2 * q + d]
                new_states.append(_dot3(m_s[d, c], h) + n_s[d, c])
                m_s[d, c] = h
        return tuple(new_states)

    zeros_half = jnp.zeros((HD_A, HD_A), F32)
    states = []
    for q in range(n_seqs):
        for d in range(2):
            if zero_init:
                states.append(jnp.zeros((LANES, LANES), F32))
            else:
                top = jnp.concatenate([h0_ref[q, 0, d, 0], zeros_half], axis=1)
                bot = jnp.concatenate([zeros_half, h0_ref[q, 0, d, 1]], axis=1)
                states.append(jnp.concatenate([top, bot], axis=0).T)
    states = tuple(states)
    if static_trip:
        for i in range(n_chunks):
            states = phase_b(i, states)
    else:
        states = lax.fori_loop(0, n_chunks, phase_b, states)
    for q in range(n_seqs):
        for d in range(2):
            s_vk = states[2 * q + d].T
            hfin_ref[q, d, 0] = s_vk[:HD_A, :HD_A]
            hfin_ref[q, d, 1] = s_vk[HD_A:, HD_A:]

    def phase_c_trip(i, carry):
        chunks = [i * trip + u for u in range(trip)]
        y0s = [y0_s[0, c] + y0_s[1, c] for c in chunks]
        extras = [(bonus_s[chunk_rows(c), :], g_s[chunk_rows(c), :]) for c in chunks]
        yfs = [_dot1(yq_s[0, c], m_s[0, c]) for c in chunks]
        ybs = [_dot1(yq_s[1, c], m_s[1, c]) for c in chunks]
        ys = [(f + b + y0)[:C] + (f + b + y0)[C:] for f, b, y0 in zip(yfs, ybs, y0s)]
        mus = [_gsum(y, ones_g) * (1.0 / HD_A) for y in ys]
        ycs = [y - mu for y, mu in zip(ys, mus)]
        vars_ = [_gsum(yc * yc, ones_g) * (1.0 / HD_A) for yc in ycs]
        for c, yc, var, (bonus, gate) in zip(chunks, ycs, vars_, extras):
            yn = yc * lax.rsqrt(var + GN_EPS) * lng_ref[...] + lnb_ref[...]
            out_ref[chunk_rows(c), :] = ((yn + bonus) * gate).astype(out_ref.dtype)
        return carry

    if static_trip:
        phase_c_trip(0, 0)
    else:
        lax.fori_loop(0, all_chunks // trip, phase_c_trip, 0)


def _rwkv(proj_r, p, h0, layer, *, seq_len, n_seq, row_off):
    zero_init = h0 is None
    seqs = max(1, CHUNKS_PER_TRIP // (seq_len // CHUNK))
    blk_rows = seqs * seq_len
    n_chunks = blk_rows // CHUNK
    off = row_off // blk_rows
    n_pairs = DA // LANES
    col = lambda cb: (lambda s, hp: (off + s, cb + hp))
    vec = lambda: pl.BlockSpec((1, LANES), lambda s, hp: (0, hp))
    in_specs = [
        pl.BlockSpec((blk_rows, LANES), col(0)),
        pl.BlockSpec((blk_rows, LANES), col(n_pairs)),
        pl.BlockSpec((blk_rows, LANES), col(2 * n_pairs)),
        pl.BlockSpec((blk_rows, 2 * LANES), lambda s, hp: (off + s, 3 * n_pairs // 2)),
        vec(), vec(), vec(),
        pl.BlockSpec((1, 2 * LANES), lambda s, hp: (0, 0)),
        pl.BlockSpec((2, LANES), lambda s, hp: (0, hp)),
        pl.BlockSpec((2, LANES, LANES), lambda s, hp: (0, 0, hp)),
        pl.BlockSpec((2, LANES), lambda s, hp: (0, hp)),
        pl.BlockSpec((2, LANES, LANES), lambda s, hp: (0, 0, hp)),
        pl.BlockSpec((G_LORA, LANES), lambda s, hp: (0, hp)),
        vec(), vec(), vec(), vec(), vec(),
    ]
    args = [proj_r, proj_r, proj_r, proj_r, p["mu_r"], p["mu_k"], p["mu_v"], p["mu_l"],
            p["w0"], p["w_up"], p["a0"], p["a_up"], p["g_up"], p["k_k"], p["k_a"], p["r_k"],
            p["ln_g"], p["ln_b"]]
    if not zero_init:
        in_specs.append(pl.BlockSpec((seqs, 1, 2, 2, HD_A, HD_A), lambda s, hp: (s, layer, 0, hp, 0, 0)))
        args.append(h0)
    blk = (2 * CHUNK, LANES)
    return pl.pallas_call(
        functools.partial(_rwkv_kernel, seq_len=seq_len, n_seqs=seqs, trip=CHUNKS_PER_TRIP,
                          zero_init=zero_init),
        grid=(n_seq // seqs, n_pairs),
        in_specs=in_specs,
        out_specs=[
            pl.BlockSpec((blk_rows, LANES), lambda s, hp: (s, hp)),
            pl.BlockSpec((seqs, 2, 2, HD_A, HD_A), lambda s, hp: (s, 0, hp, 0, 0)),
        ],
        out_shape=[
            jax.ShapeDtypeStruct((n_seq * seq_len, DA), BF16),
            jax.ShapeDtypeStruct((n_seq, 2, H_A, HD_A, HD_A), F32),
        ],
        scratch_shapes=[
            pltpu.VMEM((2, n_chunks) + blk, F32),
            pltpu.VMEM((2, n_chunks) + blk, F32),
            pltpu.VMEM((2, n_chunks) + blk, F32),
            pltpu.VMEM((2, n_chunks) + blk, F32),
            pltpu.VMEM((blk_rows, LANES), F32),
            pltpu.VMEM((blk_rows, LANES), F32),
        ],
        compiler_params=pltpu.CompilerParams(
            dimension_semantics=("arbitrary", "arbitrary"), vmem_limit_bytes=VMEM_LIMIT),
        name="rwkv_ctx" if zero_init else "rwkv_lat",
    )(*args)


def _rope(x, cos, sin):
    lane = lax.broadcasted_iota(jnp.int32, x.shape, 1)
    swapped = jnp.where((lane % 32) < 16, pltpu.roll(x, LANES - 16, 1), pltpu.roll(x, 16, 1))
    return x * cos + swapped * sin


def _slab_rms(x, g):
    return x * lax.rsqrt(jnp.sum(x * x, axis=-1, keepdims=True) * (1.0 / ROPE_DIM) + EPS) * g


def _kv_expand(ckv_n, kr_att, ukv_ref, kng_ref, k_ref, v_ref):
    kv = _dot(ckv_n.astype(BF16), ukv_ref[...])
    kr_b = kr_att.astype(BF16)
    for h in range(H_B):
        kn = _rms(kv[:, h * NOPE:(h + 1) * NOPE], kng_ref[...])
        k_ref[:, 2 * h * LANES:(2 * h + 1) * LANES] = kn.astype(BF16)
        k_ref[:, (2 * h + 1) * LANES:(2 * h + 2) * LANES] = kr_b
    v_ref[...] = kv[:, H_B * NOPE:].astype(BF16)


def _mla_proj_kernel(pm_ref, cqg_ref, uq_ref, qng_ref, qrg_ref, ckvg_ref, ukv_ref, kng_ref, krg_ref,
                     *rest, rope):
    if rope:
        cos_ref, sin_ref, q_ref, k_ref, v_ref, ckv_ref, kr_ref = rest
    else:
        q_ref, k_ref, v_ref, ckv_ref, kr_ref = rest
    pm = pm_ref[...]
    cq = _rms(pm[:, :Q_LORA], cqg_ref[...])
    q = _dot(cq.astype(BF16), uq_ref[...])
    for h in range(H_B):
        qn = _rms(q[:, 2 * h * LANES:(2 * h + 1) * LANES], qng_ref[...])
        qr = _slab_rms(q[:, (2 * h + 1) * LANES:(2 * h + 2) * LANES], qrg_ref[...])
        if rope:
            qr = _rope(qr, cos_ref[...], sin_ref[...])
        q_ref[:, 2 * h * LANES:(2 * h + 1) * LANES] = (qn * ATTN_SCALE).astype(BF16)
        q_ref[:, (2 * h + 1) * LANES:(2 * h + 2) * LANES] = (qr * ATTN_SCALE).astype(BF16)
    ckv_n = _rms(pm[:, Q_LORA:Q_LORA + KV_LORA], ckvg_ref[...])
    ckv_ref[...] = ckv_n
    kr = _slab_rms(pm[:, Q_LORA + KV_LORA:], krg_ref[...])
    kr_ref[...] = kr[:, :ROPE_DIM]
    kr_att = _rope(kr, cos_ref[...], sin_ref[...]) if rope else kr
    _kv_expand(ckv_n, kr_att, ukv_ref, kng_ref, k_ref, v_ref)


def _mla_proj(proj_m, p, rope_tabs, *, n_rows, row_off, rope):
    t0 = row_off // TM
    full = lambda shape: pl.BlockSpec(shape, lambda t: tuple(0 for _ in shape))
    in_specs = [
        pl.BlockSpec((TM, MLA_PAD), lambda t: (t0 + t, 0)),
        full((1, Q_LORA)), full((Q_LORA, 2 * H_B * LANES)), full((1, LANES)), full((1, LANES)),
        full((1, KV_LORA)), full((KV_LORA, 2 * H_B * NOPE)), full((1, LANES)), full((1, LANES)),
    ]
    args = [proj_m, p["cq_g"], p["uq_w"], p["qn_g"], p["qr_g"], p["ckv_g"], p["ukv_w"], p["kn_g"], p["kr_g"]]
    if rope:
        per_seq = DEC_SEQ // TM
        in_specs += [pl.BlockSpec((TM, LANES), lambda t: (t % per_seq, 0))] * 2
        args += list(rope_tabs)
    rows = lambda w: pl.BlockSpec((TM, w), lambda t: (t, 0))
    return pl.pallas_call(
        functools.partial(_mla_proj_kernel, rope=rope),
        grid=(n_rows // TM,),
        in_specs=in_specs,
        out_specs=[rows(2 * H_B * LANES), rows(2 * H_B * LANES), rows(H_B * V_DIM), rows(KV_LORA), rows(ROPE_DIM)],
        out_shape=[
            jax.ShapeDtypeStruct((n_rows, 2 * H_B * LANES), BF16),
            jax.ShapeDtypeStruct((n_rows, 2 * H_B * LANES), BF16),
            jax.ShapeDtypeStruct((n_rows, H_B * V_DIM), BF16),
            jax.ShapeDtypeStruct((n_rows, KV_LORA), F32),
            jax.ShapeDtypeStruct((n_rows, ROPE_DIM), F32),
        ],
        compiler_params=pltpu.CompilerParams(
            dimension_semantics=("arbitrary",), vmem_limit_bytes=VMEM_LIMIT),
        name="mla_proj_lat" if rope else "mla_proj_ctx",
    )(*args)


def _mla_cache_kernel(ckv_ref, kr_ref, ukv_ref, kng_ref, k_ref, v_ref):
    _kv_expand(ckv_ref[...], kr_ref[...], ukv_ref, kng_ref, k_ref, v_ref)


def _mla_cache_expand(ckv, kr_slab, p):
    n = ckv.shape[0]
    full = lambda shape: pl.BlockSpec(shape, lambda t: tuple(0 for _ in shape))
    return pl.pallas_call(
        _mla_cache_kernel,
        grid=(1,),
        in_specs=[full((n, KV_LORA)), full((n, LANES)), full((KV_LORA, 2 * H_B * NOPE)), full((1, LANES))],
        out_specs=[full((n, 2 * H_B * LANES)), full((n, H_B * V_DIM))],
        out_shape=[
            jax.ShapeDtypeStruct((n, 2 * H_B * LANES), BF16),
            jax.ShapeDtypeStruct((n, H_B * V_DIM), BF16),
        ],
        compiler_params=pltpu.CompilerParams(
            dimension_semantics=("arbitrary",), vmem_limit_bytes=VMEM_LIMIT),
        name="mla_cache_expand",
    )(ckv, kr_slab, p["ukv_w"], p["kn_g"])


def _attn_kernel(q_ref, k_ref, v_ref, *rest, with_ctx):
    if with_ctx:
        k2_ref, v2_ref, o_ref = rest
    else:
        (o_ref,) = rest
    for h in range(H_B):
        hs = slice(2 * h * LANES, (2 * h + 2) * LANES)
        vs = slice(h * V_DIM, (h + 1) * V_DIM)
        q = q_ref[:, hs]
        s = _dot(q, k_ref[:, hs], _NT)
        m = jnp.max(s, axis=-1, keepdims=True)
        if with_ctx:
            s2 = _dot(q, k2_ref[:, hs], _NT)
            m = jnp.maximum(m, jnp.max(s2, axis=-1, keepdims=True))
        pr = jnp.exp(s - m)
        den = jnp.sum(pr, axis=-1, keepdims=True)
        o = _dot(pr.astype(BF16), v_ref[:, vs])
        if with_ctx:
            pr2 = jnp.exp(s2 - m)
            den = den + jnp.sum(pr2, axis=-1, keepdims=True)
            o = o + _dot(pr2.astype(BF16), v2_ref[:, vs])
        o_ref[:, vs] = (o / den).astype(o_ref.dtype)


def _attention(q, k, v, k2=None, v2=None, *, n_batch, seq_len):
    with_ctx = k2 is not None
    q_tiles = seq_len // TM
    kw = 2 * H_B * LANES
    in_specs = [
        pl.BlockSpec((TM, kw), lambda b, t: (b * q_tiles + t, 0)),
        pl.BlockSpec((seq_len, kw), lambda b, t: (b, 0)),
        pl.BlockSpec((seq_len, H_B * V_DIM), lambda b, t: (b, 0)),
    ]
    args = [q, k, v]
    if with_ctx:
        in_specs += [
            pl.BlockSpec((PAST_LEN, kw), lambda b, t: (b, 0)),
            pl.BlockSpec((PAST_LEN, H_B * V_DIM), lambda b, t: (b, 0)),
        ]
        args += [k2, v2]
    return pl.pallas_call(
        functools.partial(_attn_kernel, with_ctx=with_ctx),
        grid=(n_batch, q_tiles),
        in_specs=in_specs,
        out_specs=pl.BlockSpec((TM, H_B * V_DIM), lambda b, t: (b * q_tiles + t, 0)),
        out_shape=jax.ShapeDtypeStruct((n_batch * seq_len, H_B * V_DIM), BF16),
        compiler_params=pltpu.CompilerParams(
            dimension_semantics=("arbitrary", "arbitrary"), vmem_limit_bytes=VMEM_LIMIT),
        name="attn_lat" if with_ctx else "attn_ctx",
    )(*args)


def _pool_rows(h, h_prev, h_next, pos0, seq_len, pw_ref, ps_ref):
    ext = jnp.concatenate([h, h_prev, h_next, jnp.zeros((POOL_K - TM - 2 * HALO, D_MODEL), F32)], axis=0)
    i = lax.broadcasted_iota(jnp.int32, (TM, POOL_K), 0)
    e = lax.broadcasted_iota(jnp.int32, (TM, POOL_K), 1)
    e = jnp.where(e < TM, e, jnp.where(e < TM + HALO, e - TM - HALO, e - HALO))
    pos = pos0 + lax.broadcasted_iota(jnp.int32, (TM, 1), 0)
    groups = []
    for gi, win in enumerate(POOL_WINDOWS):
        half = win // 2
        cols = slice(gi * GP, (gi + 1) * GP)
        band = ((e >= i - half) & (e < i + half)).astype(BF16)
        cnt = (jnp.minimum(pos + half, seq_len) - jnp.maximum(pos - half, 0)).astype(F32)
        d = _dot_exact_lhs(band, ext[:, cols]) / cnt - h[:, cols]
        groups.append(_dot(d.astype(BF16), pw_ref[gi]) * ps_ref[:, cols])
    return jnp.concatenate(groups, axis=1)


def _tail_kernel(*refs, mixer, n_x, row_off):
    x_refs, rest = refs[:n_x], refs[n_x:]
    if mixer == "proj":
        a0_ref, a1_ref, b0_ref, b1_ref, ow_ref, mod_ref, g2_ref, w1_ref, w3_ref, w2_ref, o_ref = rest
    else:
        xp_ref, xn_ref, g1_ref, pw_ref, ps_ref, mod_ref, g2_ref, w1_ref, w3_ref, w2_ref, o_ref = rest
    m = mod_ref[0, 0]
    x = _row_tile(x_refs, TM_FFN)
    if mixer == "proj":
        mix = (_dot(_row_tile((a0_ref, a1_ref), TM_FFN), ow_ref[:DA, :])
               + _dot(_row_tile((b0_ref, b1_ref), TM_FFN), ow_ref[DA:, :]))
    else:
        row0 = row_off + pl.program_id(0) * TM_FFN
        is_lat = row0 >= N_CTX
        seq_len = jnp.where(is_lat, DEC_SEQ, SEQ)
        norm_mod = lambda v: _rms(v, g1_ref[...]) * (1.0 + m[1:2]) + m[0:1]
        h, hp, hn = norm_mod(x), norm_mod(xp_ref[...]), norm_mod(xn_ref[...])
        parts = []
        n_sub = TM_FFN // TM
        for k in range(n_sub):
            r0 = row0 + k * TM
            pos0 = jnp.where(is_lat, (r0 - N_CTX) % DEC_SEQ, r0 % SEQ)
            prev = hp if k == 0 else h[k * TM - HALO:k * TM]
            nxt = hn if k == n_sub - 1 else h[(k + 1) * TM:(k + 1) * TM + HALO]
            prev = jnp.where(pos0 > 0, prev, 0.0)
            nxt = jnp.where(pos0 + TM < seq_len, nxt, 0.0)
            parts.append(_pool_rows(h[k * TM:(k + 1) * TM], prev, nxt, pos0, seq_len, pw_ref, ps_ref))
        mix = jnp.concatenate(parts, axis=0)
    x1 = x + m[2:3] * mix
    h2 = (_rms(x1, g2_ref[...]) * (1.0 + m[4:5]) + m[3:4]).astype(BF16)
    acc = None
    for lo, hi in FF_SPLITS:
        u = _silu(_dot(h2, w1_ref[0, :, lo:hi])) * _dot(h2, w3_ref[0, :, lo:hi])
        part = _dot(u.astype(BF16), w2_ref[0, lo:hi, :])
        acc = part if acc is None else acc + part
    o_ref[...] = x1 + m[5:6] * acc


def _layer_tail(x, mods, l, g2, w1, w3, w2, *, proj=None, pool=None, row_off=0, n_rows=N_TOK):
    per_tile = TM_FFN // TM
    t0 = row_off // TM_FFN
    const = lambda shape: pl.BlockSpec(shape, lambda i: tuple(0 for _ in shape))
    resident = lambda shape: pl.BlockSpec((1,) + shape, lambda i: (l, 0, 0), pipeline_mode=pl.Buffered(1))
    if proj is not None:
        assert row_off == 0 and n_rows == N_TOK
        a_out, b_out, out_w = proj
        x_specs, x_args = _row_specs(x, D_MODEL, TM_FFN)
        a_specs, a_args = _row_specs(a_out, DA, TM_FFN)
        b_specs, b_args = _row_specs(b_out, DA, TM_FFN)
        mix_specs = a_specs + b_specs + [const((D_MODEL, D_MODEL))]
        mix_args = a_args + b_args + [out_w]
        mixer = "proj"
    else:
        g1, pw, ps = pool
        r = TM_FFN // HALO
        last = N_TOK // HALO - 1
        x_specs = [pl.BlockSpec((TM_FFN, D_MODEL), lambda i: (t0 + i, 0))]
        x_args = [x]
        mix_specs = [
            pl.BlockSpec((HALO, D_MODEL), lambda i: (jnp.maximum((t0 + i) * r - 1, 0), 0)),
            pl.BlockSpec((HALO, D_MODEL), lambda i: (jnp.minimum((t0 + i + 1) * r, last), 0)),
            const((1, D_MODEL)), const((len(POOL_WINDOWS), GP, GP)), const((1, D_MODEL)),
        ]
        mix_args = [x, x, g1, pw, ps]
        mixer = "pool"
    return pl.pallas_call(
        functools.partial(_tail_kernel, mixer=mixer, n_x=len(x_args), row_off=row_off),
        grid=(n_rows // TM_FFN,),
        in_specs=x_specs + mix_specs + [
            pl.BlockSpec((1, 1, N_MOD, D_MODEL), lambda i: (l, _cond_of_tile((t0 + i) * per_tile), 0, 0)),
            const((1, D_MODEL)),
            resident((D_MODEL, D_FF)), resident((D_MODEL, D_FF)), resident((D_FF, D_MODEL)),
        ],
        out_specs=pl.BlockSpec((TM_FFN, D_MODEL), lambda i: (i, 0)),
        out_shape=jax.ShapeDtypeStruct((n_rows, D_MODEL), F32),
        compiler_params=pltpu.CompilerParams(
            dimension_semantics=("arbitrary",), vmem_limit_bytes=VMEM_LIMIT),
        name="tail_" + mixer,
    )(*x_args, *mix_args, mods, g2, w1, w3, w2)


def _rope_tables():
    pos = jnp.arange(DEC_SEQ)
    freqs = ROPE_BASE ** (-jnp.arange(ROPE_AXIS_FREQS, dtype=F32) / ROPE_AXIS_FREQS)
    ang_row = (pos // GRID_W).astype(F32)[:, None] * freqs
    ang_col = (pos % GRID_W).astype(F32)[:, None] * freqs
    zeros = jnp.zeros((DEC_SEQ, LANES - ROPE_DIM), F32)
    cos = jnp.concatenate([jnp.cos(ang_row)] * 2 + [jnp.cos(ang_col)] * 2 + [zeros], axis=1)
    sin = jnp.concatenate([-jnp.sin(ang_row), jnp.sin(ang_row), -jnp.sin(ang_col), jnp.sin(ang_col), zeros], axis=1)
    return cos, sin


def _slab(v):
    return jnp.pad(v, (0, LANES - v.shape[0]))[None]


def _even_params(i, in_w, out_w, shift_mu, w0, w_up, a0, a_up, g_up, k_k, k_a, r_k, ln_g, ln_b,
                 cq_g, uq_w, ckv_g, ukv_w, qn_g, qr_g, kn_g, kr_g):
    row = lambda v: v[None]
    in_w_pad = jnp.pad(in_w[i], ((0, 0), (0, MLA_PAD - MLA_IN))).astype(BF16)
    mu = shift_mu[i]
    w_up_pad = jnp.pad(w_up[i], ((0, 0), (0, A_LORA), (0, 0)))
    a_up_pad = jnp.pad(a_up[i], ((0, 0), (W_LORA, 0), (0, 0)))
    uq = uq_w[i].reshape(Q_LORA, H_B, NOPE + ROPE_DIM)
    uq = jnp.pad(uq, ((0, 0), (0, 0), (0, 2 * LANES - NOPE - ROPE_DIM))).reshape(Q_LORA, 2 * H_B * LANES)
    ukv = ukv_w[i].reshape(KV_LORA, H_B, NOPE + V_DIM)
    ukv = jnp.concatenate([ukv[:, :, :NOPE].reshape(KV_LORA, -1), ukv[:, :, NOPE:].reshape(KV_LORA, -1)], axis=1)
    rw = dict(mu_r=row(mu[:DA]), mu_k=row(mu[DA:2 * DA]), mu_v=row(mu[2 * DA:3 * DA]), mu_l=row(mu[3 * DA:]),
              w0=w0[i], w_up=w_up_pad, a0=a0[i], a_up=a_up_pad, g_up=g_up[i],
              k_k=row(k_k[i]), k_a=row(k_a[i]), r_k=row(r_k[i].reshape(DA)), ln_g=row(ln_g[i]), ln_b=row(ln_b[i]))
    mla = dict(cq_g=row(cq_g[i]), uq_w=uq.astype(BF16), qn_g=row(qn_g[i]), qr_g=_slab(qr_g[i]),
               ckv_g=row(ckv_g[i]), ukv_w=ukv.astype(BF16), kn_g=row(kn_g[i]), kr_g=_slab(kr_g[i]))
    return in_w_pad, out_w[i].astype(BF16), rw, mla


def kernel(x_prompt, x_sample, c, state_rwkv, cache_mla_ckv, cache_mla_krope, c_ctx,
           norm1_g, norm2_g, ada_w, ada_b, ffn_w1, ffn_w3, ffn_w2,
           in_w, out_w, shift_mu, rwkv_w0, rwkv_w_up, rwkv_a0, rwkv_a_up, rwkv_g_up,
           rwkv_k_k, rwkv_k_a, rwkv_r_k, rwkv_ln_g, rwkv_ln_b,
           mla_cq_g, mla_uq_w, mla_ckv_g, mla_ukv_w, mla_qn_g, mla_qr_g, mla_kn_g, mla_kr_g,
           pool_w, pool_scale):
    x = (x_prompt.reshape(N_CTX, D_MODEL), x_sample.reshape(N_LAT, D_MODEL))
    conds = jnp.concatenate([c_ctx[None], c, jnp.zeros((8 - 1 - DEC_BATCH, D_MODEL), F32)], axis=0)
    mods = _adaln(conds, ada_w, ada_b).reshape(DEPTH, 8, N_MOD, D_MODEL)
    rope_tabs = _rope_tables()
    w1_all, w3_all, w2_all = ffn_w1.astype(BF16), ffn_w3.astype(BF16), ffn_w2.astype(BF16)

    new_s, new_ckv, new_kr = [], [], []
    for l in range(DEPTH):
        n1 = norm1_g[l][None]
        if l % 2 == 0:
            i = l // 2
            in_w_pad, out_w_b, rw, mla = _even_params(
                i, in_w, out_w, shift_mu, rwkv_w0, rwkv_w_up, rwkv_a0, rwkv_a_up, rwkv_g_up,
                rwkv_k_k, rwkv_k_a, rwkv_r_k, rwkv_ln_g, rwkv_ln_b,
                mla_cq_g, mla_uq_w, mla_ckv_g, mla_ukv_w, mla_qn_g, mla_qr_g, mla_kn_g, mla_kr_g)
            proj_r, proj_m = _inproj(x, mods, l, n1, in_w_pad)
            a_ctx, s_ctx = _rwkv(proj_r, rw, None, i, seq_len=SEQ, n_seq=BATCH, row_off=0)
            a_lat, _ = _rwkv(proj_r, rw, state_rwkv, i, seq_len=DEC_SEQ, n_seq=DEC_BATCH, row_off=N_CTX)
            q_c, k_c, v_c, ckv_c, kr_c = _mla_proj(proj_m, mla, None, n_rows=N_CTX, row_off=0, rope=False)
            q_l, k_l, v_l, _, _ = _mla_proj(proj_m, mla, rope_tabs, n_rows=N_LAT, row_off=N_CTX, rope=True)
            kr_cache = jnp.pad(cache_mla_krope[:, i].reshape(DEC_BATCH * PAST_LEN, ROPE_DIM),
                               ((0, 0), (0, LANES - ROPE_DIM)))
            k_p, v_p = _mla_cache_expand(cache_mla_ckv[:, i].reshape(DEC_BATCH * PAST_LEN, KV_LORA), kr_cache, mla)
            b_ctx = _attention(q_c, k_c, v_c, n_batch=BATCH, seq_len=SEQ)
            b_lat = _attention(q_l, k_l, v_l, k_p, v_p, n_batch=DEC_BATCH, seq_len=DEC_SEQ)
            mixer = dict(proj=((a_ctx, a_lat), (b_ctx, b_lat), out_w_b))
            new_s.append(s_ctx)
            new_ckv.append(ckv_c.reshape(BATCH, SEQ, KV_LORA))
            new_kr.append(kr_c.reshape(BATCH, SEQ, ROPE_DIM))
        else:
            j = l // 2
            mixer = dict(pool=(n1, pool_w[j].astype(BF16), pool_scale[j][None]))
        tail_args = (mods, l, norm2_g[l][None], w1_all, w3_all, w2_all)
        if l < DEPTH - 1:
            x = _layer_tail(x, *tail_args, **mixer)
        else:
            y_p = _layer_tail(x, *tail_args, **mixer, row_off=0, n_rows=N_CTX).reshape(BATCH, SEQ, D_MODEL)
            y_s = _layer_tail(x, *tail_args, **mixer, row_off=N_CTX, n_rows=N_LAT).reshape(DEC_BATCH, DEC_SEQ, D_MODEL)
    return (y_p, y_s, jnp.stack(new_s, axis=1), jnp.stack(new_ckv, axis=1), jnp.stack(new_kr, axis=1))
```

```python
import functools

import jax
import jax.numpy as jnp
from jax import lax
from jax.experimental import pallas as pl
from jax.experimental.pallas import tpu as pltpu

D_MODEL = 1024
BATCH, SEQ = 32, 256
DEC_BATCH, DEC_SEQ = 2, 2048
DEPTH = 4
PAST_LEN = 256
GRID_W = 64
N_MOD = 6
D_FF = 2816
DA = 512
HD_A = 64
H_A = 8
W_LORA, A_LORA, G_LORA = 64, 64, 128
H_B = 4
NOPE, ROPE_DIM, V_DIM = 128, 64, 128
Q_LORA, KV_LORA = 384, 256
ATTN_SCALE = (NOPE + ROPE_DIM) ** -0.5
ROPE_AXIS_FREQS = ROPE_DIM // 4
ROPE_BASE = 10000.0
RWKV_IN = 3 * DA + W_LORA + A_LORA + G_LORA
MLA_IN = Q_LORA + KV_LORA + ROPE_DIM
MLA_PAD = 768
POOL_WINDOWS = (2, 4, 8, 16)
GP = 256
EPS = 1e-6
GN_EPS = 64e-5

N_CTX = BATCH * SEQ
N_LAT = DEC_BATCH * DEC_SEQ
N_TOK = N_CTX + N_LAT

LANES = 128
TM = 256
N_TILES = N_TOK // TM
CTX_TILES = N_CTX // TM
LAT_TILES_PER_SEQ = DEC_SEQ // TM
CHUNK = 64
CHUNKS_PER_TRIP = 8
HALO = 8
POOL_K = 384
TM_FFN = 512
MXU_WIDTH = 256
FF_SPLITS = ((0, 6 * MXU_WIDTH), (6 * MXU_WIDTH, D_FF))
VMEM_LIMIT = 56 * 1024 * 1024

F32 = jnp.float32
BF16 = jnp.bfloat16


def _cond_of_tile(t):
    return jnp.where(t < CTX_TILES, 0, 1 + (t - CTX_TILES) // LAT_TILES_PER_SEQ)


def _dot(a, b, dims=((1,), (0,))):
    return lax.dot_general(a, b, (dims, ((), ())), preferred_element_type=F32)


_NT = ((1,), (1,))


def _split2(x):
    hi = x.astype(BF16)
    lo = (x - hi.astype(F32)).astype(BF16)
    return hi, lo


def _dot1(a, b, dims=((1,), (0,))):
    return _dot(a.astype(BF16), b.astype(BF16), dims)


def _dot3(a, b, dims=((1,), (0,))):
    ah, al = _split2(a)
    bh, bl = _split2(b)
    return _dot(ah, bh, dims) + (_dot(ah, bl, dims) + _dot(al, bh, dims))


def _dot_exact_lhs(a_bf16, b):
    b0, b1 = _split2(b)
    return _dot(a_bf16, b0) + _dot(a_bf16, b1)


def _rms(x, g):
    return x * lax.rsqrt(jnp.mean(x * x, axis=-1, keepdims=True) + EPS) * g


def _sigmoid(x):
    return 1.0 / (1.0 + jnp.exp(-x))


def _silu(x):
    return x * _sigmoid(x)


def _softplus(x):
    return jnp.maximum(x, 0.0) + jnp.log(1.0 + jnp.exp(-jnp.abs(x)))


def _group_ones(n, g):
    i = lax.broadcasted_iota(jnp.int32, (n, n), 0) // g
    j = lax.broadcasted_iota(jnp.int32, (n, n), 1) // g
    return (i == j).astype(BF16)


def _gsum(x, ones):
    hi, lo = _split2(x)
    return _dot(hi, ones) + _dot(lo, ones)


def _adaln_kernel(c_ref, w_ref, b_ref, o_ref):
    o_ref[0] = _dot3(_silu(c_ref[...]), w_ref[0]) + b_ref[0]


def _adaln(conds, ada_w, ada_b):
    tn = 1536
    return pl.pallas_call(
        _adaln_kernel,
        grid=(DEPTH, N_MOD * D_MODEL // tn),
        in_specs=[
            pl.BlockSpec((8, D_MODEL), lambda l, j: (0, 0)),
            pl.BlockSpec((1, D_MODEL, tn), lambda l, j: (l, 0, j)),
            pl.BlockSpec((1, 1, tn), lambda l, j: (l, 0, j)),
        ],
        out_specs=pl.BlockSpec((1, 8, tn), lambda l, j: (l, 0, j)),
        out_shape=jax.ShapeDtypeStruct((DEPTH, 8, N_MOD * D_MODEL), F32),
        compiler_params=pltpu.CompilerParams(
            dimension_semantics=("arbitrary", "arbitrary"), vmem_limit_bytes=VMEM_LIMIT),
        name="adaln",
    )(conds, ada_w, ada_b.reshape(DEPTH, 1, N_MOD * D_MODEL))


def _mod_spec(l):
    return pl.BlockSpec((1, 1, N_MOD, D_MODEL), lambda t, *_: (l, _cond_of_tile(t), 0, 0))


def _row_specs(x, width, tm=TM):
    if not isinstance(x, tuple):
        return [pl.BlockSpec((tm, width), lambda t, *_: (t, 0))], [x]
    ctx_tiles = N_CTX // tm
    return [pl.BlockSpec((tm, width), lambda t, *_: (jnp.minimum(t, ctx_tiles - 1), 0)),
            pl.BlockSpec((tm, width), lambda t, *_: (jnp.maximum(t - ctx_tiles, 0), 0))], list(x)


def _row_tile(refs, tm=TM):
    if len(refs) == 1:
        return refs[0][...]
    return jnp.where(pl.program_id(0) < N_CTX // tm, refs[0][...], refs[1][...])


def _inproj_kernel(*refs):
    *x_refs, mod_ref, g_ref, w_ref, or_ref, om_ref = refs
    m = mod_ref[0, 0]
    h = _rms(_row_tile(x_refs), g_ref[...]) * (1.0 + m[1:2]) + m[0:1]
    p = _dot(h.astype(BF16), w_ref[...])
    or_ref[...] = p[:, :RWKV_IN]
    om_ref[...] = p[:, RWKV_IN:]


def _inproj(x, mods, l, g, w):
    x_specs, x_args = _row_specs(x, D_MODEL)
    return pl.pallas_call(
        _inproj_kernel,
        grid=(N_TILES,),
        in_specs=x_specs + [
            _mod_spec(l),
            pl.BlockSpec((1, D_MODEL), lambda t: (0, 0)),
            pl.BlockSpec((D_MODEL, RWKV_IN + MLA_PAD), lambda t: (0, 0)),
        ],
        out_specs=[
            pl.BlockSpec((TM, RWKV_IN), lambda t: (t, 0)),
            pl.BlockSpec((TM, MLA_PAD), lambda t: (t, 0)),
        ],
        out_shape=[
            jax.ShapeDtypeStruct((N_TOK, RWKV_IN), F32),
            jax.ShapeDtypeStruct((N_TOK, MLA_PAD), F32),
        ],
        compiler_params=pltpu.CompilerParams(
            dimension_semantics=("arbitrary",), vmem_limit_bytes=VMEM_LIMIT),
        name="inproj",
    )(*x_args, mods, g, w)


def _rwkv_kernel(pr_ref, pk_ref, pv_ref, pl_ref, mur_ref, muk_ref, muv_ref, mul_ref,
                 w0_ref, wup_ref, a0_ref, aup_ref, gup_ref, kkg_ref, kag_ref, rkg_ref,
                 lng_ref, lnb_ref, *rest, seq_len, n_seqs, trip, zero_init):
    if zero_init:
        h0_ref = None
        out_ref, hfin_ref, m_s, n_s, yq_s, y0_s, g_s, bonus_s = rest
    else:
        h0_ref, out_ref, hfin_ref, m_s, n_s, yq_s, y0_s, g_s, bonus_s = rest
    C = CHUNK
    n_chunks = seq_len // C
    all_chunks = n_seqs * n_chunks
    static_trip = all_chunks == trip
    lane = lax.broadcasted_iota(jnp.int32, (1, LANES), 1)
    m0 = (lane < HD_A).astype(F32)
    m1 = 1.0 - m0
    ones_g = _group_ones(LANES, HD_A)
    row = lax.broadcasted_iota(jnp.int32, (C, 1), 0)
    i2 = lax.broadcasted_iota(jnp.int32, (2 * C, 2 * C), 0)
    j2 = lax.broadcasted_iota(jnp.int32, (2 * C, 2 * C), 1)
    same_head = (i2 // C) == (j2 // C)
    eye2 = (i2 == j2).astype(F32)
    ic = lax.broadcasted_iota(jnp.int32, (C, C), 0)
    jc = lax.broadcasted_iota(jnp.int32, (C, C), 1)
    incl_c = ((jc <= ic).astype(BF16), (jc >= ic).astype(BF16))
    incl_blk = (same_head & (j2 <= i2), same_head & (j2 >= i2))
    strict_blk = (same_head & (j2 < i2), same_head & (j2 > i2))

    def pair_masks(d):
        masks = []
        s = 1
        while s < C:
            later, earlier = ((i2 // s) % 2, (j2 // s) % 2) if d == 0 else ((j2 // s) % 2, (i2 // s) % 2)
            masks.append(((i2 // (2 * s)) == (j2 // (2 * s))) & (later == 1) & (earlier == 0))
            s *= 2
        return masks

    pair_blk = (pair_masks(0), pair_masks(1))

    def stack(x):
        return jnp.concatenate([x * m0, x * m1], axis=0)

    def chunk_rows(c):
        return pl.ds(c * C, C) if isinstance(c, int) else pl.ds(pl.multiple_of(c * C, C), C)

    def shifted(ref, mu, c):
        x = ref[chunk_rows(c), :]
        if isinstance(c, int):
            first, last = c % n_chunks == 0, c % n_chunks == n_chunks - 1
            prev_row = jnp.zeros((1, x.shape[1]), F32) if first else ref[c * C - 1:c * C, :]
            next_row = jnp.zeros((1, x.shape[1]), F32) if last else ref[(c + 1) * C:(c + 1) * C + 1, :]
        else:
            assert n_seqs == 1
            pstart = pl.multiple_of(jnp.maximum(c * C - 8, 0), 8)
            nstart = pl.multiple_of(jnp.minimum((c + 1) * C, seq_len - 8), 8)
            prev_row = jnp.where(c > 0, ref[pl.ds(pstart, 8), :][7:8], 0.0)
            next_row = jnp.where(c < n_chunks - 1, ref[pl.ds(nstart, 8), :][0:1], 0.0)
        prev = jnp.where(row == 0, prev_row, pltpu.roll(x, 1, 0))
        nxt = jnp.where(row == C - 1, next_row, pltpu.roll(x, C - 1, 0))
        return x + mu * (0.5 * (prev + nxt) - x)

    def chain_mats(chains):
        ds = [ch[0] for ch in chains]
        lp = [_dot_exact_lhs(incl_c[d], logw) for d, *_, logw in chains]
        lpc = [x[C - 1:C] if d == 0 else x[0:1] for d, x in zip(ds, lp)]
        st = []
        for (d, r, kd, v, kk, b, logw), x, xc in zip(chains, lp, lpc):
            p_in, p_ex, p_inv, p_end = jnp.exp(x), jnp.exp(x - logw), jnp.exp(-x), jnp.exp(xc - x)
            st.append(dict(qs=stack(r * p_in), a_s=stack(kk * p_ex), khs=stack(kd * p_inv),
                           bs=stack(b * p_inv), kbs=stack(kd * p_end), bbs=stack(b * p_end), vs=stack(v)))
        R = 2 * C
        quad = [_dot1(jnp.concatenate([s["a_s"], s["qs"]], axis=0),
                      jnp.concatenate([s["bs"], s["khs"]], axis=0), _NT) for s in st]
        lab = [jnp.where(strict_blk[d], q[:R, :R], 0.0) for d, q in zip(ds, quad)]
        lak = [jnp.where(strict_blk[d], q[:R, R:], 0.0) for d, q in zip(ds, quad)]
        dqb = [jnp.where(incl_blk[d], q[R:, :R], 0.0) for d, q in zip(ds, quad)]
        dqk = [jnp.where(incl_blk[d], q[R:, R:], 0.0) for d, q in zip(ds, quad)]
        x2 = [_dot1(l, s["vs"]) for l, s in zip(lak, st)]
        tinv = [eye2 - jnp.where(pair_blk[d][0], l, 0.0) for d, l in zip(ds, lab)]
        for lvl in range(1, len(pair_blk[0])):
            half = [_dot1(t, jnp.where(pair_blk[d][lvl], l, 0.0)) for d, t, l in zip(ds, tinv, lab)]
            tinv = [t - _dot1(h, t) for h, t in zip(half, tinv)]
        w12 = [_dot1(t, jnp.concatenate([s["a_s"], x], axis=1)) for t, s, x in zip(tinv, st, x2)]
        zero_blk = jnp.zeros((R, R), F32)
        fin = [_dot1(jnp.concatenate([jnp.concatenate([s["bbs"].T, -s["kbs"].T], axis=1),
                                      jnp.concatenate([qb, -qk], axis=1)], axis=0),
                     jnp.concatenate([w, jnp.concatenate([zero_blk, s["vs"]], axis=1)], axis=0))
               for s, qb, qk, w in zip(st, dqb, dqk, w12)]
        return [(eye2 * jnp.exp(xc) - f[:R, :R], -f[:R, R:], s["qs"] - f[R:, :R], -f[R:, R:])
                for xc, f, s in zip(lpc, fin, st)]

    def load_chunk(c):
        return (shifted(pr_ref, mur_ref[...], c), shifted(pk_ref, muk_ref[...], c),
                shifted(pv_ref, muv_ref[...], c), shifted(pl_ref, mul_ref[...], c))

    def compute_chunks(loaded):
        rows_of = lambda j: slice(j * C, (j + 1) * C)
        r_all, k_all, v_all, lo_all = (jnp.concatenate([ch[j] for ch in loaded], axis=0) for j in range(4))
        wa = lo_all[:, :LANES]
        gate_all = _dot1(_sigmoid(lo_all[:, LANES:]), gup_ref[...])
        kkr = k_all * kkg_ref[...]
        kk_all = kkr / (jnp.sqrt(_gsum(kkr * kkr, ones_g)) + 1e-12)
        bonus_all = _gsum(r_all * k_all * rkg_ref[...], ones_g) * v_all
        both = lambda ref: jnp.concatenate([ref[0], ref[1]], axis=1)
        both_rows = lambda ref: jnp.concatenate([ref[0:1, :], ref[1:2, :]], axis=1)
        wl = both_rows(w0_ref) + _dot3(jnp.tanh(wa), both(wup_ref))
        logw_all = -jnp.exp(-_softplus(-wl) - 0.5)
        a_all = _sigmoid(both_rows(a0_ref) + _dot3(wa, both(aup_ref)))
        chains = []
        for j, (r, k, v, _) in enumerate(loaded):
            kk = kk_all[rows_of(j)]
            for d in range(2):
                a = a_all[rows_of(j), d * LANES:(d + 1) * LANES]
                logw = logw_all[rows_of(j), d * LANES:(d + 1) * LANES]
                chains.append((d, r, k * (1.0 + (a - 1.0) * kag_ref[...]), v, kk, kk * a, logw))
        mats = chain_mats(chains)
        return [(gate_all[rows_of(j)], bonus_all[rows_of(j)], mats[2 * j:2 * j + 2]) for j in range(len(loaded))]

    def store_chunk(c, gate, bonus, mats):
        g_s[chunk_rows(c), :] = gate
        bonus_s[chunk_rows(c), :] = bonus
        for d in range(2):
            m_s[d, c], n_s[d, c], yq_s[d, c], y0_s[d, c] = mats[d]

    def phase_a_trip(i, carry):
        chunks = [i * trip + u for u in range(trip)]
        results = compute_chunks([load_chunk(c) for c in chunks])
        for c, res in zip(chunks, results):
            store_chunk(c, *res)
        return carry

    if static_trip:
        phase_a_trip(0, 0)
    else:
        lax.fori_loop(0, all_chunks // trip, phase_a_trip, 0)

    def phase_b(i, states):
        new_states = []
        for q in range(n_seqs):
            for d, c in ((0, q * n_chunks + i), (1, q * n_chunks + n_chunks - 1 - i)):
                h = states[2 * q + d]
                new_states.append(_dot3(m_s[d, c], h) + n_s[d, c])
                m_s[d, c] = h
        return tuple(new_states)

    zeros_half = jnp.zeros((HD_A, HD_A), F32)
    states = []
    for q in range(n_seqs):
        for d in range(2):
            if zero_init:
                states.append(jnp.zeros((LANES, LANES), F32))
            else:
                top = jnp.concatenate([h0_ref[q, 0, d, 0], zeros_half], axis=1)
                bot = jnp.concatenate([zeros_half, h0_ref[q, 0, d, 1]], axis=1)
                states.append(jnp.concatenate([top, bot], axis=0).T)
    states = tuple(states)
    if static_trip:
        for i in range(n_chunks):
            states = phase_b(i, states)
    else:
        states = lax.fori_loop(0, n_chunks, phase_b, states)
    for q in range(n_seqs):
        for d in range(2):
            s_vk = states[2 * q + d].T
            hfin_ref[q, d, 0] = s_vk[:HD_A, :HD_A]
            hfin_ref[q, d, 1] = s_vk[HD_A:, HD_A:]

    def phase_c_trip(i, carry):
        chunks = [i * trip + u for u in range(trip)]
        y0s = [y0_s[0, c] + y0_s[1, c] for c in chunks]
        extras = [(bonus_s[chunk_rows(c), :], g_s[chunk_rows(c), :]) for c in chunks]
        yfs = [_dot1(yq_s[0, c], m_s[0, c]) for c in chunks]
        ybs = [_dot1(yq_s[1, c], m_s[1, c]) for c in chunks]
        ys = [(f + b + y0)[:C] + (f + b + y0)[C:] for f, b, y0 in zip(yfs, ybs, y0s)]
        y_all = jnp.concatenate(ys, axis=0)
        yc = y_all - _gsum(y_all, ones_g) * (1.0 / HD_A)
        var = _gsum(yc * yc, ones_g) * (1.0 / HD_A)
        yn = yc * lax.rsqrt(var + GN_EPS) * lng_ref[...] + lnb_ref[...]
        for j, (c, (bonus, gate)) in enumerate(zip(chunks, extras)):
            out_ref[chunk_rows(c), :] = ((yn[j * C:(j + 1) * C] + bonus) * gate).astype(out_ref.dtype)
        return carry

    if static_trip:
        phase_c_trip(0, 0)
    else:
        lax.fori_loop(0, all_chunks // trip, phase_c_trip, 0)


def _rwkv(proj_r, p, h0, layer, *, seq_len, n_seq, row_off):
    zero_init = h0 is None
    seqs = max(1, CHUNKS_PER_TRIP // (seq_len // CHUNK))
    blk_rows = seqs * seq_len
    n_chunks = blk_rows // CHUNK
    off = row_off // blk_rows
    n_pairs = DA // LANES
    col = lambda cb: (lambda s, hp: (off + s, cb + hp))
    vec = lambda: pl.BlockSpec((1, LANES), lambda s, hp: (0, hp))
    in_specs = [
        pl.BlockSpec((blk_rows, LANES), col(0)),
        pl.BlockSpec((blk_rows, LANES), col(n_pairs)),
        pl.BlockSpec((blk_rows, LANES), col(2 * n_pairs)),
        pl.BlockSpec((blk_rows, 2 * LANES), lambda s, hp: (off + s, 3 * n_pairs // 2)),
        vec(), vec(), vec(),
        pl.BlockSpec((1, 2 * LANES), lambda s, hp: (0, 0)),
        pl.BlockSpec((2, LANES), lambda s, hp: (0, hp)),
        pl.BlockSpec((2, LANES, LANES), lambda s, hp: (0, 0, hp)),
        pl.BlockSpec((2, LANES), lambda s, hp: (0, hp)),
        pl.BlockSpec((2, LANES, LANES), lambda s, hp: (0, 0, hp)),
        pl.BlockSpec((G_LORA, LANES), lambda s, hp: (0, hp)),
        vec(), vec(), vec(), vec(), vec(),
    ]
    args = [proj_r, proj_r, proj_r, proj_r, p["mu_r"], p["mu_k"], p["mu_v"], p["mu_l"],
            p["w0"], p["w_up"], p["a0"], p["a_up"], p["g_up"], p["k_k"], p["k_a"], p["r_k"],
            p["ln_g"], p["ln_b"]]
    if not zero_init:
        in_specs.append(pl.BlockSpec((seqs, 1, 2, 2, HD_A, HD_A), lambda s, hp: (s, layer, 0, hp, 0, 0)))
        args.append(h0)
    blk = (2 * CHUNK, LANES)
    return pl.pallas_call(
        functools.partial(_rwkv_kernel, seq_len=seq_len, n_seqs=seqs, trip=CHUNKS_PER_TRIP,
                          zero_init=zero_init),
        grid=(n_seq // seqs, n_pairs),
        in_specs=in_specs,
        out_specs=[
            pl.BlockSpec((blk_rows, LANES), lambda s, hp: (s, hp)),
            pl.BlockSpec((seqs, 2, 2, HD_A, HD_A), lambda s, hp: (s, 0, hp, 0, 0)),
        ],
        out_shape=[
            jax.ShapeDtypeStruct((n_seq * seq_len, DA), BF16),
            jax.ShapeDtypeStruct((n_seq, 2, H_A, HD_A, HD_A), F32),
        ],
        scratch_shapes=[
            pltpu.VMEM((2, n_chunks) + blk, F32),
            pltpu.VMEM((2, n_chunks) + blk, F32),
            pltpu.VMEM((2, n_chunks) + blk, F32),
            pltpu.VMEM((2, n_chunks) + blk, F32),
            pltpu.VMEM((blk_rows, LANES), F32),
            pltpu.VMEM((blk_rows, LANES), F32),
        ],
        compiler_params=pltpu.CompilerParams(
            dimension_semantics=("arbitrary", "arbitrary"), vmem_limit_bytes=VMEM_LIMIT),
        name="rwkv_ctx" if zero_init else "rwkv_lat",
    )(*args)


def _rope(x, cos, sin):
    lane = lax.broadcasted_iota(jnp.int32, x.shape, 1)
    swapped = jnp.where((lane % 32) < 16, pltpu.roll(x, LANES - 16, 1), pltpu.roll(x, 16, 1))
    return x * cos + swapped * sin


def _slab_rms(x, g):
    return x * lax.rsqrt(jnp.sum(x * x, axis=-1, keepdims=True) * (1.0 / ROPE_DIM) + EPS) * g


def _kv_expand(ckv_n, kr_att, ukv_ref, kng_ref, k_ref, v_ref):
    kv = _dot(ckv_n.astype(BF16), ukv_ref[...])
    kr_b = kr_att.astype(BF16)
    for h in range(H_B):
        kn = _rms(kv[:, h * NOPE:(h + 1) * NOPE], kng_ref[...])
        k_ref[:, 2 * h * LANES:(2 * h + 1) * LANES] = kn.astype(BF16)
        k_ref[:, (2 * h + 1) * LANES:(2 * h + 2) * LANES] = kr_b
    v_ref[...] = kv[:, H_B * NOPE:].astype(BF16)


def _mla_proj_kernel(pm_ref, cqg_ref, uq_ref, qng_ref, qrg_ref, ckvg_ref, ukv_ref, kng_ref, krg_ref,
                     *rest, rope):
    if rope:
        cos_ref, sin_ref, q_ref, k_ref, v_ref, ckv_ref, kr_ref = rest
    else:
        q_ref, k_ref, v_ref, ckv_ref, kr_ref = rest
    pm = pm_ref[...]
    cq = _rms(pm[:, :Q_LORA], cqg_ref[...])
    q = _dot(cq.astype(BF16), uq_ref[...])
    for h in range(H_B):
        qn = _rms(q[:, 2 * h * LANES:(2 * h + 1) * LANES], qng_ref[...])
        qr = _slab_rms(q[:, (2 * h + 1) * LANES:(2 * h + 2) * LANES], qrg_ref[...])
        if rope:
            qr = _rope(qr, cos_ref[...], sin_ref[...])
        q_ref[:, 2 * h * LANES:(2 * h + 1) * LANES] = (qn * ATTN_SCALE).astype(BF16)
        q_ref[:, (2 * h + 1) * LANES:(2 * h + 2) * LANES] = (qr * ATTN_SCALE).astype(BF16)
    ckv_n = _rms(pm[:, Q_LORA:Q_LORA + KV_LORA], ckvg_ref[...])
    ckv_ref[...] = ckv_n
    kr = _slab_rms(pm[:, Q_LORA + KV_LORA:], krg_ref[...])
    kr_ref[...] = kr[:, :ROPE_DIM]
    kr_att = _rope(kr, cos_ref[...], sin_ref[...]) if rope else kr
    _kv_expand(ckv_n, kr_att, ukv_ref, kng_ref, k_ref, v_ref)


def _mla_proj(proj_m, p, rope_tabs, *, n_rows, row_off, rope):
    t0 = row_off // TM
    full = lambda shape: pl.BlockSpec(shape, lambda t: tuple(0 for _ in shape))
    in_specs = [
        pl.BlockSpec((TM, MLA_PAD), lambda t: (t0 + t, 0)),
        full((1, Q_LORA)), full((Q_LORA, 2 * H_B * LANES)), full((1, LANES)), full((1, LANES)),
        full((1, KV_LORA)), full((KV_LORA, 2 * H_B * NOPE)), full((1, LANES)), full((1, LANES)),
    ]
    args = [proj_m, p["cq_g"], p["uq_w"], p["qn_g"], p["qr_g"], p["ckv_g"], p["ukv_w"], p["kn_g"], p["kr_g"]]
    if rope:
        per_seq = DEC_SEQ // TM
        in_specs += [pl.BlockSpec((TM, LANES), lambda t: (t % per_seq, 0))] * 2
        args += list(rope_tabs)
    rows = lambda w: pl.BlockSpec((TM, w), lambda t: (t, 0))
    return pl.pallas_call(
        functools.partial(_mla_proj_kernel, rope=rope),
        grid=(n_rows // TM,),
        in_specs=in_specs,
        out_specs=[rows(2 * H_B * LANES), rows(2 * H_B * LANES), rows(H_B * V_DIM), rows(KV_LORA), rows(ROPE_DIM)],
        out_shape=[
            jax.ShapeDtypeStruct((n_rows, 2 * H_B * LANES), BF16),
            jax.ShapeDtypeStruct((n_rows, 2 * H_B * LANES), BF16),
            jax.ShapeDtypeStruct((n_rows, H_B * V_DIM), BF16),
            jax.ShapeDtypeStruct((n_rows, KV_LORA), F32),
            jax.ShapeDtypeStruct((n_rows, ROPE_DIM), F32),
        ],
        compiler_params=pltpu.CompilerParams(
            dimension_semantics=("arbitrary",), vmem_limit_bytes=VMEM_LIMIT),
        name="mla_proj_lat" if rope else "mla_proj_ctx",
    )(*args)


def _mla_cache_kernel(ckv_ref, kr_ref, ukv_ref, kng_ref, k_ref, v_ref):
    _kv_expand(ckv_ref[...], kr_ref[...], ukv_ref, kng_ref, k_ref, v_ref)


def _mla_cache_expand(ckv, kr_slab, p):
    n = ckv.shape[0]
    full = lambda shape: pl.BlockSpec(shape, lambda t: tuple(0 for _ in shape))
    return pl.pallas_call(
        _mla_cache_kernel,
        grid=(1,),
        in_specs=[full((n, KV_LORA)), full((n, LANES)), full((KV_LORA, 2 * H_B * NOPE)), full((1, LANES))],
        out_specs=[full((n, 2 * H_B * LANES)), full((n, H_B * V_DIM))],
        out_shape=[
            jax.ShapeDtypeStruct((n, 2 * H_B * LANES), BF16),
            jax.ShapeDtypeStruct((n, H_B * V_DIM), BF16),
        ],
        compiler_params=pltpu.CompilerParams(
            dimension_semantics=("arbitrary",), vmem_limit_bytes=VMEM_LIMIT),
        name="mla_cache_expand",
    )(ckv, kr_slab, p["ukv_w"], p["kn_g"])


def _attn_kernel(q_ref, k_ref, v_ref, *rest, with_ctx):
    if with_ctx:
        k2_ref, v2_ref, o_ref = rest
    else:
        (o_ref,) = rest
    hs = [slice(2 * h * LANES, (2 * h + 2) * LANES) for h in range(H_B)]
    vs = [slice(h * V_DIM, (h + 1) * V_DIM) for h in range(H_B)]
    k_refs = (k_ref, k2_ref) if with_ctx else (k_ref,)
    v_refs = (v_ref, v2_ref) if with_ctx else (v_ref,)
    scores = [[_dot(q_ref[:, hs[h]], kr[:, hs[h]], _NT) for kr in k_refs] for h in range(H_B)]
    probs, dens = [], []
    for h in range(H_B):
        m = functools.reduce(jnp.maximum, [jnp.max(s, axis=-1, keepdims=True) for s in scores[h]])
        pr = [jnp.exp(s - m) for s in scores[h]]
        dens.append(functools.reduce(lambda a, b: a + b, [jnp.sum(p, axis=-1, keepdims=True) for p in pr]))
        probs.append([p.astype(BF16) for p in pr])
    for h in range(H_B):
        o = functools.reduce(lambda a, b: a + b, [_dot(p, vr[:, vs[h]]) for p, vr in zip(probs[h], v_refs)])
        o_ref[:, vs[h]] = (o / dens[h]).astype(o_ref.dtype)


def _attention(q, k, v, k2=None, v2=None, *, n_batch, seq_len):
    with_ctx = k2 is not None
    q_tiles = seq_len // TM
    kw = 2 * H_B * LANES
    in_specs = [
        pl.BlockSpec((TM, kw), lambda b, t: (b * q_tiles + t, 0)),
        pl.BlockSpec((seq_len, kw), lambda b, t: (b, 0)),
        pl.BlockSpec((seq_len, H_B * V_DIM), lambda b, t: (b, 0)),
    ]
    args = [q, k, v]
    if with_ctx:
        in_specs += [
            pl.BlockSpec((PAST_LEN, kw), lambda b, t: (b, 0)),
            pl.BlockSpec((PAST_LEN, H_B * V_DIM), lambda b, t: (b, 0)),
        ]
        args += [k2, v2]
    return pl.pallas_call(
        functools.partial(_attn_kernel, with_ctx=with_ctx),
        grid=(n_batch, q_tiles),
        in_specs=in_specs,
        out_specs=pl.BlockSpec((TM, H_B * V_DIM), lambda b, t: (b * q_tiles + t, 0)),
        out_shape=jax.ShapeDtypeStruct((n_batch * seq_len, H_B * V_DIM), BF16),
        compiler_params=pltpu.CompilerParams(
            dimension_semantics=("arbitrary", "arbitrary"), vmem_limit_bytes=VMEM_LIMIT),
        name="attn_lat" if with_ctx else "attn_ctx",
    )(*args)


def _pool_bands():
    i = jnp.arange(TM)[:, None]
    e = jnp.arange(POOL_K)[None, :]
    e = jnp.where(e < TM, e, jnp.where(e < TM + HALO, e - TM - HALO, e - HALO))
    return jnp.stack([(e >= i - w // 2) & (e < i + w // 2) for w in POOL_WINDOWS]).astype(BF16)


def _pool_diffs(h, h_prev, h_next, pos0, seq_len, band_ref):
    ext = jnp.concatenate([h, h_prev, h_next, jnp.zeros((POOL_K - TM - 2 * HALO, D_MODEL), F32)], axis=0)
    pos = pos0 + lax.broadcasted_iota(jnp.int32, (TM, 1), 0)
    sums = [_dot_exact_lhs(band_ref[gi], ext[:, gi * GP:(gi + 1) * GP]) for gi in range(len(POOL_WINDOWS))]
    diffs = []
    for gi, win in enumerate(POOL_WINDOWS):
        half = win // 2
        cnt = (jnp.minimum(pos + half, seq_len) - jnp.maximum(pos - half, 0)).astype(F32)
        diffs.append((sums[gi] / cnt - h[:, gi * GP:(gi + 1) * GP]).astype(BF16))
    return diffs


def _tail_kernel(*refs, mixer, n_x, row_off):
    x_refs, rest = refs[:n_x], refs[n_x:]
    if mixer == "proj":
        a0_ref, a1_ref, b0_ref, b1_ref, ow_ref, mod_ref, g2_ref, w1_ref, w3_ref, w2_ref, o_ref = rest
    else:
        xp_ref, xn_ref, g1_ref, band_ref, pw_ref, ps_ref, mod_ref, g2_ref, w1_ref, w3_ref, w2_ref, o_ref = rest
    m = mod_ref[0, 0]
    x = _row_tile(x_refs, TM_FFN)
    if mixer == "proj":
        mix = (_dot(_row_tile((a0_ref, a1_ref), TM_FFN), ow_ref[:DA, :])
               + _dot(_row_tile((b0_ref, b1_ref), TM_FFN), ow_ref[DA:, :]))
    else:
        row0 = row_off + pl.program_id(0) * TM_FFN
        is_lat = row0 >= N_CTX
        seq_len = jnp.where(is_lat, DEC_SEQ, SEQ)
        norm_mod = lambda v: _rms(v, g1_ref[...]) * (1.0 + m[1:2]) + m[0:1]
        h, hp, hn = norm_mod(x), norm_mod(xp_ref[...]), norm_mod(xn_ref[...])
        parts = []
        n_sub = TM_FFN // TM
        for k in range(n_sub):
            r0 = row0 + k * TM
            pos0 = jnp.where(is_lat, (r0 - N_CTX) % DEC_SEQ, r0 % SEQ)
            prev = hp if k == 0 else h[k * TM - HALO:k * TM]
            nxt = hn if k == n_sub - 1 else h[(k + 1) * TM:(k + 1) * TM + HALO]
            prev = jnp.where(pos0 > 0, prev, 0.0)
            nxt = jnp.where(pos0 + TM < seq_len, nxt, 0.0)
            parts.append(_pool_diffs(h[k * TM:(k + 1) * TM], prev, nxt, pos0, seq_len, band_ref))
        mix = jnp.concatenate(
            [_dot(jnp.concatenate([p[gi] for p in parts], axis=0), pw_ref[gi]) for gi in range(len(POOL_WINDOWS))],
            axis=1) * ps_ref[...]
    x1 = x + m[2:3] * mix
    h2 = (_rms(x1, g2_ref[...]) * (1.0 + m[4:5]) + m[3:4]).astype(BF16)
    ups = [(_dot(h2, w1_ref[0, :, lo:hi]), _dot(h2, w3_ref[0, :, lo:hi])) for lo, hi in FF_SPLITS]
    acts = [(_silu(a) * b).astype(BF16) for a, b in ups]
    acc = functools.reduce(lambda p, q: p + q,
                           [_dot(u, w2_ref[0, lo:hi, :]) for u, (lo, hi) in zip(acts, FF_SPLITS)])
    o_ref[...] = x1 + m[5:6] * acc


def _layer_tail(x, mods, l, g2, w1, w3, w2, *, proj=None, pool=None, row_off=0, n_rows=N_TOK):
    per_tile = TM_FFN // TM
    t0 = row_off // TM_FFN
    const = lambda shape: pl.BlockSpec(shape, lambda i: tuple(0 for _ in shape))
    resident = lambda shape: pl.BlockSpec((1,) + shape, lambda i: (l, 0, 0), pipeline_mode=pl.Buffered(1))
    if proj is not None:
        assert row_off == 0 and n_rows == N_TOK
        a_out, b_out, out_w = proj
        x_specs, x_args = _row_specs(x, D_MODEL, TM_FFN)
        a_specs, a_args = _row_specs(a_out, DA, TM_FFN)
        b_specs, b_args = _row_specs(b_out, DA, TM_FFN)
        mix_specs = a_specs + b_specs + [const((D_MODEL, D_MODEL))]
        mix_args = a_args + b_args + [out_w]
        mixer = "proj"
    else:
        g1, pw, ps = pool
        r = TM_FFN // HALO
        last = N_TOK // HALO - 1
        x_specs = [pl.BlockSpec((TM_FFN, D_MODEL), lambda i: (t0 + i, 0))]
        x_args = [x]
        mix_specs = [
            pl.BlockSpec((HALO, D_MODEL), lambda i: (jnp.maximum((t0 + i) * r - 1, 0), 0)),
            pl.BlockSpec((HALO, D_MODEL), lambda i: (jnp.minimum((t0 + i + 1) * r, last), 0)),
            const((1, D_MODEL)), const((len(POOL_WINDOWS), TM, POOL_K)),
            const((len(POOL_WINDOWS), GP, GP)), const((1, D_MODEL)),
        ]
        mix_args = [x, x, g1, _pool_bands(), pw, ps]
        mixer = "pool"
    return pl.pallas_call(
        functools.partial(_tail_kernel, mixer=mixer, n_x=len(x_args), row_off=row_off),
        grid=(n_rows // TM_FFN,),
        in_specs=x_specs + mix_specs + [
            pl.BlockSpec((1, 1, N_MOD, D_MODEL), lambda i: (l, _cond_of_tile((t0 + i) * per_tile), 0, 0)),
            const((1, D_MODEL)),
            resident((D_MODEL, D_FF)), resident((D_MODEL, D_FF)), resident((D_FF, D_MODEL)),
        ],
        out_specs=pl.BlockSpec((TM_FFN, D_MODEL), lambda i: (i, 0)),
        out_shape=jax.ShapeDtypeStruct((n_rows, D_MODEL), F32),
        compiler_params=pltpu.CompilerParams(
            dimension_semantics=("arbitrary",), vmem_limit_bytes=VMEM_LIMIT),
        name="tail_" + mixer,
    )(*x_args, *mix_args, mods, g2, w1, w3, w2)


def _rope_tables():
    pos = jnp.arange(DEC_SEQ)
    freqs = ROPE_BASE ** (-jnp.arange(ROPE_AXIS_FREQS, dtype=F32) / ROPE_AXIS_FREQS)
    ang_row = (pos // GRID_W).astype(F32)[:, None] * freqs
    ang_col = (pos % GRID_W).astype(F32)[:, None] * freqs
    zeros = jnp.zeros((DEC_SEQ, LANES - ROPE_DIM), F32)
    cos = jnp.concatenate([jnp.cos(ang_row)] * 2 + [jnp.cos(ang_col)] * 2 + [zeros], axis=1)
    sin = jnp.concatenate([-jnp.sin(ang_row), jnp.sin(ang_row), -jnp.sin(ang_col), jnp.sin(ang_col), zeros], axis=1)
    return cos, sin


def _slab(v):
    return jnp.pad(v, (0, LANES - v.shape[0]))[None]


def _even_params(i, in_w, out_w, shift_mu, w0, w_up, a0, a_up, g_up, k_k, k_a, r_k, ln_g, ln_b,
                 cq_g, uq_w, ckv_g, ukv_w, qn_g, qr_g, kn_g, kr_g):
    row = lambda v: v[None]
    in_w_pad = jnp.pad(in_w[i], ((0, 0), (0, MLA_PAD - MLA_IN))).astype(BF16)
    mu = shift_mu[i]
    w_up_pad = jnp.pad(w_up[i], ((0, 0), (0, A_LORA), (0, 0)))
    a_up_pad = jnp.pad(a_up[i], ((0, 0), (W_LORA, 0), (0, 0)))
    uq = uq_w[i].reshape(Q_LORA, H_B, NOPE + ROPE_DIM)
    uq = jnp.pad(uq, ((0, 0), (0, 0), (0, 2 * LANES - NOPE - ROPE_DIM))).reshape(Q_LORA, 2 * H_B * LANES)
    ukv = ukv_w[i].reshape(KV_LORA, H_B, NOPE + V_DIM)
    ukv = jnp.concatenate([ukv[:, :, :NOPE].reshape(KV_LORA, -1), ukv[:, :, NOPE:].reshape(KV_LORA, -1)], axis=1)
    rw = dict(mu_r=row(mu[:DA]), mu_k=row(mu[DA:2 * DA]), mu_v=row(mu[2 * DA:3 * DA]), mu_l=row(mu[3 * DA:]),
              w0=w0[i], w_up=w_up_pad, a0=a0[i], a_up=a_up_pad, g_up=g_up[i],
              k_k=row(k_k[i]), k_a=row(k_a[i]), r_k=row(r_k[i].reshape(DA)), ln_g=row(ln_g[i]), ln_b=row(ln_b[i]))
    mla = dict(cq_g=row(cq_g[i]), uq_w=uq.astype(BF16), qn_g=row(qn_g[i]), qr_g=_slab(qr_g[i]),
               ckv_g=row(ckv_g[i]), ukv_w=ukv.astype(BF16), kn_g=row(kn_g[i]), kr_g=_slab(kr_g[i]))
    return in_w_pad, out_w[i].astype(BF16), rw, mla


def kernel(x_prompt, x_sample, c, state_rwkv, cache_mla_ckv, cache_mla_krope, c_ctx,
           norm1_g, norm2_g, ada_w, ada_b, ffn_w1, ffn_w3, ffn_w2,
           in_w, out_w, shift_mu, rwkv_w0, rwkv_w_up, rwkv_a0, rwkv_a_up, rwkv_g_up,
           rwkv_k_k, rwkv_k_a, rwkv_r_k, rwkv_ln_g, rwkv_ln_b,
           mla_cq_g, mla_uq_w, mla_ckv_g, mla_ukv_w, mla_qn_g, mla_qr_g, mla_kn_g, mla_kr_g,
           pool_w, pool_scale):
    x = (x_prompt.reshape(N_CTX, D_MODEL), x_sample.reshape(N_LAT, D_MODEL))
    conds = jnp.concatenate([c_ctx[None], c, jnp.zeros((8 - 1 - DEC_BATCH, D_MODEL), F32)], axis=0)
    mods = _adaln(conds, ada_w, ada_b).reshape(DEPTH, 8, N_MOD, D_MODEL)
    rope_tabs = _rope_tables()
    w1_all, w3_all, w2_all = ffn_w1.astype(BF16), ffn_w3.astype(BF16), ffn_w2.astype(BF16)

    new_s, new_ckv, new_kr = [], [], []
    for l in range(DEPTH):
        n1 = norm1_g[l][None]
        if l % 2 == 0:
            i = l // 2
            in_w_pad, out_w_b, rw, mla = _even_params(
                i, in_w, out_w, shift_mu, rwkv_w0, rwkv_w_up, rwkv_a0, rwkv_a_up, rwkv_g_up,
                rwkv_k_k, rwkv_k_a, rwkv_r_k, rwkv_ln_g, rwkv_ln_b,
                mla_cq_g, mla_uq_w, mla_ckv_g, mla_ukv_w, mla_qn_g, mla_qr_g, mla_kn_g, mla_kr_g)
            proj_r, proj_m = _inproj(x, mods, l, n1, in_w_pad)
            a_ctx, s_ctx = _rwkv(proj_r, rw, None, i, seq_len=SEQ, n_seq=BATCH, row_off=0)
            a_lat, _ = _rwkv(proj_r, rw, state_rwkv, i, seq_len=DEC_SEQ, n_seq=DEC_BATCH, row_off=N_CTX)
            q_c, k_c, v_c, ckv_c, kr_c = _mla_proj(proj_m, mla, None, n_rows=N_CTX, row_off=0, rope=False)
            q_l, k_l, v_l, _, _ = _mla_proj(proj_m, mla, rope_tabs, n_rows=N_LAT, row_off=N_CTX, rope=True)
            kr_cache = jnp.pad(cache_mla_krope[:, i].reshape(DEC_BATCH * PAST_LEN, ROPE_DIM),
                               ((0, 0), (0, LANES - ROPE_DIM)))
            k_p, v_p = _mla_cache_expand(cache_mla_ckv[:, i].reshape(DEC_BATCH * PAST_LEN, KV_LORA), kr_cache, mla)
            b_ctx = _attention(q_c, k_c, v_c, n_batch=BATCH, seq_len=SEQ)
            b_lat = _attention(q_l, k_l, v_l, k_p, v_p, n_batch=DEC_BATCH, seq_len=DEC_SEQ)
            mixer = dict(proj=((a_ctx, a_lat), (b_ctx, b_lat), out_w_b))
            new_s.append(s_ctx)
            new_ckv.append(ckv_c.reshape(BATCH, SEQ, KV_LORA))
            new_kr.append(kr_c.reshape(BATCH, SEQ, ROPE_DIM))
        else:
            j = l // 2
            mixer = dict(pool=(n1, pool_w[j].astype(BF16), pool_scale[j][None]))
        tail_args = (mods, l, norm2_g[l][None], w1_all, w3_all, w2_all)
        if l < DEPTH - 1:
            x = _layer_tail(x, *tail_args, **mixer)
        else:
            y_p = _layer_tail(x, *tail_args, **mixer, row_off=0, n_rows=N_CTX).reshape(BATCH, SEQ, D_MODEL)
            y_s = _layer_tail(x, *tail_args, **mixer, row_off=N_CTX, n_rows=N_LAT).reshape(DEC_BATCH, DEC_SEQ, D_MODEL)
    return (y_p, y_s, jnp.stack(new_s, axis=1), jnp.stack(new_ckv, axis=1), jnp.stack(new_kr, axis=1))
```

```python
import functools

import jax
import jax.numpy as jnp
from jax import lax
from jax.experimental import pallas as pl
from jax.experimental.pallas import tpu as pltpu

D_MODEL = 1024
BATCH, SEQ = 32, 256
DEC_BATCH, DEC_SEQ = 2, 2048
DEPTH = 4
PAST_LEN = 256
GRID_W = 64
N_MOD = 6
D_FF = 2816
DA = 512
HD_A = 64
H_A = 8
W_LORA, A_LORA, G_LORA = 64, 64, 128
H_B = 4
NOPE, ROPE_DIM, V_DIM = 128, 64, 128
Q_LORA, KV_LORA = 384, 256
ATTN_SCALE = (NOPE + ROPE_DIM) ** -0.5
ROPE_AXIS_FREQS = ROPE_DIM // 4
ROPE_BASE = 10000.0
RWKV_IN = 3 * DA + W_LORA + A_LORA + G_LORA
MLA_IN = Q_LORA + KV_LORA + ROPE_DIM
MLA_PAD = 768
POOL_WINDOWS = (2, 4, 8, 16)
GP = 256
EPS = 1e-6
GN_EPS = 64e-5

N_CTX = BATCH * SEQ
N_LAT = DEC_BATCH * DEC_SEQ
N_TOK = N_CTX + N_LAT

LANES = 128
TM = 256
N_TILES = N_TOK // TM
CTX_TILES = N_CTX // TM
LAT_TILES_PER_SEQ = DEC_SEQ // TM
CHUNK = 64
CHUNKS_PER_TRIP = 8
ATTN_SEQS_PER_STEP = 4
HALO = 8
POOL_K = 384
TM_FFN = 512
MXU_WIDTH = 256
FF_SPLITS = ((0, 6 * MXU_WIDTH), (6 * MXU_WIDTH, D_FF))
VMEM_LIMIT = 56 * 1024 * 1024

F32 = jnp.float32
BF16 = jnp.bfloat16


def _cond_of_tile(t):
    return jnp.where(t < CTX_TILES, 0, 1 + (t - CTX_TILES) // LAT_TILES_PER_SEQ)


def _dot(a, b, dims=((1,), (0,))):
    return lax.dot_general(a, b, (dims, ((), ())), preferred_element_type=F32)


_NT = ((1,), (1,))


def _split2(x):
    hi = x.astype(BF16)
    lo = (x - hi.astype(F32)).astype(BF16)
    return hi, lo


def _dot1(a, b, dims=((1,), (0,))):
    return _dot(a.astype(BF16), b.astype(BF16), dims)


def _dot3(a, b, dims=((1,), (0,))):
    ah, al = _split2(a)
    bh, bl = _split2(b)
    return _dot(ah, bh, dims) + (_dot(ah, bl, dims) + _dot(al, bh, dims))


def _dot_exact_lhs(a_bf16, b):
    b0, b1 = _split2(b)
    return _dot(a_bf16, b0) + _dot(a_bf16, b1)


def _rms(x, g):
    return x * lax.rsqrt(jnp.mean(x * x, axis=-1, keepdims=True) + EPS) * g


def _sigmoid(x):
    return 1.0 / (1.0 + jnp.exp(-x))


def _silu(x):
    return x * _sigmoid(x)


def _softplus(x):
    return jnp.maximum(x, 0.0) + jnp.log(1.0 + jnp.exp(-jnp.abs(x)))


def _group_ones(n, g):
    i = lax.broadcasted_iota(jnp.int32, (n, n), 0) // g
    j = lax.broadcasted_iota(jnp.int32, (n, n), 1) // g
    return (i == j).astype(BF16)


def _gsum(x, ones):
    hi, lo = _split2(x)
    return _dot(hi, ones) + _dot(lo, ones)


def _adaln_kernel(c_ref, w_ref, b_ref, o_ref):
    o_ref[0] = _dot3(_silu(c_ref[...]), w_ref[0]) + b_ref[0]


def _adaln(conds, ada_w, ada_b):
    tn = 1536
    return pl.pallas_call(
        _adaln_kernel,
        grid=(DEPTH, N_MOD * D_MODEL // tn),
        in_specs=[
            pl.BlockSpec((8, D_MODEL), lambda l, j: (0, 0)),
            pl.BlockSpec((1, D_MODEL, tn), lambda l, j: (l, 0, j)),
            pl.BlockSpec((1, 1, tn), lambda l, j: (l, 0, j)),
        ],
        out_specs=pl.BlockSpec((1, 8, tn), lambda l, j: (l, 0, j)),
        out_shape=jax.ShapeDtypeStruct((DEPTH, 8, N_MOD * D_MODEL), F32),
        compiler_params=pltpu.CompilerParams(
            dimension_semantics=("arbitrary", "arbitrary"), vmem_limit_bytes=VMEM_LIMIT),
        name="adaln",
    )(conds, ada_w, ada_b.reshape(DEPTH, 1, N_MOD * D_MODEL))


def _mod_spec(l):
    return pl.BlockSpec((1, 1, N_MOD, D_MODEL), lambda t, *_: (l, _cond_of_tile(t), 0, 0))


def _row_specs(x, width, tm=TM):
    if not isinstance(x, tuple):
        return [pl.BlockSpec((tm, width), lambda t, *_: (t, 0))], [x]
    ctx_tiles = N_CTX // tm
    return [pl.BlockSpec((tm, width), lambda t, *_: (jnp.minimum(t, ctx_tiles - 1), 0)),
            pl.BlockSpec((tm, width), lambda t, *_: (jnp.maximum(t - ctx_tiles, 0), 0))], list(x)


def _row_tile(refs, tm=TM):
    if len(refs) == 1:
        return refs[0][...]
    return jnp.where(pl.program_id(0) < N_CTX // tm, refs[0][...], refs[1][...])


def _inproj_kernel(*refs, n_x):
    x_refs, (mod_ref, g_ref, w_ref, *mla_refs) = refs[:n_x], refs[n_x:]
    *mla_in, or_ref, q_ref, k_ref, v_ref, ckv_ref, kr_ref = mla_refs
    m = mod_ref[0, 0]
    h = _rms(_row_tile(x_refs), g_ref[...]) * (1.0 + m[1:2]) + m[0:1]
    p = _dot(h.astype(BF16), w_ref[...])
    or_ref[...] = p[:, :RWKV_IN]
    _mla_project(p[:, RWKV_IN:], pl.program_id(0) >= CTX_TILES, *mla_in, q_ref, k_ref, v_ref, ckv_ref, kr_ref)


def _inproj(x, mods, l, g, w, p, rope_tabs):
    x_specs, x_args = _row_specs(x, D_MODEL)
    full = lambda shape: pl.BlockSpec(shape, lambda t: tuple(0 for _ in shape))
    rope_spec = pl.BlockSpec((TM, LANES), lambda t: (jnp.maximum(t - CTX_TILES, 0) % LAT_TILES_PER_SEQ, 0))
    rows = lambda w_: pl.BlockSpec((TM, w_), lambda t: (t, 0))
    return pl.pallas_call(
        functools.partial(_inproj_kernel, n_x=len(x_args)),
        grid=(N_TILES,),
        in_specs=x_specs + [
            _mod_spec(l),
            full((1, D_MODEL)),
            full((D_MODEL, RWKV_IN + MLA_PAD)),
            full((1, Q_LORA)), full((Q_LORA, 2 * H_B * LANES)), full((1, LANES)), full((1, LANES)),
            full((1, KV_LORA)), full((KV_LORA, 2 * H_B * NOPE)), full((1, LANES)), full((1, LANES)),
            rope_spec, rope_spec,
        ],
        out_specs=[rows(RWKV_IN), rows(2 * H_B * LANES), rows(2 * H_B * LANES), rows(H_B * V_DIM),
                   rows(KV_LORA), rows(ROPE_DIM)],
        out_shape=[
            jax.ShapeDtypeStruct((N_TOK, RWKV_IN), F32),
            jax.ShapeDtypeStruct((N_TOK, 2 * H_B * LANES), BF16),
            jax.ShapeDtypeStruct((N_TOK, 2 * H_B * LANES), BF16),
            jax.ShapeDtypeStruct((N_TOK, H_B * V_DIM), BF16),
            jax.ShapeDtypeStruct((N_TOK, KV_LORA), F32),
            jax.ShapeDtypeStruct((N_TOK, ROPE_DIM), F32),
        ],
        compiler_params=pltpu.CompilerParams(
            dimension_semantics=("arbitrary",), vmem_limit_bytes=VMEM_LIMIT),
        name="inproj",
    )(*x_args, mods, g, w, p["cq_g"], p["uq_w"], p["qn_g"], p["qr_g"], p["ckv_g"], p["ukv_w"], p["kn_g"],
      p["kr_g"], *rope_tabs)


def _rwkv_kernel(pr_ref, pk_ref, pv_ref, pl_ref, mur_ref, muk_ref, muv_ref, mul_ref,
                 w0_ref, wup_ref, a0_ref, aup_ref, gup_ref, kkg_ref, kag_ref, rkg_ref,
                 lng_ref, lnb_ref, *rest, seq_len, n_seqs, trip, zero_init):
    if zero_init:
        h0_ref = None
        out_ref, hfin_ref, m_s, n_s, yq_s, y0_s, g_s, bonus_s = rest
    else:
        h0_ref, out_ref, hfin_ref, m_s, n_s, yq_s, y0_s, g_s, bonus_s = rest
    C = CHUNK
    n_chunks = seq_len // C
    all_chunks = n_seqs * n_chunks
    static_trip = all_chunks == trip
    lane = lax.broadcasted_iota(jnp.int32, (1, LANES), 1)
    m0 = (lane < HD_A).astype(F32)
    m1 = 1.0 - m0
    ones_g = _group_ones(LANES, HD_A)
    row = lax.broadcasted_iota(jnp.int32, (C, 1), 0)
    i2 = lax.broadcasted_iota(jnp.int32, (2 * C, 2 * C), 0)
    j2 = lax.broadcasted_iota(jnp.int32, (2 * C, 2 * C), 1)
    same_head = (i2 // C) == (j2 // C)
    eye2 = (i2 == j2).astype(F32)
    ic = lax.broadcasted_iota(jnp.int32, (C, C), 0)
    jc = lax.broadcasted_iota(jnp.int32, (C, C), 1)
    incl_c = ((jc <= ic).astype(BF16), (jc >= ic).astype(BF16))
    incl_blk = (same_head & (j2 <= i2), same_head & (j2 >= i2))
    strict_blk = (same_head & (j2 < i2), same_head & (j2 > i2))

    def pair_masks(d):
        masks = []
        s = 1
        while s < C:
            later, earlier = ((i2 // s) % 2, (j2 // s) % 2) if d == 0 else ((j2 // s) % 2, (i2 // s) % 2)
            masks.append(((i2 // (2 * s)) == (j2 // (2 * s))) & (later == 1) & (earlier == 0))
            s *= 2
        return masks

    pair_blk = (pair_masks(0), pair_masks(1))

    def stack(x):
        return jnp.concatenate([x * m0, x * m1], axis=0)

    def chunk_rows(c):
        return pl.ds(c * C, C) if isinstance(c, int) else pl.ds(pl.multiple_of(c * C, C), C)

    def shifted(ref, mu, c):
        x = ref[chunk_rows(c), :]
        if isinstance(c, int):
            first, last = c % n_chunks == 0, c % n_chunks == n_chunks - 1
            prev_row = jnp.zeros((1, x.shape[1]), F32) if first else ref[c * C - 1:c * C, :]
            next_row = jnp.zeros((1, x.shape[1]), F32) if last else ref[(c + 1) * C:(c + 1) * C + 1, :]
        else:
            assert n_seqs == 1
            pstart = pl.multiple_of(jnp.maximum(c * C - 8, 0), 8)
            nstart = pl.multiple_of(jnp.minimum((c + 1) * C, seq_len - 8), 8)
            prev_row = jnp.where(c > 0, ref[pl.ds(pstart, 8), :][7:8], 0.0)
            next_row = jnp.where(c < n_chunks - 1, ref[pl.ds(nstart, 8), :][0:1], 0.0)
        prev = jnp.where(row == 0, prev_row, pltpu.roll(x, 1, 0))
        nxt = jnp.where(row == C - 1, next_row, pltpu.roll(x, C - 1, 0))
        return x + mu * (0.5 * (prev + nxt) - x)

    def chain_mats(chains):
        ds = [ch[0] for ch in chains]
        lp = [_dot_exact_lhs(incl_c[d], logw) for d, *_, logw in chains]
        lpc = [x[C - 1:C] if d == 0 else x[0:1] for d, x in zip(ds, lp)]
        st = []
        for (d, r, kd, v, kk, b, logw), x, xc in zip(chains, lp, lpc):
            p_in, p_ex, p_inv, p_end = jnp.exp(x), jnp.exp(x - logw), jnp.exp(-x), jnp.exp(xc - x)
            st.append(dict(qs=stack(r * p_in), a_s=stack(kk * p_ex), khs=stack(kd * p_inv),
                           bs=stack(b * p_inv), kbs=stack(kd * p_end), bbs=stack(b * p_end), vs=stack(v)))
        R = 2 * C
        quad = [_dot1(jnp.concatenate([s["a_s"], s["qs"]], axis=0),
                      jnp.concatenate([s["bs"], s["khs"]], axis=0), _NT) for s in st]
        lab = [jnp.where(strict_blk[d], q[:R, :R], 0.0) for d, q in zip(ds, quad)]
        lak = [jnp.where(strict_blk[d], q[:R, R:], 0.0) for d, q in zip(ds, quad)]
        dqb = [jnp.where(incl_blk[d], q[R:, :R], 0.0) for d, q in zip(ds, quad)]
        dqk = [jnp.where(incl_blk[d], q[R:, R:], 0.0) for d, q in zip(ds, quad)]
        x2 = [_dot1(l, s["vs"]) for l, s in zip(lak, st)]
        tinv = [eye2 - jnp.where(pair_blk[d][0], l, 0.0) for d, l in zip(ds, lab)]
        for lvl in range(1, len(pair_blk[0])):
            half = [_dot1(t, jnp.where(pair_blk[d][lvl], l, 0.0)) for d, t, l in zip(ds, tinv, lab)]
            tinv = [t - _dot1(h, t) for h, t in zip(half, tinv)]
        w12 = [_dot1(t, jnp.concatenate([s["a_s"], x], axis=1)) for t, s, x in zip(tinv, st, x2)]
        zero_blk = jnp.zeros((R, R), F32)
        fin = [_dot1(jnp.concatenate([jnp.concatenate([s["bbs"].T, -s["kbs"].T], axis=1),
                                      jnp.concatenate([qb, -qk], axis=1)], axis=0),
                     jnp.concatenate([w, jnp.concatenate([zero_blk, s["vs"]], axis=1)], axis=0))
               for s, qb, qk, w in zip(st, dqb, dqk, w12)]
        return [(eye2 * jnp.exp(xc) - f[:R, :R], -f[:R, R:], s["qs"] - f[R:, :R], -f[R:, R:])
                for xc, f, s in zip(lpc, fin, st)]

    def load_chunk(c):
        return (shifted(pr_ref, mur_ref[...], c), shifted(pk_ref, muk_ref[...], c),
                shifted(pv_ref, muv_ref[...], c), shifted(pl_ref, mul_ref[...], c))

    def compute_chunks(loaded):
        rows_of = lambda j: slice(j * C, (j + 1) * C)
        r_all, k_all, v_all, lo_all = (jnp.concatenate([ch[j] for ch in loaded], axis=0) for j in range(4))
        wa = lo_all[:, :LANES]
        gate_all = _dot1(_sigmoid(lo_all[:, LANES:]), gup_ref[...])
        kkr = k_all * kkg_ref[...]
        kk_all = kkr / (jnp.sqrt(_gsum(kkr * kkr, ones_g)) + 1e-12)
        bonus_all = _gsum(r_all * k_all * rkg_ref[...], ones_g) * v_all
        both = lambda ref: jnp.concatenate([ref[0], ref[1]], axis=1)
        both_rows = lambda ref: jnp.concatenate([ref[0:1, :], ref[1:2, :]], axis=1)
        wl = both_rows(w0_ref) + _dot3(jnp.tanh(wa), both(wup_ref))
        logw_all = -jnp.exp(-_softplus(-wl) - 0.5)
        a_all = _sigmoid(both_rows(a0_ref) + _dot3(wa, both(aup_ref)))
        chains = []
        for j, (r, k, v, _) in enumerate(loaded):
            kk = kk_all[rows_of(j)]
            for d in range(2):
                a = a_all[rows_of(j), d * LANES:(d + 1) * LANES]
                logw = logw_all[rows_of(j), d * LANES:(d + 1) * LANES]
                chains.append((d, r, k * (1.0 + (a - 1.0) * kag_ref[...]), v, kk, kk * a, logw))
        mats = chain_mats(chains)
        return [(gate_all[rows_of(j)], bonus_all[rows_of(j)], mats[2 * j:2 * j + 2]) for j in range(len(loaded))]

    def store_chunk(c, gate, bonus, mats):
        g_s[chunk_rows(c), :] = gate
        bonus_s[chunk_rows(c), :] = bonus
        for d in range(2):
            m_s[d, c], n_s[d, c], yq_s[d, c], y0_s[d, c] = mats[d]

    def phase_a_trip(i, carry):
        chunks = [i * trip + u for u in range(trip)]
        results = compute_chunks([load_chunk(c) for c in chunks])
        for c, res in zip(chunks, results):
            store_chunk(c, *res)
        return carry

    if static_trip:
        phase_a_trip(0, 0)
    else:
        lax.fori_loop(0, all_chunks // trip, phase_a_trip, 0)

    def phase_b(i, states):
        new_states = []
        for q in range(n_seqs):
            for d, c in ((0, q * n_chunks + i), (1, q * n_chunks + n_chunks - 1 - i)):
                h = states[2 * q + d]
                if zero_init and isinstance(i, int) and i == 0:
                    new_states.append(n_s[d, c])
                else:
                    new_states.append(_dot3(m_s[d, c], h) + n_s[d, c])
                m_s[d, c] = h
        return tuple(new_states)

    zeros_half = jnp.zeros((HD_A, HD_A), F32)
    states = []
    for q in range(n_seqs):
        for d in range(2):
            if zero_init:
                states.append(jnp.zeros((LANES, LANES), F32))
            else:
                top = jnp.concatenate([h0_ref[q, 0, d, 0], zeros_half], axis=1)
                bot = jnp.concatenate([zeros_half, h0_ref[q, 0, d, 1]], axis=1)
                states.append(jnp.concatenate([top, bot], axis=0).T)
    states = tuple(states)
    if static_trip:
        for i in range(n_chunks):
            states = phase_b(i, states)
    else:
        states = lax.fori_loop(0, n_chunks, phase_b, states)
    for q in range(n_seqs):
        for d in range(2):
            s_vk = states[2 * q + d].T
            hfin_ref[q, d, 0] = s_vk[:HD_A, :HD_A]
            hfin_ref[q, d, 1] = s_vk[HD_A:, HD_A:]

    def phase_c_trip(i, carry):
        chunks = [i * trip + u for u in range(trip)]
        y0s = [y0_s[0, c] + y0_s[1, c] for c in chunks]
        extras = [(bonus_s[chunk_rows(c), :], g_s[chunk_rows(c), :]) for c in chunks]
        yfs = [_dot1(yq_s[0, c], m_s[0, c]) for c in chunks]
        ybs = [_dot1(yq_s[1, c], m_s[1, c]) for c in chunks]
        ys = [(f + b + y0)[:C] + (f + b + y0)[C:] for f, b, y0 in zip(yfs, ybs, y0s)]
        y_all = jnp.concatenate(ys, axis=0)
        yc = y_all - _gsum(y_all, ones_g) * (1.0 / HD_A)
        var = _gsum(yc * yc, ones_g) * (1.0 / HD_A)
        yn = yc * lax.rsqrt(var + GN_EPS) * lng_ref[...] + lnb_ref[...]
        for j, (c, (bonus, gate)) in enumerate(zip(chunks, extras)):
            out_ref[chunk_rows(c), :] = ((yn[j * C:(j + 1) * C] + bonus) * gate).astype(out_ref.dtype)
        return carry

    if static_trip:
        phase_c_trip(0, 0)
    else:
        lax.fori_loop(0, all_chunks // trip, phase_c_trip, 0)


def _rwkv(proj_r, p, h0, layer, *, seq_len, n_seq, row_off):
    zero_init = h0 is None
    seqs = max(1, CHUNKS_PER_TRIP // (seq_len // CHUNK))
    blk_rows = seqs * seq_len
    n_chunks = blk_rows // CHUNK
    off = row_off // blk_rows
    n_pairs = DA // LANES
    col = lambda cb: (lambda s, hp: (off + s, cb + hp))
    vec = lambda: pl.BlockSpec((1, LANES), lambda s, hp: (0, hp))
    in_specs = [
        pl.BlockSpec((blk_rows, LANES), col(0)),
        pl.BlockSpec((blk_rows, LANES), col(n_pairs)),
        pl.BlockSpec((blk_rows, LANES), col(2 * n_pairs)),
        pl.BlockSpec((blk_rows, 2 * LANES), lambda s, hp: (off + s, 3 * n_pairs // 2)),
        vec(), vec(), vec(),
        pl.BlockSpec((1, 2 * LANES), lambda s, hp: (0, 0)),
        pl.BlockSpec((2, LANES), lambda s, hp: (0, hp)),
        pl.BlockSpec((2, LANES, LANES), lambda s, hp: (0, 0, hp)),
        pl.BlockSpec((2, LANES), lambda s, hp: (0, hp)),
        pl.BlockSpec((2, LANES, LANES), lambda s, hp: (0, 0, hp)),
        pl.BlockSpec((G_LORA, LANES), lambda s, hp: (0, hp)),
        vec(), vec(), vec(), vec(), vec(),
    ]
    args = [proj_r, proj_r, proj_r, proj_r, p["mu_r"], p["mu_k"], p["mu_v"], p["mu_l"],
            p["w0"], p["w_up"], p["a0"], p["a_up"], p["g_up"], p["k_k"], p["k_a"], p["r_k"],
            p["ln_g"], p["ln_b"]]
    if not zero_init:
        in_specs.append(pl.BlockSpec((seqs, 1, 2, 2, HD_A, HD_A), lambda s, hp: (s, layer, 0, hp, 0, 0)))
        args.append(h0)
    blk = (2 * CHUNK, LANES)
    return pl.pallas_call(
        functools.partial(_rwkv_kernel, seq_len=seq_len, n_seqs=seqs, trip=CHUNKS_PER_TRIP,
                          zero_init=zero_init),
        grid=(n_seq // seqs, n_pairs),
        in_specs=in_specs,
        out_specs=[
            pl.BlockSpec((blk_rows, LANES), lambda s, hp: (s, hp)),
            pl.BlockSpec((seqs, 2, 2, HD_A, HD_A), lambda s, hp: (s, 0, hp, 0, 0)),
        ],
        out_shape=[
            jax.ShapeDtypeStruct((n_seq * seq_len, DA), BF16),
            jax.ShapeDtypeStruct((n_seq, 2, H_A, HD_A, HD_A), F32),
        ],
        scratch_shapes=[
            pltpu.VMEM((2, n_chunks) + blk, F32),
            pltpu.VMEM((2, n_chunks) + blk, F32),
            pltpu.VMEM((2, n_chunks) + blk, F32),
            pltpu.VMEM((2, n_chunks) + blk, F32),
            pltpu.VMEM((blk_rows, LANES), F32),
            pltpu.VMEM((blk_rows, LANES), F32),
        ],
        compiler_params=pltpu.CompilerParams(
            dimension_semantics=("arbitrary", "arbitrary"), vmem_limit_bytes=VMEM_LIMIT),
        name="rwkv_ctx" if zero_init else "rwkv_lat",
    )(*args)


def _rope(x, cos, sin):
    lane = lax.broadcasted_iota(jnp.int32, x.shape, 1)
    swapped = jnp.where((lane % 32) < 16, pltpu.roll(x, LANES - 16, 1), pltpu.roll(x, 16, 1))
    return x * cos + swapped * sin


def _slab_rms(x, g):
    return x * lax.rsqrt(jnp.sum(x * x, axis=-1, keepdims=True) * (1.0 / ROPE_DIM) + EPS) * g


def _kv_expand(ckv_n, kr_att, ukv_ref, kng_ref, k_ref, v_ref):
    kv = _dot(ckv_n.astype(BF16), ukv_ref[...])
    kr_b = kr_att.astype(BF16)
    for h in range(H_B):
        kn = _rms(kv[:, h * NOPE:(h + 1) * NOPE], kng_ref[...])
        k_ref[:, 2 * h * LANES:(2 * h + 1) * LANES] = kn.astype(BF16)
        k_ref[:, (2 * h + 1) * LANES:(2 * h + 2) * LANES] = kr_b
    v_ref[...] = kv[:, H_B * NOPE:].astype(BF16)


def _mla_project(pm, is_lat, cqg_ref, uq_ref, qng_ref, qrg_ref, ckvg_ref, ukv_ref, kng_ref, krg_ref,
                 cos_ref, sin_ref, q_ref, k_ref, v_ref, ckv_ref, kr_ref):
    cq = _rms(pm[:, :Q_LORA], cqg_ref[...])
    ckv_n = _rms(pm[:, Q_LORA:Q_LORA + KV_LORA], ckvg_ref[...])
    q = _dot(cq.astype(BF16), uq_ref[...])
    ckv_ref[...] = ckv_n
    kr = _slab_rms(pm[:, Q_LORA + KV_LORA:], krg_ref[...])
    kr_ref[...] = kr[:, :ROPE_DIM]
    _kv_expand(ckv_n, kr, ukv_ref, kng_ref, k_ref, v_ref)
    qrs = []
    for h in range(H_B):
        qn = _rms(q[:, 2 * h * LANES:(2 * h + 1) * LANES], qng_ref[...])
        qrs.append(_slab_rms(q[:, (2 * h + 1) * LANES:(2 * h + 2) * LANES], qrg_ref[...]))
        q_ref[:, 2 * h * LANES:(2 * h + 1) * LANES] = (qn * ATTN_SCALE).astype(BF16)
        q_ref[:, (2 * h + 1) * LANES:(2 * h + 2) * LANES] = (qrs[h] * ATTN_SCALE).astype(BF16)

    @pl.when(is_lat)
    def _():
        cos, sin = cos_ref[...], sin_ref[...]
        kr_rot = _rope(kr, cos, sin).astype(BF16)
        for h in range(H_B):
            k_ref[:, (2 * h + 1) * LANES:(2 * h + 2) * LANES] = kr_rot
            q_ref[:, (2 * h + 1) * LANES:(2 * h + 2) * LANES] = (_rope(qrs[h], cos, sin) * ATTN_SCALE).astype(BF16)


def _mla_cache_kernel(ckv_ref, kr_ref, ukv_ref, kng_ref, k_ref, v_ref):
    _kv_expand(ckv_ref[...], kr_ref[...], ukv_ref, kng_ref, k_ref, v_ref)


def _mla_cache_expand(ckv, kr_slab, p):
    n = ckv.shape[0]
    full = lambda shape: pl.BlockSpec(shape, lambda t: tuple(0 for _ in shape))
    return pl.pallas_call(
        _mla_cache_kernel,
        grid=(1,),
        in_specs=[full((n, KV_LORA)), full((n, LANES)), full((KV_LORA, 2 * H_B * NOPE)), full((1, LANES))],
        out_specs=[full((n, 2 * H_B * LANES)), full((n, H_B * V_DIM))],
        out_shape=[
            jax.ShapeDtypeStruct((n, 2 * H_B * LANES), BF16),
            jax.ShapeDtypeStruct((n, H_B * V_DIM), BF16),
        ],
        compiler_params=pltpu.CompilerParams(
            dimension_semantics=("arbitrary",), vmem_limit_bytes=VMEM_LIMIT),
        name="mla_cache_expand",
    )(ckv, kr_slab, p["ukv_w"], p["kn_g"])


def _attn_kernel(q_ref, k_ref, v_ref, *rest, with_ctx, seq_len, n_seqs):
    if with_ctx:
        k2_ref, v2_ref, o_ref = rest
    else:
        (o_ref,) = rest
    hs = [slice(2 * h * LANES, (2 * h + 2) * LANES) for h in range(H_B)]
    vs = [slice(h * V_DIM, (h + 1) * V_DIM) for h in range(H_B)]
    units = [(b, h) for b in range(n_seqs) for h in range(H_B)]
    qrows = lambda b: slice(b * TM, (b + 1) * TM)
    krows = lambda b: slice(b * seq_len, (b + 1) * seq_len)

    def keys_values(b, h):
        kv = [(k_ref[krows(b), hs[h]], v_ref[krows(b), vs[h]])]
        if with_ctx:
            kv.append((k2_ref[:, hs[h]], v2_ref[:, vs[h]]))
        return kv

    scores = [[_dot(q_ref[qrows(b), hs[h]], k, _NT) for k, _ in keys_values(b, h)] for b, h in units]
    probs, dens = [], []
    for sc in scores:
        m = functools.reduce(jnp.maximum, [jnp.max(s, axis=-1, keepdims=True) for s in sc])
        pr = [jnp.exp(s - m) for s in sc]
        dens.append(functools.reduce(lambda a, b: a + b, [jnp.sum(p, axis=-1, keepdims=True) for p in pr]))
        probs.append([p.astype(BF16) for p in pr])
    for (b, h), pr, den in zip(units, probs, dens):
        o = functools.reduce(lambda a, c: a + c, [_dot(p, v) for p, (_, v) in zip(pr, keys_values(b, h))])
        o_ref[qrows(b), vs[h]] = (o / den).astype(o_ref.dtype)


def _attention(q, k, v, k2=None, v2=None, *, n_batch, seq_len, row_off):
    with_ctx = k2 is not None
    q_tiles = seq_len // TM
    n_seqs = ATTN_SEQS_PER_STEP if q_tiles == 1 else 1
    kw = 2 * H_B * LANES
    q0 = row_off // (n_seqs * TM)
    k0 = row_off // (n_seqs * seq_len)
    in_specs = [
        pl.BlockSpec((n_seqs * TM, kw), lambda b, t: (q0 + b * q_tiles + t, 0)),
        pl.BlockSpec((n_seqs * seq_len, kw), lambda b, t: (k0 + b, 0)),
        pl.BlockSpec((n_seqs * seq_len, H_B * V_DIM), lambda b, t: (k0 + b, 0)),
    ]
    args = [q, k, v]
    if with_ctx:
        in_specs += [
            pl.BlockSpec((PAST_LEN, kw), lambda b, t: (b, 0)),
            pl.BlockSpec((PAST_LEN, H_B * V_DIM), lambda b, t: (b, 0)),
        ]
        args += [k2, v2]
    return pl.pallas_call(
        functools.partial(_attn_kernel, with_ctx=with_ctx, seq_len=seq_len, n_seqs=n_seqs),
        grid=(n_batch // n_seqs, q_tiles),
        in_specs=in_specs,
        out_specs=pl.BlockSpec((n_seqs * TM, H_B * V_DIM), lambda b, t: (b * q_tiles + t, 0)),
        out_shape=jax.ShapeDtypeStruct((n_batch * seq_len, H_B * V_DIM), BF16),
        compiler_params=pltpu.CompilerParams(
            dimension_semantics=("arbitrary", "arbitrary"), vmem_limit_bytes=VMEM_LIMIT),
        name="attn_lat" if with_ctx else "attn_ctx",
    )(*args)


def _pool_bands():
    i = jnp.arange(TM)[:, None]
    e = jnp.arange(POOL_K)[None, :]
    e = jnp.where(e < TM, e, jnp.where(e < TM + HALO, e - TM - HALO, e - HALO))
    return jnp.stack([(e >= i - w // 2) & (e < i + w // 2) for w in POOL_WINDOWS]).astype(BF16)


def _pool_diffs(h, h_prev, h_next, pos0, seq_len, band_ref):
    ext = jnp.concatenate([h, h_prev, h_next, jnp.zeros((POOL_K - TM - 2 * HALO, D_MODEL), F32)], axis=0)
    pos = pos0 + lax.broadcasted_iota(jnp.int32, (TM, 1), 0)
    sums = [_dot_exact_lhs(band_ref[gi], ext[:, gi * GP:(gi + 1) * GP]) for gi in range(len(POOL_WINDOWS))]
    diffs = []
    for gi, win in enumerate(POOL_WINDOWS):
        half = win // 2
        cnt = (jnp.minimum(pos + half, seq_len) - jnp.maximum(pos - half, 0)).astype(F32)
        diffs.append((sums[gi] / cnt - h[:, gi * GP:(gi + 1) * GP]).astype(BF16))
    return diffs


def _tail_kernel(*refs, mixer, n_x, row_off):
    x_refs, rest = refs[:n_x], refs[n_x:]
    if mixer == "proj":
        a0_ref, a1_ref, b0_ref, b1_ref, ow_ref, mod_ref, g2_ref, w1_ref, w3_ref, w2_ref, o_ref = rest
    else:
        xp_ref, xn_ref, g1_ref, band_ref, pw_ref, ps_ref, mod_ref, g2_ref, w1_ref, w3_ref, w2_ref, o_ref = rest
    m = mod_ref[0, 0]
    x = _row_tile(x_refs, TM_FFN)
    if mixer == "proj":
        mix = (_dot(_row_tile((a0_ref, a1_ref), TM_FFN), ow_ref[:DA, :])
               + _dot(_row_tile((b0_ref, b1_ref), TM_FFN), ow_ref[DA:, :]))
    else:
        row0 = row_off + pl.program_id(0) * TM_FFN
        is_lat = row0 >= N_CTX
        seq_len = jnp.where(is_lat, DEC_SEQ, SEQ)
        norm_mod = lambda v: _rms(v, g1_ref[...]) * (1.0 + m[1:2]) + m[0:1]
        h, hp, hn = norm_mod(x), norm_mod(xp_ref[...]), norm_mod(xn_ref[...])
        parts = []
        n_sub = TM_FFN // TM
        for k in range(n_sub):
            r0 = row0 + k * TM
            pos0 = jnp.where(is_lat, (r0 - N_CTX) % DEC_SEQ, r0 % SEQ)
            prev = hp if k == 0 else h[k * TM - HALO:k * TM]
            nxt = hn if k == n_sub - 1 else h[(k + 1) * TM:(k + 1) * TM + HALO]
            prev = jnp.where(pos0 > 0, prev, 0.0)
            nxt = jnp.where(pos0 + TM < seq_len, nxt, 0.0)
            parts.append(_pool_diffs(h[k * TM:(k + 1) * TM], prev, nxt, pos0, seq_len, band_ref))
        mix = jnp.concatenate(
            [_dot(jnp.concatenate([p[gi] for p in parts], axis=0), pw_ref[gi]) for gi in range(len(POOL_WINDOWS))],
            axis=1) * ps_ref[...]
    x1 = x + m[2:3] * mix
    h2 = (_rms(x1, g2_ref[...]) * (1.0 + m[4:5]) + m[3:4]).astype(BF16)
    ups = [(_dot(h2, w1_ref[0, :, lo:hi]), _dot(h2, w3_ref[0, :, lo:hi])) for lo, hi in FF_SPLITS]
    acts = [(_silu(a) * b).astype(BF16) for a, b in ups]
    acc = functools.reduce(lambda p, q: p + q,
                           [_dot(u, w2_ref[0, lo:hi, :]) for u, (lo, hi) in zip(acts, FF_SPLITS)])
    o_ref[...] = x1 + m[5:6] * acc


def _layer_tail(x, mods, l, g2, w1, w3, w2, *, proj=None, pool=None, row_off=0, n_rows=N_TOK):
    per_tile = TM_FFN // TM
    t0 = row_off // TM_FFN
    const = lambda shape: pl.BlockSpec(shape, lambda i: tuple(0 for _ in shape))
    resident = lambda shape: pl.BlockSpec((1,) + shape, lambda i: (l, 0, 0), pipeline_mode=pl.Buffered(1))
    if proj is not None:
        assert row_off == 0 and n_rows == N_TOK
        a_out, b_out, out_w = proj
        x_specs, x_args = _row_specs(x, D_MODEL, TM_FFN)
        a_specs, a_args = _row_specs(a_out, DA, TM_FFN)
        b_specs, b_args = _row_specs(b_out, DA, TM_FFN)
        mix_specs = a_specs + b_specs + [const((D_MODEL, D_MODEL))]
        mix_args = a_args + b_args + [out_w]
        mixer = "proj"
    else:
        g1, pw, ps = pool
        r = TM_FFN // HALO
        last = N_TOK // HALO - 1
        x_specs = [pl.BlockSpec((TM_FFN, D_MODEL), lambda i: (t0 + i, 0))]
        x_args = [x]
        mix_specs = [
            pl.BlockSpec((HALO, D_MODEL), lambda i: (jnp.maximum((t0 + i) * r - 1, 0), 0)),
            pl.BlockSpec((HALO, D_MODEL), lambda i: (jnp.minimum((t0 + i + 1) * r, last), 0)),
            const((1, D_MODEL)), const((len(POOL_WINDOWS), TM, POOL_K)),
            const((len(POOL_WINDOWS), GP, GP)), const((1, D_MODEL)),
        ]
        mix_args = [x, x, g1, _pool_bands(), pw, ps]
        mixer = "pool"
    return pl.pallas_call(
        functools.partial(_tail_kernel, mixer=mixer, n_x=len(x_args), row_off=row_off),
        grid=(n_rows // TM_FFN,),
        in_specs=x_specs + mix_specs + [
            pl.BlockSpec((1, 1, N_MOD, D_MODEL), lambda i: (l, _cond_of_tile((t0 + i) * per_tile), 0, 0)),
            const((1, D_MODEL)),
            resident((D_MODEL, D_FF)), resident((D_MODEL, D_FF)), resident((D_FF, D_MODEL)),
        ],
        out_specs=pl.BlockSpec((TM_FFN, D_MODEL), lambda i: (i, 0)),
        out_shape=jax.ShapeDtypeStruct((n_rows, D_MODEL), F32),
        compiler_params=pltpu.CompilerParams(
            dimension_semantics=("arbitrary",), vmem_limit_bytes=VMEM_LIMIT),
        name="tail_" + mixer,
    )(*x_args, *mix_args, mods, g2, w1, w3, w2)


def _rope_tables():
    pos = jnp.arange(DEC_SEQ)
    freqs = ROPE_BASE ** (-jnp.arange(ROPE_AXIS_FREQS, dtype=F32) / ROPE_AXIS_FREQS)
    ang_row = (pos // GRID_W).astype(F32)[:, None] * freqs
    ang_col = (pos % GRID_W).astype(F32)[:, None] * freqs
    zeros = jnp.zeros((DEC_SEQ, LANES - ROPE_DIM), F32)
    cos = jnp.concatenate([jnp.cos(ang_row)] * 2 + [jnp.cos(ang_col)] * 2 + [zeros], axis=1)
    sin = jnp.concatenate([-jnp.sin(ang_row), jnp.sin(ang_row), -jnp.sin(ang_col), jnp.sin(ang_col), zeros], axis=1)
    return cos, sin


def _slab(v):
    return jnp.pad(v, (0, LANES - v.shape[0]))[None]


def _even_params(i, in_w, out_w, shift_mu, w0, w_up, a0, a_up, g_up, k_k, k_a, r_k, ln_g, ln_b,
                 cq_g, uq_w, ckv_g, ukv_w, qn_g, qr_g, kn_g, kr_g):
    row = lambda v: v[None]
    in_w_pad = jnp.pad(in_w[i], ((0, 0), (0, MLA_PAD - MLA_IN))).astype(BF16)
    mu = shift_mu[i]
    w_up_pad = jnp.pad(w_up[i], ((0, 0), (0, A_LORA), (0, 0)))
    a_up_pad = jnp.pad(a_up[i], ((0, 0), (W_LORA, 0), (0, 0)))
    uq = uq_w[i].reshape(Q_LORA, H_B, NOPE + ROPE_DIM)
    uq = jnp.pad(uq, ((0, 0), (0, 0), (0, 2 * LANES - NOPE - ROPE_DIM))).reshape(Q_LORA, 2 * H_B * LANES)
    ukv = ukv_w[i].reshape(KV_LORA, H_B, NOPE + V_DIM)
    ukv = jnp.concatenate([ukv[:, :, :NOPE].reshape(KV_LORA, -1), ukv[:, :, NOPE:].reshape(KV_LORA, -1)], axis=1)
    rw = dict(mu_r=row(mu[:DA]), mu_k=row(mu[DA:2 * DA]), mu_v=row(mu[2 * DA:3 * DA]), mu_l=row(mu[3 * DA:]),
              w0=w0[i], w_up=w_up_pad, a0=a0[i], a_up=a_up_pad, g_up=g_up[i],
              k_k=row(k_k[i]), k_a=row(k_a[i]), r_k=row(r_k[i].reshape(DA)), ln_g=row(ln_g[i]), ln_b=row(ln_b[i]))
    mla = dict(cq_g=row(cq_g[i]), uq_w=uq.astype(BF16), qn_g=row(qn_g[i]), qr_g=_slab(qr_g[i]),
               ckv_g=row(ckv_g[i]), ukv_w=ukv.astype(BF16), kn_g=row(kn_g[i]), kr_g=_slab(kr_g[i]))
    return in_w_pad, out_w[i].astype(BF16), rw, mla


def kernel(x_prompt, x_sample, c, state_rwkv, cache_mla_ckv, cache_mla_krope, c_ctx,
           norm1_g, norm2_g, ada_w, ada_b, ffn_w1, ffn_w3, ffn_w2,
           in_w, out_w, shift_mu, rwkv_w0, rwkv_w_up, rwkv_a0, rwkv_a_up, rwkv_g_up,
           rwkv_k_k, rwkv_k_a, rwkv_r_k, rwkv_ln_g, rwkv_ln_b,
           mla_cq_g, mla_uq_w, mla_ckv_g, mla_ukv_w, mla_qn_g, mla_qr_g, mla_kn_g, mla_kr_g,
           pool_w, pool_scale):
    x = (x_prompt.reshape(N_CTX, D_MODEL), x_sample.reshape(N_LAT, D_MODEL))
    conds = jnp.concatenate([c_ctx[None], c, jnp.zeros((8 - 1 - DEC_BATCH, D_MODEL), F32)], axis=0)
    mods = _adaln(conds, ada_w, ada_b).reshape(DEPTH, 8, N_MOD, D_MODEL)
    rope_tabs = _rope_tables()
    w1_all, w3_all, w2_all = ffn_w1.astype(BF16), ffn_w3.astype(BF16), ffn_w2.astype(BF16)

    new_s, new_ckv, new_kr = [], [], []
    for l in range(DEPTH):
        n1 = norm1_g[l][None]
        if l % 2 == 0:
            i = l // 2
            in_w_pad, out_w_b, rw, mla = _even_params(
                i, in_w, out_w, shift_mu, rwkv_w0, rwkv_w_up, rwkv_a0, rwkv_a_up, rwkv_g_up,
                rwkv_k_k, rwkv_k_a, rwkv_r_k, rwkv_ln_g, rwkv_ln_b,
                mla_cq_g, mla_uq_w, mla_ckv_g, mla_ukv_w, mla_qn_g, mla_qr_g, mla_kn_g, mla_kr_g)
            proj_r, q_all, k_all, v_all, ckv_all, kr_all = _inproj(x, mods, l, n1, in_w_pad, mla, rope_tabs)
            a_ctx, s_ctx = _rwkv(proj_r, rw, None, i, seq_len=SEQ, n_seq=BATCH, row_off=0)
            a_lat, _ = _rwkv(proj_r, rw, state_rwkv, i, seq_len=DEC_SEQ, n_seq=DEC_BATCH, row_off=N_CTX)
            kr_cache = jnp.pad(cache_mla_krope[:, i].reshape(DEC_BATCH * PAST_LEN, ROPE_DIM),
                               ((0, 0), (0, LANES - ROPE_DIM)))
            k_p, v_p = _mla_cache_expand(cache_mla_ckv[:, i].reshape(DEC_BATCH * PAST_LEN, KV_LORA), kr_cache, mla)
            b_ctx = _attention(q_all, k_all, v_all, n_batch=BATCH, seq_len=SEQ, row_off=0)
            b_lat = _attention(q_all, k_all, v_all, k_p, v_p, n_batch=DEC_BATCH, seq_len=DEC_SEQ, row_off=N_CTX)
            mixer = dict(proj=((a_ctx, a_lat), (b_ctx, b_lat), out_w_b))
            new_s.append(s_ctx)
            new_ckv.append(ckv_all[:N_CTX].reshape(BATCH, SEQ, KV_LORA))
            new_kr.append(kr_all[:N_CTX].reshape(BATCH, SEQ, ROPE_DIM))
        else:
            j = l // 2
            mixer = dict(pool=(n1, pool_w[j].astype(BF16), pool_scale[j][None]))
        tail_args = (mods, l, norm2_g[l][None], w1_all, w3_all, w2_all)
        if l < DEPTH - 1:
            x = _layer_tail(x, *tail_args, **mixer)
        else:
            y_p = _layer_tail(x, *tail_args, **mixer, row_off=0, n_rows=N_CTX).reshape(BATCH, SEQ, D_MODEL)
            y_s = _layer_tail(x, *tail_args, **mixer, row_off=N_CTX, n_rows=N_LAT).reshape(DEC_BATCH, DEC_SEQ, D_MODEL)
    return (y_p, y_s, jnp.stack(new_s, axis=1), jnp.stack(new_ckv, axis=1), jnp.stack(new_kr, axis=1))
```

```python
import functools

import jax
import jax.numpy as jnp
from jax import lax
from jax.experimental import pallas as pl
from jax.experimental.pallas import tpu as pltpu

D_MODEL = 1024
BATCH, SEQ = 32, 256
DEC_BATCH, DEC_SEQ = 2, 2048
DEPTH = 4
PAST_LEN = 256
GRID_W = 64
N_MOD = 6
D_FF = 2816
DA = 512
HD_A = 64
H_A = 8
W_LORA, A_LORA, G_LORA = 64, 64, 128
H_B = 4
NOPE, ROPE_DIM, V_DIM = 128, 64, 128
Q_LORA, KV_LORA = 384, 256
ATTN_SCALE = (NOPE + ROPE_DIM) ** -0.5
ROPE_AXIS_FREQS = ROPE_DIM // 4
ROPE_BASE = 10000.0
RWKV_IN = 3 * DA + W_LORA + A_LORA + G_LORA
MLA_IN = Q_LORA + KV_LORA + ROPE_DIM
MLA_PAD = 768
POOL_WINDOWS = (2, 4, 8, 16)
GP = 256
EPS = 1e-6
GN_EPS = 64e-5

N_CTX = BATCH * SEQ
N_LAT = DEC_BATCH * DEC_SEQ
N_TOK = N_CTX + N_LAT

LANES = 128
TM = 256
N_TILES = N_TOK // TM
CTX_TILES = N_CTX // TM
LAT_TILES_PER_SEQ = DEC_SEQ // TM
CHUNK = 64
CHUNKS_PER_TRIP = 8
SHORT_SEQ_TRIP = 16
ATTN_SEQS_PER_STEP = 4
HALO = 8
POOL_K = 384
TM_FFN = 512
TM_IN = 512
MXU_WIDTH = 256
FF_SPLITS = ((0, 6 * MXU_WIDTH), (6 * MXU_WIDTH, D_FF))
VMEM_LIMIT = 56 * 1024 * 1024

F32 = jnp.float32
BF16 = jnp.bfloat16


def _cond_of_tile(t):
    return jnp.where(t < CTX_TILES, 0, 1 + (t - CTX_TILES) // LAT_TILES_PER_SEQ)


def _dot(a, b, dims=((1,), (0,))):
    return lax.dot_general(a, b, (dims, ((), ())), preferred_element_type=F32)


_NT = ((1,), (1,))


def _split2(x):
    hi = x.astype(BF16)
    lo = (x - hi.astype(F32)).astype(BF16)
    return hi, lo


def _dot1(a, b, dims=((1,), (0,))):
    return _dot(a.astype(BF16), b.astype(BF16), dims)


def _dot3(a, b, dims=((1,), (0,))):
    ah, al = _split2(a)
    bh, bl = _split2(b)
    return _dot(ah, bh, dims) + (_dot(ah, bl, dims) + _dot(al, bh, dims))


def _dot_exact_lhs(a_bf16, b):
    b0, b1 = _split2(b)
    return _dot(a_bf16, b0) + _dot(a_bf16, b1)


def _rms(x, g):
    return x * lax.rsqrt(jnp.mean(x * x, axis=-1, keepdims=True) + EPS) * g


def _sigmoid(x):
    return 1.0 / (1.0 + jnp.exp(-x))


def _silu(x):
    return x * _sigmoid(x)


def _softplus(x):
    return jnp.maximum(x, 0.0) + jnp.log(1.0 + jnp.exp(-jnp.abs(x)))


def _group_ones(n, g):
    i = lax.broadcasted_iota(jnp.int32, (n, n), 0) // g
    j = lax.broadcasted_iota(jnp.int32, (n, n), 1) // g
    return (i == j).astype(BF16)


def _gsum(x, ones):
    hi, lo = _split2(x)
    return _dot(hi, ones) + _dot(lo, ones)


def _adaln_kernel(c_ref, w_ref, b_ref, o_ref):
    o_ref[0] = _dot3(_silu(c_ref[...]), w_ref[0]) + b_ref[0]


def _adaln(conds, ada_w, ada_b):
    tn = 1536
    return pl.pallas_call(
        _adaln_kernel,
        grid=(DEPTH, N_MOD * D_MODEL // tn),
        in_specs=[
            pl.BlockSpec((8, D_MODEL), lambda l, j: (0, 0)),
            pl.BlockSpec((1, D_MODEL, tn), lambda l, j: (l, 0, j)),
            pl.BlockSpec((1, 1, tn), lambda l, j: (l, 0, j)),
        ],
        out_specs=pl.BlockSpec((1, 8, tn), lambda l, j: (l, 0, j)),
        out_shape=jax.ShapeDtypeStruct((DEPTH, 8, N_MOD * D_MODEL), F32),
        compiler_params=pltpu.CompilerParams(
            dimension_semantics=("arbitrary", "arbitrary"), vmem_limit_bytes=VMEM_LIMIT),
        name="adaln",
    )(conds, ada_w, ada_b.reshape(DEPTH, 1, N_MOD * D_MODEL))


def _mod_spec(l):
    return pl.BlockSpec((1, 1, N_MOD, D_MODEL), lambda t, *_: (l, _cond_of_tile(t), 0, 0))


def _row_specs(x, width, tm=TM):
    if not isinstance(x, tuple):
        return [pl.BlockSpec((tm, width), lambda t, *_: (t, 0))], [x]
    ctx_tiles = N_CTX // tm
    return [pl.BlockSpec((tm, width), lambda t, *_: (jnp.minimum(t, ctx_tiles - 1), 0)),
            pl.BlockSpec((tm, width), lambda t, *_: (jnp.maximum(t - ctx_tiles, 0), 0))], list(x)


def _row_tile(refs, tm=TM):
    if len(refs) == 1:
        return refs[0][...]
    return jnp.where(pl.program_id(0) < N_CTX // tm, refs[0][...], refs[1][...])


def _inproj_kernel(*refs, n_x):
    x_refs, (mod_ref, g_ref, w_ref, *mla_refs) = refs[:n_x], refs[n_x:]
    *mla_in, or_ref, q_ref, k_ref, v_ref, ckv_ref, kr_ref = mla_refs
    *mla_w, cos_ref, sin_ref = mla_in
    m = mod_ref[0, 0]
    x = _row_tile(x_refs, TM_IN)
    is_lat = pl.program_id(0) >= N_CTX // TM_IN
    halves = [pl.ds(k * TM, TM) for k in range(TM_IN // TM)]
    ps = [_dot((_rms(x[k * TM:(k + 1) * TM], g_ref[...]) * (1.0 + m[1:2]) + m[0:1]).astype(BF16), w_ref[...])
          for k in range(TM_IN // TM)]
    for rows, p in zip(halves, ps):
        or_ref[rows, :] = p[:, :RWKV_IN]
        _mla_project(p[:, RWKV_IN:], is_lat, *mla_w, cos_ref.at[rows], sin_ref.at[rows],
                     *(r.at[rows] for r in (q_ref, k_ref, v_ref, ckv_ref, kr_ref)))


def _inproj(x, mods, l, g, w, p, rope_tabs):
    x_specs, x_args = _row_specs(x, D_MODEL, TM_IN)
    per_tile = TM_IN // TM
    ctx_tiles, lat_tiles_per_seq = N_CTX // TM_IN, DEC_SEQ // TM_IN
    full = lambda shape: pl.BlockSpec(shape, lambda t: tuple(0 for _ in shape))
    rope_spec = pl.BlockSpec((TM_IN, LANES), lambda t: (jnp.maximum(t - ctx_tiles, 0) % lat_tiles_per_seq, 0))
    rows = lambda w_: pl.BlockSpec((TM_IN, w_), lambda t: (t, 0))
    return pl.pallas_call(
        functools.partial(_inproj_kernel, n_x=len(x_args)),
        grid=(N_TOK // TM_IN,),
        in_specs=x_specs + [
            pl.BlockSpec((1, 1, N_MOD, D_MODEL), lambda t: (l, _cond_of_tile(t * per_tile), 0, 0)),
            full((1, D_MODEL)),
            full((D_MODEL, RWKV_IN + MLA_PAD)),
            full((1, Q_LORA)), full((Q_LORA, 2 * H_B * LANES)), full((1, LANES)), full((1, LANES)),
            full((1, KV_LORA)), full((KV_LORA, 2 * H_B * NOPE)), full((1, LANES)), full((1, LANES)),
            rope_spec, rope_spec,
        ],
        out_specs=[rows(RWKV_IN), rows(2 * H_B * LANES), rows(2 * H_B * LANES), rows(H_B * V_DIM),
                   rows(KV_LORA), rows(ROPE_DIM)],
        out_shape=[
            jax.ShapeDtypeStruct((N_TOK, RWKV_IN), F32),
            jax.ShapeDtypeStruct((N_TOK, 2 * H_B * LANES), BF16),
            jax.ShapeDtypeStruct((N_TOK, 2 * H_B * LANES), BF16),
            jax.ShapeDtypeStruct((N_TOK, H_B * V_DIM), BF16),
            jax.ShapeDtypeStruct((N_TOK, KV_LORA), F32),
            jax.ShapeDtypeStruct((N_TOK, ROPE_DIM), F32),
        ],
        compiler_params=pltpu.CompilerParams(
            dimension_semantics=("arbitrary",), vmem_limit_bytes=VMEM_LIMIT),
        name="inproj",
    )(*x_args, mods, g, w, p["cq_g"], p["uq_w"], p["qn_g"], p["qr_g"], p["ckv_g"], p["ukv_w"], p["kn_g"],
      p["kr_g"], *rope_tabs)


def _rwkv_kernel(pr_ref, pk_ref, pv_ref, pl_ref, mur_ref, muk_ref, muv_ref, mul_ref,
                 w0_ref, wup_ref, a0_ref, aup_ref, gup_ref, kkg_ref, kag_ref, rkg_ref,
                 lng_ref, lnb_ref, *rest, seq_len, n_seqs, trip, zero_init):
    if zero_init:
        h0_ref = None
        out_ref, hfin_ref, m_s, n_s, yq_s, y0_s, g_s, bonus_s = rest
    else:
        h0_ref, out_ref, hfin_ref, m_s, n_s, yq_s, y0_s, g_s, bonus_s = rest
    C = CHUNK
    n_chunks = seq_len // C
    all_chunks = n_seqs * n_chunks
    static_trip = all_chunks == trip
    lane = lax.broadcasted_iota(jnp.int32, (1, LANES), 1)
    m0 = (lane < HD_A).astype(F32)
    m1 = 1.0 - m0
    ones_g = _group_ones(LANES, HD_A)
    row = lax.broadcasted_iota(jnp.int32, (C, 1), 0)
    i2 = lax.broadcasted_iota(jnp.int32, (2 * C, 2 * C), 0)
    j2 = lax.broadcasted_iota(jnp.int32, (2 * C, 2 * C), 1)
    same_head = (i2 // C) == (j2 // C)
    eye2 = (i2 == j2).astype(F32)
    ic = lax.broadcasted_iota(jnp.int32, (C, C), 0)
    jc = lax.broadcasted_iota(jnp.int32, (C, C), 1)
    incl_c = ((jc <= ic).astype(BF16), (jc >= ic).astype(BF16))
    incl_blk = (same_head & (j2 <= i2), same_head & (j2 >= i2))
    strict_blk = (same_head & (j2 < i2), same_head & (j2 > i2))

    def pair_masks(d):
        masks = []
        s = 1
        while s < C:
            later, earlier = ((i2 // s) % 2, (j2 // s) % 2) if d == 0 else ((j2 // s) % 2, (i2 // s) % 2)
            masks.append(((i2 // (2 * s)) == (j2 // (2 * s))) & (later == 1) & (earlier == 0))
            s *= 2
        return masks

    pair_blk = (pair_masks(0), pair_masks(1))

    def stack(x):
        return jnp.concatenate([x * m0, x * m1], axis=0)

    def chunk_rows(c):
        return pl.ds(c * C, C) if isinstance(c, int) else pl.ds(pl.multiple_of(c * C, C), C)

    def shifted(ref, mu, c):
        x = ref[chunk_rows(c), :]
        if isinstance(c, int):
            first, last = c % n_chunks == 0, c % n_chunks == n_chunks - 1
            prev_row = jnp.zeros((1, x.shape[1]), F32) if first else ref[c * C - 1:c * C, :]
            next_row = jnp.zeros((1, x.shape[1]), F32) if last else ref[(c + 1) * C:(c + 1) * C + 1, :]
        else:
            assert n_seqs == 1
            pstart = pl.multiple_of(jnp.maximum(c * C - 8, 0), 8)
            nstart = pl.multiple_of(jnp.minimum((c + 1) * C, seq_len - 8), 8)
            prev_row = jnp.where(c > 0, ref[pl.ds(pstart, 8), :][7:8], 0.0)
            next_row = jnp.where(c < n_chunks - 1, ref[pl.ds(nstart, 8), :][0:1], 0.0)
        prev = jnp.where(row == 0, prev_row, pltpu.roll(x, 1, 0))
        nxt = jnp.where(row == C - 1, next_row, pltpu.roll(x, C - 1, 0))
        return x + mu * (0.5 * (prev + nxt) - x)

    def chain_mats(chains):
        ds = [ch[0] for ch in chains]
        lp = [_dot_exact_lhs(incl_c[d], logw) for d, *_, logw in chains]
        lpc = [x[C - 1:C] if d == 0 else x[0:1] for d, x in zip(ds, lp)]
        st = []
        for (d, r, kd, v, kk, b, logw), x, xc in zip(chains, lp, lpc):
            p_in, p_ex, p_inv, p_end = jnp.exp(x), jnp.exp(x - logw), jnp.exp(-x), jnp.exp(xc - x)
            st.append(dict(qs=stack(r * p_in), a_s=stack(kk * p_ex), khs=stack(kd * p_inv),
                           bs=stack(b * p_inv), kbs=stack(kd * p_end), bbs=stack(b * p_end), vs=stack(v)))
        R = 2 * C
        quad = [_dot1(jnp.concatenate([s["a_s"], s["qs"]], axis=0),
                      jnp.concatenate([s["bs"], s["khs"]], axis=0), _NT) for s in st]
        lab = [jnp.where(strict_blk[d], q[:R, :R], 0.0) for d, q in zip(ds, quad)]
        lak = [jnp.where(strict_blk[d], q[:R, R:], 0.0) for d, q in zip(ds, quad)]
        dqb = [jnp.where(incl_blk[d], q[R:, :R], 0.0) for d, q in zip(ds, quad)]
        dqk = [jnp.where(incl_blk[d], q[R:, R:], 0.0) for d, q in zip(ds, quad)]
        x2 = [_dot1(l, s["vs"]) for l, s in zip(lak, st)]
        tinv = [eye2 - jnp.where(pair_blk[d][0], l, 0.0) for d, l in zip(ds, lab)]
        for lvl in range(1, len(pair_blk[0])):
            half = [_dot1(t, jnp.where(pair_blk[d][lvl], l, 0.0)) for d, t, l in zip(ds, tinv, lab)]
            tinv = [t - _dot1(h, t) for h, t in zip(half, tinv)]
        w12 = [_dot1(t, jnp.concatenate([s["a_s"], x], axis=1)) for t, s, x in zip(tinv, st, x2)]
        zero_blk = jnp.zeros((R, R), F32)
        fin = [_dot1(jnp.concatenate([jnp.concatenate([s["bbs"].T, -s["kbs"].T], axis=1),
                                      jnp.concatenate([qb, -qk], axis=1)], axis=0),
                     jnp.concatenate([w, jnp.concatenate([zero_blk, s["vs"]], axis=1)], axis=0))
               for s, qb, qk, w in zip(st, dqb, dqk, w12)]
        fold = lambda x: x[:C] + x[C:]
        return [tuple(fold(x) for x in (eye2 * jnp.exp(xc) - f[:R, :R], -f[:R, R:], s["qs"] - f[R:, :R], -f[R:, R:]))
                for xc, f, s in zip(lpc, fin, st)]

    def load_chunk(c):
        return (shifted(pr_ref, mur_ref[...], c), shifted(pk_ref, muk_ref[...], c),
                shifted(pv_ref, muv_ref[...], c), shifted(pl_ref, mul_ref[...], c))

    def compute_chunks(loaded):
        rows_of = lambda j: slice(j * C, (j + 1) * C)
        r_all, k_all, v_all, lo_all = (jnp.concatenate([ch[j] for ch in loaded], axis=0) for j in range(4))
        wa = lo_all[:, :LANES]
        gate_all = _dot1(_sigmoid(lo_all[:, LANES:]), gup_ref[...])
        kkr = k_all * kkg_ref[...]
        kk_all = kkr / (jnp.sqrt(_gsum(kkr * kkr, ones_g)) + 1e-12)
        bonus_all = _gsum(r_all * k_all * rkg_ref[...], ones_g) * v_all
        both = lambda ref: jnp.concatenate([ref[0], ref[1]], axis=1)
        both_rows = lambda ref: jnp.concatenate([ref[0:1, :], ref[1:2, :]], axis=1)
        wl = both_rows(w0_ref) + _dot3(jnp.tanh(wa), both(wup_ref))
        logw_all = -jnp.exp(-_softplus(-wl) - 0.5)
        a_all = _sigmoid(both_rows(a0_ref) + _dot3(wa, both(aup_ref)))
        chains = []
        for j, (r, k, v, _) in enumerate(loaded):
            kk = kk_all[rows_of(j)]
            for d in range(2):
                a = a_all[rows_of(j), d * LANES:(d + 1) * LANES]
                logw = logw_all[rows_of(j), d * LANES:(d + 1) * LANES]
                chains.append((d, r, k * (1.0 + (a - 1.0) * kag_ref[...]), v, kk, kk * a, logw))
        mats = chain_mats(chains)
        return [(gate_all[rows_of(j)], bonus_all[rows_of(j)], mats[2 * j:2 * j + 2]) for j in range(len(loaded))]

    def store_chunk(c, gate, bonus, mats):
        g_s[chunk_rows(c), :] = gate
        bonus_s[chunk_rows(c), :] = bonus
        for d in range(2):
            m_s[d, c], n_s[d, c], yq_s[d, c], y0_s[d, c] = mats[d]

    def phase_a_trip(i, carry):
        chunks = [i * trip + u for u in range(trip)]
        results = compute_chunks([load_chunk(c) for c in chunks])
        for c, res in zip(chunks, results):
            store_chunk(c, *res)
        return carry

    if static_trip:
        phase_a_trip(0, 0)
    else:
        lax.fori_loop(0, all_chunks // trip, phase_a_trip, 0)

    def phase_b(i, states):
        first_zero = zero_init and isinstance(i, int) and i == 0
        units = [(q, d, q * n_chunks + (i if d == 0 else n_chunks - 1 - i))
                 for q in range(n_seqs) for d in range(2)]
        if first_zero:
            return tuple(stack(n_s[d, c]) for _, d, c in units)
        new_states = tuple(stack(_dot3(m_s[d, c], states[2 * q + d]) + n_s[d, c]) for q, d, c in units)
        for q, d, c in units:
            y0_s[d, c] = y0_s[d, c] + _dot1(yq_s[d, c], states[2 * q + d])
        return new_states

    zeros_half = jnp.zeros((HD_A, HD_A), F32)
    states = []
    for q in range(n_seqs):
        for d in range(2):
            if zero_init:
                states.append(jnp.zeros((LANES, LANES), F32))
            else:
                top = jnp.concatenate([h0_ref[q, 0, d, 0], zeros_half], axis=1)
                bot = jnp.concatenate([zeros_half, h0_ref[q, 0, d, 1]], axis=1)
                states.append(jnp.concatenate([top, bot], axis=0).T)
    states = tuple(states)
    if static_trip:
        for i in range(n_chunks):
            states = phase_b(i, states)
    else:
        states = lax.fori_loop(0, n_chunks, phase_b, states)
    for q in range(n_seqs):
        for d in range(2):
            s_vk = states[2 * q + d].T
            hfin_ref[q, d, 0] = s_vk[:HD_A, :HD_A]
            hfin_ref[q, d, 1] = s_vk[HD_A:, HD_A:]

    def phase_c_trip(i, carry):
        chunks = [i * trip + u for u in range(trip)]
        ys = [y0_s[0, c] + y0_s[1, c] for c in chunks]
        extras = [(bonus_s[chunk_rows(c), :], g_s[chunk_rows(c), :]) for c in chunks]
        y_all = jnp.concatenate(ys, axis=0)
        yc = y_all - _gsum(y_all, ones_g) * (1.0 / HD_A)
        var = _gsum(yc * yc, ones_g) * (1.0 / HD_A)
        yn = yc * lax.rsqrt(var + GN_EPS) * lng_ref[...] + lnb_ref[...]
        for j, (c, (bonus, gate)) in enumerate(zip(chunks, extras)):
            out_ref[chunk_rows(c), :] = ((yn[j * C:(j + 1) * C] + bonus) * gate).astype(out_ref.dtype)
        return carry

    if static_trip:
        phase_c_trip(0, 0)
    else:
        lax.fori_loop(0, all_chunks // trip, phase_c_trip, 0)


def _rwkv(proj_r, p, h0, layer, *, seq_len, n_seq, row_off):
    zero_init = h0 is None
    trip = SHORT_SEQ_TRIP if seq_len // CHUNK < CHUNKS_PER_TRIP else CHUNKS_PER_TRIP
    seqs = max(1, trip // (seq_len // CHUNK))
    blk_rows = seqs * seq_len
    n_chunks = blk_rows // CHUNK
    off = row_off // blk_rows
    n_pairs = DA // LANES
    col = lambda cb: (lambda s, hp: (off + s, cb + hp))
    vec = lambda: pl.BlockSpec((1, LANES), lambda s, hp: (0, hp))
    in_specs = [
        pl.BlockSpec((blk_rows, LANES), col(0)),
        pl.BlockSpec((blk_rows, LANES), col(n_pairs)),
        pl.BlockSpec((blk_rows, LANES), col(2 * n_pairs)),
        pl.BlockSpec((blk_rows, 2 * LANES), lambda s, hp: (off + s, 3 * n_pairs // 2)),
        vec(), vec(), vec(),
        pl.BlockSpec((1, 2 * LANES), lambda s, hp: (0, 0)),
        pl.BlockSpec((2, LANES), lambda s, hp: (0, hp)),
        pl.BlockSpec((2, LANES, LANES), lambda s, hp: (0, 0, hp)),
        pl.BlockSpec((2, LANES), lambda s, hp: (0, hp)),
        pl.BlockSpec((2, LANES, LANES), lambda s, hp: (0, 0, hp)),
        pl.BlockSpec((G_LORA, LANES), lambda s, hp: (0, hp)),
        vec(), vec(), vec(), vec(), vec(),
    ]
    args = [proj_r, proj_r, proj_r, proj_r, p["mu_r"], p["mu_k"], p["mu_v"], p["mu_l"],
            p["w0"], p["w_up"], p["a0"], p["a_up"], p["g_up"], p["k_k"], p["k_a"], p["r_k"],
            p["ln_g"], p["ln_b"]]
    if not zero_init:
        in_specs.append(pl.BlockSpec((seqs, 1, 2, 2, HD_A, HD_A), lambda s, hp: (s, layer, 0, hp, 0, 0)))
        args.append(h0)
    blk = (CHUNK, LANES)
    return pl.pallas_call(
        functools.partial(_rwkv_kernel, seq_len=seq_len, n_seqs=seqs, trip=trip, zero_init=zero_init),
        grid=(n_seq // seqs, n_pairs),
        in_specs=in_specs,
        out_specs=[
            pl.BlockSpec((blk_rows, LANES), lambda s, hp: (s, hp)),
            pl.BlockSpec((seqs, 2, 2, HD_A, HD_A), lambda s, hp: (s, 0, hp, 0, 0)),
        ],
        out_shape=[
            jax.ShapeDtypeStruct((n_seq * seq_len, DA), BF16),
            jax.ShapeDtypeStruct((n_seq, 2, H_A, HD_A, HD_A), F32),
        ],
        scratch_shapes=[
            pltpu.VMEM((2, n_chunks) + blk, F32),
            pltpu.VMEM((2, n_chunks) + blk, F32),
            pltpu.VMEM((2, n_chunks) + blk, F32),
            pltpu.VMEM((2, n_chunks) + blk, F32),
            pltpu.VMEM((blk_rows, LANES), F32),
            pltpu.VMEM((blk_rows, LANES), F32),
        ],
        compiler_params=pltpu.CompilerParams(
            dimension_semantics=("arbitrary", "arbitrary"), vmem_limit_bytes=VMEM_LIMIT),
        name="rwkv_ctx" if zero_init else "rwkv_lat",
    )(*args)


def _rope(x, cos, sin):
    lane = lax.broadcasted_iota(jnp.int32, x.shape, 1)
    swapped = jnp.where((lane % 32) < 16, pltpu.roll(x, LANES - 16, 1), pltpu.roll(x, 16, 1))
    return x * cos + swapped * sin


def _slab_rms(x, g):
    return x * lax.rsqrt(jnp.sum(x * x, axis=-1, keepdims=True) * (1.0 / ROPE_DIM) + EPS) * g


def _kv_expand(ckv_n, kr_att, ukv_ref, kng_ref, k_ref, v_ref):
    kv = _dot(ckv_n.astype(BF16), ukv_ref[...])
    kr_b = kr_att.astype(BF16)
    for h in range(H_B):
        kn = _rms(kv[:, h * NOPE:(h + 1) * NOPE], kng_ref[...])
        k_ref[:, 2 * h * LANES:(2 * h + 1) * LANES] = kn.astype(BF16)
        k_ref[:, (2 * h + 1) * LANES:(2 * h + 2) * LANES] = kr_b
    v_ref[...] = kv[:, H_B * NOPE:].astype(BF16)


def _mla_project(pm, is_lat, cqg_ref, uq_ref, qng_ref, qrg_ref, ckvg_ref, ukv_ref, kng_ref, krg_ref,
                 cos_ref, sin_ref, q_ref, k_ref, v_ref, ckv_ref, kr_ref):
    cq = _rms(pm[:, :Q_LORA], cqg_ref[...])
    ckv_n = _rms(pm[:, Q_LORA:Q_LORA + KV_LORA], ckvg_ref[...])
    q = _dot(cq.astype(BF16), uq_ref[...])
    ckv_ref[...] = ckv_n
    kr = _slab_rms(pm[:, Q_LORA + KV_LORA:], krg_ref[...])
    kr_ref[...] = kr[:, :ROPE_DIM]
    _kv_expand(ckv_n, kr, ukv_ref, kng_ref, k_ref, v_ref)
    qrs = []
    for h in range(H_B):
        qn = _rms(q[:, 2 * h * LANES:(2 * h + 1) * LANES], qng_ref[...])
        qrs.append(_slab_rms(q[:, (2 * h + 1) * LANES:(2 * h + 2) * LANES], qrg_ref[...]))
        q_ref[:, 2 * h * LANES:(2 * h + 1) * LANES] = (qn * ATTN_SCALE).astype(BF16)
        q_ref[:, (2 * h + 1) * LANES:(2 * h + 2) * LANES] = (qrs[h] * ATTN_SCALE).astype(BF16)

    @pl.when(is_lat)
    def _():
        cos, sin = cos_ref[...], sin_ref[...]
        kr_rot = _rope(kr, cos, sin).astype(BF16)
        for h in range(H_B):
            k_ref[:, (2 * h + 1) * LANES:(2 * h + 2) * LANES] = kr_rot
            q_ref[:, (2 * h + 1) * LANES:(2 * h + 2) * LANES] = (_rope(qrs[h], cos, sin) * ATTN_SCALE).astype(BF16)


def _mla_cache_kernel(ckv_ref, kr_ref, ukv_ref, kng_ref, k_ref, v_ref):
    _kv_expand(ckv_ref[...], kr_ref[...], ukv_ref, kng_ref, k_ref, v_ref)


def _mla_cache_expand(ckv, kr_slab, p):
    n = ckv.shape[0]
    full = lambda shape: pl.BlockSpec(shape, lambda t: tuple(0 for _ in shape))
    return pl.pallas_call(
        _mla_cache_kernel,
        grid=(1,),
        in_specs=[full((n, KV_LORA)), full((n, LANES)), full((KV_LORA, 2 * H_B * NOPE)), full((1, LANES))],
        out_specs=[full((n, 2 * H_B * LANES)), full((n, H_B * V_DIM))],
        out_shape=[
            jax.ShapeDtypeStruct((n, 2 * H_B * LANES), BF16),
            jax.ShapeDtypeStruct((n, H_B * V_DIM), BF16),
        ],
        compiler_params=pltpu.CompilerParams(
            dimension_semantics=("arbitrary",), vmem_limit_bytes=VMEM_LIMIT),
        name="mla_cache_expand",
    )(ckv, kr_slab, p["ukv_w"], p["kn_g"])


def _attn_kernel(q_ref, k_ref, v_ref, *rest, with_ctx, seq_len, n_seqs):
    if with_ctx:
        k2_ref, v2_ref, o_ref = rest
    else:
        (o_ref,) = rest
    hs = [slice(2 * h * LANES, (2 * h + 2) * LANES) for h in range(H_B)]
    vs = [slice(h * V_DIM, (h + 1) * V_DIM) for h in range(H_B)]
    units = [(b, h) for b in range(n_seqs) for h in range(H_B)]
    qrows = lambda b: slice(b * TM, (b + 1) * TM)
    krows = lambda b: slice(b * seq_len, (b + 1) * seq_len)

    def keys_values(b, h):
        kv = [(k_ref[krows(b), hs[h]], v_ref[krows(b), vs[h]])]
        if with_ctx:
            kv.append((k2_ref[:, hs[h]], v2_ref[:, vs[h]]))
        return kv

    scores = [[_dot(q_ref[qrows(b), hs[h]], k, _NT) for k, _ in keys_values(b, h)] for b, h in units]
    probs, dens = [], []
    for sc in scores:
        m = functools.reduce(jnp.maximum, [jnp.max(s, axis=-1, keepdims=True) for s in sc])
        pr = [jnp.exp(s - m) for s in sc]
        dens.append(functools.reduce(lambda a, b: a + b, [jnp.sum(p, axis=-1, keepdims=True) for p in pr]))
        probs.append([p.astype(BF16) for p in pr])
    for (b, h), pr, den in zip(units, probs, dens):
        o = functools.reduce(lambda a, c: a + c, [_dot(p, v) for p, (_, v) in zip(pr, keys_values(b, h))])
        o_ref[qrows(b), vs[h]] = (o / den).astype(o_ref.dtype)


def _attention(q, k, v, k2=None, v2=None, *, n_batch, seq_len, row_off):
    with_ctx = k2 is not None
    q_tiles = seq_len // TM
    n_seqs = ATTN_SEQS_PER_STEP if q_tiles == 1 else 1
    kw = 2 * H_B * LANES
    q0 = row_off // (n_seqs * TM)
    k0 = row_off // (n_seqs * seq_len)
    in_specs = [
        pl.BlockSpec((n_seqs * TM, kw), lambda b, t: (q0 + b * q_tiles + t, 0)),
        pl.BlockSpec((n_seqs * seq_len, kw), lambda b, t: (k0 + b, 0)),
        pl.BlockSpec((n_seqs * seq_len, H_B * V_DIM), lambda b, t: (k0 + b, 0)),
    ]
    args = [q, k, v]
    if with_ctx:
        in_specs += [
            pl.BlockSpec((PAST_LEN, kw), lambda b, t: (b, 0)),
            pl.BlockSpec((PAST_LEN, H_B * V_DIM), lambda b, t: (b, 0)),
        ]
        args += [k2, v2]
    return pl.pallas_call(
        functools.partial(_attn_kernel, with_ctx=with_ctx, seq_len=seq_len, n_seqs=n_seqs),
        grid=(n_batch // n_seqs, q_tiles),
        in_specs=in_specs,
        out_specs=pl.BlockSpec((n_seqs * TM, H_B * V_DIM), lambda b, t: (b * q_tiles + t, 0)),
        out_shape=jax.ShapeDtypeStruct((n_batch * seq_len, H_B * V_DIM), BF16),
        compiler_params=pltpu.CompilerParams(
            dimension_semantics=("arbitrary", "arbitrary"), vmem_limit_bytes=VMEM_LIMIT),
        name="attn_lat" if with_ctx else "attn_ctx",
    )(*args)


def _pool_bands():
    i = jnp.arange(TM)[:, None]
    e = jnp.arange(POOL_K)[None, :]
    e = jnp.where(e < TM, e, jnp.where(e < TM + HALO, e - TM - HALO, e - HALO))
    return jnp.stack([(e >= i - w // 2) & (e < i + w // 2) for w in POOL_WINDOWS]).astype(BF16)


def _pool_diffs(h, h_prev, h_next, pos0, seq_len, band_ref):
    ext = jnp.concatenate([h, h_prev, h_next, jnp.zeros((POOL_K - TM - 2 * HALO, D_MODEL), F32)], axis=0)
    pos = pos0 + lax.broadcasted_iota(jnp.int32, (TM, 1), 0)
    sums = [_dot_exact_lhs(band_ref[gi], ext[:, gi * GP:(gi + 1) * GP]) for gi in range(len(POOL_WINDOWS))]
    diffs = []
    for gi, win in enumerate(POOL_WINDOWS):
        half = win // 2
        cnt = (jnp.minimum(pos + half, seq_len) - jnp.maximum(pos - half, 0)).astype(F32)
        diffs.append((sums[gi] / cnt - h[:, gi * GP:(gi + 1) * GP]).astype(BF16))
    return diffs


def _tail_kernel(*refs, mixer, n_x, row_off):
    x_refs, rest = refs[:n_x], refs[n_x:]
    if mixer == "proj":
        a0_ref, a1_ref, b0_ref, b1_ref, ow_ref, mod_ref, g2_ref, w1_ref, w3_ref, w2_ref, o_ref = rest
    else:
        xp_ref, xn_ref, g1_ref, band_ref, pw_ref, ps_ref, mod_ref, g2_ref, w1_ref, w3_ref, w2_ref, o_ref = rest
    m = mod_ref[0, 0]
    x = _row_tile(x_refs, TM_FFN)
    if mixer == "proj":
        mix = (_dot(_row_tile((a0_ref, a1_ref), TM_FFN), ow_ref[:DA, :])
               + _dot(_row_tile((b0_ref, b1_ref), TM_FFN), ow_ref[DA:, :]))
    else:
        row0 = row_off + pl.program_id(0) * TM_FFN
        is_lat = row0 >= N_CTX
        seq_len = jnp.where(is_lat, DEC_SEQ, SEQ)
        norm_mod = lambda v: _rms(v, g1_ref[...]) * (1.0 + m[1:2]) + m[0:1]
        h, hp, hn = norm_mod(x), norm_mod(xp_ref[...]), norm_mod(xn_ref[...])
        parts = []
        n_sub = TM_FFN // TM
        for k in range(n_sub):
            r0 = row0 + k * TM
            pos0 = jnp.where(is_lat, (r0 - N_CTX) % DEC_SEQ, r0 % SEQ)
            prev = hp if k == 0 else h[k * TM - HALO:k * TM]
            nxt = hn if k == n_sub - 1 else h[(k + 1) * TM:(k + 1) * TM + HALO]
            prev = jnp.where(pos0 > 0, prev, 0.0)
            nxt = jnp.where(pos0 + TM < seq_len, nxt, 0.0)
            parts.append(_pool_diffs(h[k * TM:(k + 1) * TM], prev, nxt, pos0, seq_len, band_ref))
        mix = jnp.concatenate(
            [_dot(jnp.concatenate([p[gi] for p in parts], axis=0), pw_ref[gi]) for gi in range(len(POOL_WINDOWS))],
            axis=1) * ps_ref[...]
    x1 = x + m[2:3] * mix
    h2 = (_rms(x1, g2_ref[...]) * (1.0 + m[4:5]) + m[3:4]).astype(BF16)
    ups = [(_dot(h2, w1_ref[0, :, lo:hi]), _dot(h2, w3_ref[0, :, lo:hi])) for lo, hi in FF_SPLITS]
    acts = [(_silu(a) * b).astype(BF16) for a, b in ups]
    acc = functools.reduce(lambda p, q: p + q,
                           [_dot(u, w2_ref[0, lo:hi, :]) for u, (lo, hi) in zip(acts, FF_SPLITS)])
    o_ref[...] = x1 + m[5:6] * acc


def _layer_tail(x, mods, l, g2, w1, w3, w2, *, proj=None, pool=None, row_off=0, n_rows=N_TOK):
    per_tile = TM_FFN // TM
    t0 = row_off // TM_FFN
    const = lambda shape: pl.BlockSpec(shape, lambda i: tuple(0 for _ in shape))
    resident = lambda shape: pl.BlockSpec((1,) + shape, lambda i: (l, 0, 0), pipeline_mode=pl.Buffered(1))
    if proj is not None:
        assert row_off == 0 and n_rows == N_TOK
        a_out, b_out, out_w = proj
        x_specs, x_args = _row_specs(x, D_MODEL, TM_FFN)
        a_specs, a_args = _row_specs(a_out, DA, TM_FFN)
        b_specs, b_args = _row_specs(b_out, DA, TM_FFN)
        mix_specs = a_specs + b_specs + [const((D_MODEL, D_MODEL))]
        mix_args = a_args + b_args + [out_w]
        mixer = "proj"
    else:
        g1, pw, ps = pool
        r = TM_FFN // HALO
        last = N_TOK // HALO - 1
        x_specs = [pl.BlockSpec((TM_FFN, D_MODEL), lambda i: (t0 + i, 0))]
        x_args = [x]
        mix_specs = [
            pl.BlockSpec((HALO, D_MODEL), lambda i: (jnp.maximum((t0 + i) * r - 1, 0), 0)),
            pl.BlockSpec((HALO, D_MODEL), lambda i: (jnp.minimum((t0 + i + 1) * r, last), 0)),
            const((1, D_MODEL)), const((len(POOL_WINDOWS), TM, POOL_K)),
            const((len(POOL_WINDOWS), GP, GP)), const((1, D_MODEL)),
        ]
        mix_args = [x, x, g1, _pool_bands(), pw, ps]
        mixer = "pool"
    return pl.pallas_call(
        functools.partial(_tail_kernel, mixer=mixer, n_x=len(x_args), row_off=row_off),
        grid=(n_rows // TM_FFN,),
        in_specs=x_specs + mix_specs + [
            pl.BlockSpec((1, 1, N_MOD, D_MODEL), lambda i: (l, _cond_of_tile((t0 + i) * per_tile), 0, 0)),
            const((1, D_MODEL)),
            resident((D_MODEL, D_FF)), resident((D_MODEL, D_FF)), resident((D_FF, D_MODEL)),
        ],
        out_specs=pl.BlockSpec((TM_FFN, D_MODEL), lambda i: (i, 0)),
        out_shape=jax.ShapeDtypeStruct((n_rows, D_MODEL), F32),
        compiler_params=pltpu.CompilerParams(
            dimension_semantics=("arbitrary",), vmem_limit_bytes=VMEM_LIMIT),
        name="tail_" + mixer,
    )(*x_args, *mix_args, mods, g2, w1, w3, w2)


def _rope_tables():
    pos = jnp.arange(DEC_SEQ)
    freqs = ROPE_BASE ** (-jnp.arange(ROPE_AXIS_FREQS, dtype=F32) / ROPE_AXIS_FREQS)
    ang_row = (pos // GRID_W).astype(F32)[:, None] * freqs
    ang_col = (pos % GRID_W).astype(F32)[:, None] * freqs
    zeros = jnp.zeros((DEC_SEQ, LANES - ROPE_DIM), F32)
    cos = jnp.concatenate([jnp.cos(ang_row)] * 2 + [jnp.cos(ang_col)] * 2 + [zeros], axis=1)
    sin = jnp.concatenate([-jnp.sin(ang_row), jnp.sin(ang_row), -jnp.sin(ang_col), jnp.sin(ang_col), zeros], axis=1)
    return cos, sin


def _slab(v):
    return jnp.pad(v, (0, LANES - v.shape[0]))[None]


def _even_params(i, in_w, out_w, shift_mu, w0, w_up, a0, a_up, g_up, k_k, k_a, r_k, ln_g, ln_b,
                 cq_g, uq_w, ckv_g, ukv_w, qn_g, qr_g, kn_g, kr_g):
    row = lambda v: v[None]
    in_w_pad = jnp.pad(in_w[i], ((0, 0), (0, MLA_PAD - MLA_IN))).astype(BF16)
    mu = shift_mu[i]
    w_up_pad = jnp.pad(w_up[i], ((0, 0), (0, A_LORA), (0, 0)))
    a_up_pad = jnp.pad(a_up[i], ((0, 0), (W_LORA, 0), (0, 0)))
    uq = uq_w[i].reshape(Q_LORA, H_B, NOPE + ROPE_DIM)
    uq = jnp.pad(uq, ((0, 0), (0, 0), (0, 2 * LANES - NOPE - ROPE_DIM))).reshape(Q_LORA, 2 * H_B * LANES)
    ukv = ukv_w[i].reshape(KV_LORA, H_B, NOPE + V_DIM)
    ukv = jnp.concatenate([ukv[:, :, :NOPE].reshape(KV_LORA, -1), ukv[:, :, NOPE:].reshape(KV_LORA, -1)], axis=1)
    rw = dict(mu_r=row(mu[:DA]), mu_k=row(mu[DA:2 * DA]), mu_v=row(mu[2 * DA:3 * DA]), mu_l=row(mu[3 * DA:]),
              w0=w0[i], w_up=w_up_pad, a0=a0[i], a_up=a_up_pad, g_up=g_up[i],
              k_k=row(k_k[i]), k_a=row(k_a[i]), r_k=row(r_k[i].reshape(DA)), ln_g=row(ln_g[i]), ln_b=row(ln_b[i]))
    mla = dict(cq_g=row(cq_g[i]), uq_w=uq.astype(BF16), qn_g=row(qn_g[i]), qr_g=_slab(qr_g[i]),
               ckv_g=row(ckv_g[i]), ukv_w=ukv.astype(BF16), kn_g=row(kn_g[i]), kr_g=_slab(kr_g[i]))
    return in_w_pad, out_w[i].astype(BF16), rw, mla


def kernel(x_prompt, x_sample, c, state_rwkv, cache_mla_ckv, cache_mla_krope, c_ctx,
           norm1_g, norm2_g, ada_w, ada_b, ffn_w1, ffn_w3, ffn_w2,
           in_w, out_w, shift_mu, rwkv_w0, rwkv_w_up, rwkv_a0, rwkv_a_up, rwkv_g_up,
           rwkv_k_k, rwkv_k_a, rwkv_r_k, rwkv_ln_g, rwkv_ln_b,
           mla_cq_g, mla_uq_w, mla_ckv_g, mla_ukv_w, mla_qn_g, mla_qr_g, mla_kn_g, mla_kr_g,
           pool_w, pool_scale):
    x = (x_prompt.reshape(N_CTX, D_MODEL), x_sample.reshape(N_LAT, D_MODEL))
    conds = jnp.concatenate([c_ctx[None], c, jnp.zeros((8 - 1 - DEC_BATCH, D_MODEL), F32)], axis=0)
    mods = _adaln(conds, ada_w, ada_b).reshape(DEPTH, 8, N_MOD, D_MODEL)
    rope_tabs = _rope_tables()
    w1_all, w3_all, w2_all = ffn_w1.astype(BF16), ffn_w3.astype(BF16), ffn_w2.astype(BF16)

    new_s, new_ckv, new_kr = [], [], []
    for l in range(DEPTH):
        n1 = norm1_g[l][None]
        if l % 2 == 0:
            i = l // 2
            in_w_pad, out_w_b, rw, mla = _even_params(
                i, in_w, out_w, shift_mu, rwkv_w0, rwkv_w_up, rwkv_a0, rwkv_a_up, rwkv_g_up,
                rwkv_k_k, rwkv_k_a, rwkv_r_k, rwkv_ln_g, rwkv_ln_b,
                mla_cq_g, mla_uq_w, mla_ckv_g, mla_ukv_w, mla_qn_g, mla_qr_g, mla_kn_g, mla_kr_g)
            proj_r, q_all, k_all, v_all, ckv_all, kr_all = _inproj(x, mods, l, n1, in_w_pad, mla, rope_tabs)
            a_ctx, s_ctx = _rwkv(proj_r, rw, None, i, seq_len=SEQ, n_seq=BATCH, row_off=0)
            a_lat, _ = _rwkv(proj_r, rw, state_rwkv, i, seq_len=DEC_SEQ, n_seq=DEC_BATCH, row_off=N_CTX)
            kr_cache = jnp.pad(cache_mla_krope[:, i].reshape(DEC_BATCH * PAST_LEN, ROPE_DIM),
                               ((0, 0), (0, LANES - ROPE_DIM)))
            k_p, v_p = _mla_cache_expand(cache_mla_ckv[:, i].reshape(DEC_BATCH * PAST_LEN, KV_LORA), kr_cache, mla)
            b_ctx = _attention(q_all, k_all, v_all, n_batch=BATCH, seq_len=SEQ, row_off=0)
            b_lat = _attention(q_all, k_all, v_all, k_p, v_p, n_batch=DEC_BATCH, seq_len=DEC_SEQ, row_off=N_CTX)
            mixer = dict(proj=((a_ctx, a_lat), (b_ctx, b_lat), out_w_b))
            new_s.append(s_ctx)
            new_ckv.append(ckv_all[:N_CTX].reshape(BATCH, SEQ, KV_LORA))
            new_kr.append(kr_all[:N_CTX].reshape(BATCH, SEQ, ROPE_DIM))
        else:
            j = l // 2
            mixer = dict(pool=(n1, pool_w[j].astype(BF16), pool_scale[j][None]))
        tail_args = (mods, l, norm2_g[l][None], w1_all, w3_all, w2_all)
        if l < DEPTH - 1:
            x = _layer_tail(x, *tail_args, **mixer)
        else:
            y_p = _layer_tail(x, *tail_args, **mixer, row_off=0, n_rows=N_CTX).reshape(BATCH, SEQ, D_MODEL)
            y_s = _layer_tail(x, *tail_args, **mixer, row_off=N_CTX, n_rows=N_LAT).reshape(DEC_BATCH, DEC_SEQ, D_MODEL)
    return (y_p, y_s, jnp.stack(new_s, axis=1), jnp.stack(new_ckv, axis=1), jnp.stack(new_kr, axis=1))
```

```python
import functools

import jax
import jax.numpy as jnp
from jax import lax
from jax.experimental import pallas as pl
from jax.experimental.pallas import tpu as pltpu

D_MODEL = 1024
BATCH, SEQ = 32, 256
DEC_BATCH, DEC_SEQ = 2, 2048
DEPTH = 4
PAST_LEN = 256
GRID_W = 64
N_MOD = 6
D_FF = 2816
DA = 512
HD_A = 64
H_A = 8
W_LORA, A_LORA, G_LORA = 64, 64, 128
H_B = 4
NOPE, ROPE_DIM, V_DIM = 128, 64, 128
Q_LORA, KV_LORA = 384, 256
ATTN_SCALE = (NOPE + ROPE_DIM) ** -0.5
ROPE_AXIS_FREQS = ROPE_DIM // 4
ROPE_BASE = 10000.0
RWKV_IN = 3 * DA + W_LORA + A_LORA + G_LORA
MLA_IN = Q_LORA + KV_LORA + ROPE_DIM
MLA_PAD = 768
POOL_WINDOWS = (2, 4, 8, 16)
GP = 256
EPS = 1e-6
GN_EPS = 64e-5

N_CTX = BATCH * SEQ
N_LAT = DEC_BATCH * DEC_SEQ
N_TOK = N_CTX + N_LAT

LANES = 128
TM = 256
N_TILES = N_TOK // TM
CTX_TILES = N_CTX // TM
LAT_TILES_PER_SEQ = DEC_SEQ // TM
CHUNK = 64
CHUNKS_PER_TRIP = 8
LONG_SEQS_PER_STEP = 2
SHORT_SEQ_TRIP = 16
ATTN_SEQS_PER_STEP = 4
HALO = 8
POOL_K = 384
TM_FFN = 512
TM_IN = 512
MXU_WIDTH = 256
FF_SPLITS = ((0, 6 * MXU_WIDTH), (6 * MXU_WIDTH, D_FF))
VMEM_LIMIT = 56 * 1024 * 1024

F32 = jnp.float32
BF16 = jnp.bfloat16


def _cond_of_tile(t):
    return jnp.where(t < CTX_TILES, 0, 1 + (t - CTX_TILES) // LAT_TILES_PER_SEQ)


def _dot(a, b, dims=((1,), (0,))):
    return lax.dot_general(a, b, (dims, ((), ())), preferred_element_type=F32)


_NT = ((1,), (1,))


def _split2(x):
    hi = x.astype(BF16)
    lo = (x - hi.astype(F32)).astype(BF16)
    return hi, lo


def _dot1(a, b, dims=((1,), (0,))):
    return _dot(a.astype(BF16), b.astype(BF16), dims)


def _dot3(a, b, dims=((1,), (0,))):
    ah, al = _split2(a)
    bh, bl = _split2(b)
    return _dot(ah, bh, dims) + (_dot(ah, bl, dims) + _dot(al, bh, dims))


def _dot_exact_lhs(a_bf16, b):
    b0, b1 = _split2(b)
    return _dot(a_bf16, b0) + _dot(a_bf16, b1)


def _rms(x, g):
    return x * lax.rsqrt(jnp.mean(x * x, axis=-1, keepdims=True) + EPS) * g


def _sigmoid(x):
    return 1.0 / (1.0 + jnp.exp(-x))


def _silu(x):
    return x * _sigmoid(x)


def _softplus(x):
    return jnp.maximum(x, 0.0) + jnp.log(1.0 + jnp.exp(-jnp.abs(x)))


def _group_ones(n, g):
    i = lax.broadcasted_iota(jnp.int32, (n, n), 0) // g
    j = lax.broadcasted_iota(jnp.int32, (n, n), 1) // g
    return (i == j).astype(BF16)


def _gsum(x, ones):
    hi, lo = _split2(x)
    return _dot(hi, ones) + _dot(lo, ones)


def _adaln_kernel(c_ref, w_ref, b_ref, o_ref):
    o_ref[0] = _dot3(_silu(c_ref[...]), w_ref[0]) + b_ref[0]


def _adaln(conds, ada_w, ada_b):
    tn = 1536
    return pl.pallas_call(
        _adaln_kernel,
        grid=(DEPTH, N_MOD * D_MODEL // tn),
        in_specs=[
            pl.BlockSpec((8, D_MODEL), lambda l, j: (0, 0)),
            pl.BlockSpec((1, D_MODEL, tn), lambda l, j: (l, 0, j)),
            pl.BlockSpec((1, 1, tn), lambda l, j: (l, 0, j)),
        ],
        out_specs=pl.BlockSpec((1, 8, tn), lambda l, j: (l, 0, j)),
        out_shape=jax.ShapeDtypeStruct((DEPTH, 8, N_MOD * D_MODEL), F32),
        compiler_params=pltpu.CompilerParams(
            dimension_semantics=("arbitrary", "arbitrary"), vmem_limit_bytes=VMEM_LIMIT),
        name="adaln",
    )(conds, ada_w, ada_b.reshape(DEPTH, 1, N_MOD * D_MODEL))


def _mod_spec(l):
    return pl.BlockSpec((1, 1, N_MOD, D_MODEL), lambda t, *_: (l, _cond_of_tile(t), 0, 0))


def _row_specs(x, width, tm=TM):
    if not isinstance(x, tuple):
        return [pl.BlockSpec((tm, width), lambda t, *_: (t, 0))], [x]
    ctx_tiles = N_CTX // tm
    return [pl.BlockSpec((tm, width), lambda t, *_: (jnp.minimum(t, ctx_tiles - 1), 0)),
            pl.BlockSpec((tm, width), lambda t, *_: (jnp.maximum(t - ctx_tiles, 0), 0))], list(x)


def _row_tile(refs, tm=TM):
    if len(refs) == 1:
        return refs[0][...]
    return jnp.where(pl.program_id(0) < N_CTX // tm, refs[0][...], refs[1][...])


def _inproj_kernel(*refs, n_x):
    x_refs, (mod_ref, g_ref, w_ref, *mla_refs) = refs[:n_x], refs[n_x:]
    *mla_in, or_ref, q_ref, k_ref, v_ref, ckv_ref, kr_ref = mla_refs
    *mla_w, cos_ref, sin_ref = mla_in
    m = mod_ref[0, 0]
    x = _row_tile(x_refs, TM_IN)
    is_lat = pl.program_id(0) >= N_CTX // TM_IN
    halves = [pl.ds(k * TM, TM) for k in range(TM_IN // TM)]
    ps = [_dot((_rms(x[k * TM:(k + 1) * TM], g_ref[...]) * (1.0 + m[1:2]) + m[0:1]).astype(BF16), w_ref[...])
          for k in range(TM_IN // TM)]
    for rows, p in zip(halves, ps):
        or_ref[rows, :] = p[:, :RWKV_IN]
        _mla_project(p[:, RWKV_IN:], is_lat, *mla_w, cos_ref.at[rows], sin_ref.at[rows],
                     *(r.at[rows] for r in (q_ref, k_ref, v_ref, ckv_ref, kr_ref)))


def _inproj(x, mods, l, g, w, p, rope_tabs):
    x_specs, x_args = _row_specs(x, D_MODEL, TM_IN)
    per_tile = TM_IN // TM
    ctx_tiles, lat_tiles_per_seq = N_CTX // TM_IN, DEC_SEQ // TM_IN
    full = lambda shape: pl.BlockSpec(shape, lambda t: tuple(0 for _ in shape))
    rope_spec = pl.BlockSpec((TM_IN, LANES), lambda t: (jnp.maximum(t - ctx_tiles, 0) % lat_tiles_per_seq, 0))
    rows = lambda w_: pl.BlockSpec((TM_IN, w_), lambda t: (t, 0))
    return pl.pallas_call(
        functools.partial(_inproj_kernel, n_x=len(x_args)),
        grid=(N_TOK // TM_IN,),
        in_specs=x_specs + [
            pl.BlockSpec((1, 1, N_MOD, D_MODEL), lambda t: (l, _cond_of_tile(t * per_tile), 0, 0)),
            full((1, D_MODEL)),
            full((D_MODEL, RWKV_IN + MLA_PAD)),
            full((1, Q_LORA)), full((Q_LORA, 2 * H_B * LANES)), full((1, LANES)), full((1, LANES)),
            full((1, KV_LORA)), full((KV_LORA, 2 * H_B * NOPE)), full((1, LANES)), full((1, LANES)),
            rope_spec, rope_spec,
        ],
        out_specs=[rows(RWKV_IN), rows(2 * H_B * LANES), rows(2 * H_B * LANES), rows(H_B * V_DIM),
                   rows(KV_LORA), rows(ROPE_DIM)],
        out_shape=[
            jax.ShapeDtypeStruct((N_TOK, RWKV_IN), F32),
            jax.ShapeDtypeStruct((N_TOK, 2 * H_B * LANES), BF16),
            jax.ShapeDtypeStruct((N_TOK, 2 * H_B * LANES), BF16),
            jax.ShapeDtypeStruct((N_TOK, H_B * V_DIM), BF16),
            jax.ShapeDtypeStruct((N_TOK, KV_LORA), F32),
            jax.ShapeDtypeStruct((N_TOK, ROPE_DIM), F32),
        ],
        compiler_params=pltpu.CompilerParams(
            dimension_semantics=("arbitrary",), vmem_limit_bytes=VMEM_LIMIT),
        name="inproj",
    )(*x_args, mods, g, w, p["cq_g"], p["uq_w"], p["qn_g"], p["qr_g"], p["ckv_g"], p["ukv_w"], p["kn_g"],
      p["kr_g"], *rope_tabs)


def _rwkv_kernel(pr_ref, pk_ref, pv_ref, pl_ref, mur_ref, muk_ref, muv_ref, mul_ref,
                 w0_ref, wup_ref, a0_ref, aup_ref, gup_ref, kkg_ref, kag_ref, rkg_ref,
                 lng_ref, lnb_ref, *rest, seq_len, n_seqs, trip, zero_init):
    if zero_init:
        h0_ref = None
        out_ref, hfin_ref, m_s, n_s, yq_s, y0_s, g_s, bonus_s = rest
    else:
        h0_ref, out_ref, hfin_ref, m_s, n_s, yq_s, y0_s, g_s, bonus_s = rest
    C = CHUNK
    n_chunks = seq_len // C
    all_chunks = n_seqs * n_chunks
    static_trip = all_chunks == trip
    lane = lax.broadcasted_iota(jnp.int32, (1, LANES), 1)
    m0 = (lane < HD_A).astype(F32)
    m1 = 1.0 - m0
    ones_g = _group_ones(LANES, HD_A)
    row = lax.broadcasted_iota(jnp.int32, (C, 1), 0)
    i2 = lax.broadcasted_iota(jnp.int32, (2 * C, 2 * C), 0)
    j2 = lax.broadcasted_iota(jnp.int32, (2 * C, 2 * C), 1)
    same_head = (i2 // C) == (j2 // C)
    eye2 = (i2 == j2).astype(F32)
    ic = lax.broadcasted_iota(jnp.int32, (C, C), 0)
    jc = lax.broadcasted_iota(jnp.int32, (C, C), 1)
    incl_c = ((jc <= ic).astype(BF16), (jc >= ic).astype(BF16))
    incl_blk = (same_head & (j2 <= i2), same_head & (j2 >= i2))
    strict_blk = (same_head & (j2 < i2), same_head & (j2 > i2))

    def pair_masks(d):
        masks = []
        s = 1
        while s < C:
            later, earlier = ((i2 // s) % 2, (j2 // s) % 2) if d == 0 else ((j2 // s) % 2, (i2 // s) % 2)
            masks.append(((i2 // (2 * s)) == (j2 // (2 * s))) & (later == 1) & (earlier == 0))
            s *= 2
        return masks

    pair_blk = (pair_masks(0), pair_masks(1))

    def stack(x):
        return jnp.concatenate([x * m0, x * m1], axis=0)

    def chunk_rows(c):
        return pl.ds(c * C, C) if isinstance(c, int) else pl.ds(pl.multiple_of(c * C, C), C)

    def shifted(ref, mu, c):
        x = ref[chunk_rows(c), :]
        if isinstance(c, int):
            first, last = c % n_chunks == 0, c % n_chunks == n_chunks - 1
            prev_row = jnp.zeros((1, x.shape[1]), F32) if first else ref[c * C - 1:c * C, :]
            next_row = jnp.zeros((1, x.shape[1]), F32) if last else ref[(c + 1) * C:(c + 1) * C + 1, :]
        else:
            pstart = pl.multiple_of(jnp.maximum(c * C - 8, 0), 8)
            nstart = pl.multiple_of(jnp.minimum((c + 1) * C, n_seqs * seq_len - 8), 8)
            prev_row = jnp.where(c % n_chunks > 0, ref[pl.ds(pstart, 8), :][7:8], 0.0)
            next_row = jnp.where(c % n_chunks < n_chunks - 1, ref[pl.ds(nstart, 8), :][0:1], 0.0)
        prev = jnp.where(row == 0, prev_row, pltpu.roll(x, 1, 0))
        nxt = jnp.where(row == C - 1, next_row, pltpu.roll(x, C - 1, 0))
        return x + mu * (0.5 * (prev + nxt) - x)

    def chain_mats(chains):
        ds = [ch[0] for ch in chains]
        lp = [_dot_exact_lhs(incl_c[d], logw) for d, *_, logw in chains]
        lpc = [x[C - 1:C] if d == 0 else x[0:1] for d, x in zip(ds, lp)]
        st = []
        for (d, r, kd, v, kk, b, logw), x, xc in zip(chains, lp, lpc):
            p_in, p_ex, p_inv, p_end = jnp.exp(x), jnp.exp(x - logw), jnp.exp(-x), jnp.exp(xc - x)
            st.append(dict(qs=stack(r * p_in), a_s=stack(kk * p_ex), khs=stack(kd * p_inv),
                           bs=stack(b * p_inv), kbs=stack(kd * p_end), bbs=stack(b * p_end), vs=stack(v)))
        R = 2 * C
        quad = [_dot1(jnp.concatenate([s["a_s"], s["qs"]], axis=0),
                      jnp.concatenate([s["bs"], s["khs"]], axis=0), _NT) for s in st]
        lab = [jnp.where(strict_blk[d], q[:R, :R], 0.0) for d, q in zip(ds, quad)]
        lak = [jnp.where(strict_blk[d], q[:R, R:], 0.0) for d, q in zip(ds, quad)]
        dqb = [jnp.where(incl_blk[d], q[R:, :R], 0.0) for d, q in zip(ds, quad)]
        dqk = [jnp.where(incl_blk[d], q[R:, R:], 0.0) for d, q in zip(ds, quad)]
        x2 = [_dot1(l, s["vs"]) for l, s in zip(lak, st)]
        tinv = [eye2 - jnp.where(pair_blk[d][0], l, 0.0) for d, l in zip(ds, lab)]
        for lvl in range(1, len(pair_blk[0])):
            half = [_dot1(t, jnp.where(pair_blk[d][lvl], l, 0.0)) for d, t, l in zip(ds, tinv, lab)]
            tinv = [t - _dot1(h, t) for h, t in zip(half, tinv)]
        w12 = [_dot1(t, jnp.concatenate([s["a_s"], x], axis=1)) for t, s, x in zip(tinv, st, x2)]
        zero_blk = jnp.zeros((R, R), F32)
        fin = [_dot1(jnp.concatenate([jnp.concatenate([s["bbs"].T, -s["kbs"].T], axis=1),
                                      jnp.concatenate([qb, -qk], axis=1)], axis=0),
                     jnp.concatenate([w, jnp.concatenate([zero_blk, s["vs"]], axis=1)], axis=0))
               for s, qb, qk, w in zip(st, dqb, dqk, w12)]
        fold = lambda x: x[:C] + x[C:]
        return [tuple(fold(x) for x in (eye2 * jnp.exp(xc) - f[:R, :R], -f[:R, R:], s["qs"] - f[R:, :R], -f[R:, R:]))
                for xc, f, s in zip(lpc, fin, st)]

    def load_chunk(c):
        return (shifted(pr_ref, mur_ref[...], c), shifted(pk_ref, muk_ref[...], c),
                shifted(pv_ref, muv_ref[...], c), shifted(pl_ref, mul_ref[...], c))

    def compute_chunks(loaded):
        rows_of = lambda j: slice(j * C, (j + 1) * C)
        r_all, k_all, v_all, lo_all = (jnp.concatenate([ch[j] for ch in loaded], axis=0) for j in range(4))
        wa = lo_all[:, :LANES]
        gate_all = _dot1(_sigmoid(lo_all[:, LANES:]), gup_ref[...])
        kkr = k_all * kkg_ref[...]
        kk_all = kkr / (jnp.sqrt(_gsum(kkr * kkr, ones_g)) + 1e-12)
        bonus_all = _gsum(r_all * k_all * rkg_ref[...], ones_g) * v_all
        both = lambda ref: jnp.concatenate([ref[0], ref[1]], axis=1)
        both_rows = lambda ref: jnp.concatenate([ref[0:1, :], ref[1:2, :]], axis=1)
        wl = both_rows(w0_ref) + _dot3(jnp.tanh(wa), both(wup_ref))
        logw_all = -jnp.exp(-_softplus(-wl) - 0.5)
        a_all = _sigmoid(both_rows(a0_ref) + _dot3(wa, both(aup_ref)))
        chains = []
        for j, (r, k, v, _) in enumerate(loaded):
            kk = kk_all[rows_of(j)]
            for d in range(2):
                a = a_all[rows_of(j), d * LANES:(d + 1) * LANES]
                logw = logw_all[rows_of(j), d * LANES:(d + 1) * LANES]
                chains.append((d, r, k * (1.0 + (a - 1.0) * kag_ref[...]), v, kk, kk * a, logw))
        mats = chain_mats(chains)
        return [(gate_all[rows_of(j)], bonus_all[rows_of(j)], mats[2 * j:2 * j + 2]) for j in range(len(loaded))]

    def store_chunk(c, gate, bonus, mats):
        g_s[chunk_rows(c), :] = gate
        bonus_s[chunk_rows(c), :] = bonus
        for d in range(2):
            m_s[d, c], n_s[d, c], yq_s[d, c], y0_s[d, c] = mats[d]

    def phase_a_trip(i, carry):
        chunks = [i * trip + u for u in range(trip)]
        results = compute_chunks([load_chunk(c) for c in chunks])
        for c, res in zip(chunks, results):
            store_chunk(c, *res)
        return carry

    if static_trip:
        phase_a_trip(0, 0)
    else:
        lax.fori_loop(0, all_chunks // trip, phase_a_trip, 0)

    def phase_b(i, states):
        first_zero = zero_init and isinstance(i, int) and i == 0
        units = [(q, d, q * n_chunks + (i if d == 0 else n_chunks - 1 - i))
                 for q in range(n_seqs) for d in range(2)]
        if first_zero:
            return tuple(stack(n_s[d, c]) for _, d, c in units)
        new_states = tuple(stack(_dot3(m_s[d, c], states[2 * q + d]) + n_s[d, c]) for q, d, c in units)
        for q, d, c in units:
            y0_s[d, c] = y0_s[d, c] + _dot1(yq_s[d, c], states[2 * q + d])
        return new_states

    zeros_half = jnp.zeros((HD_A, HD_A), F32)
    states = []
    for q in range(n_seqs):
        for d in range(2):
            if zero_init:
                states.append(jnp.zeros((LANES, LANES), F32))
            else:
                top = jnp.concatenate([h0_ref[q, 0, d, 0], zeros_half], axis=1)
                bot = jnp.concatenate([zeros_half, h0_ref[q, 0, d, 1]], axis=1)
                states.append(jnp.concatenate([top, bot], axis=0).T)
    states = tuple(states)
    if static_trip:
        for i in range(n_chunks):
            states = phase_b(i, states)
    else:
        states = lax.fori_loop(0, n_chunks, phase_b, states)
    for q in range(n_seqs):
        for d in range(2):
            s_vk = states[2 * q + d].T
            hfin_ref[q, d, 0] = s_vk[:HD_A, :HD_A]
            hfin_ref[q, d, 1] = s_vk[HD_A:, HD_A:]

    def phase_c_trip(i, carry):
        chunks = [i * trip + u for u in range(trip)]
        ys = [y0_s[0, c] + y0_s[1, c] for c in chunks]
        extras = [(bonus_s[chunk_rows(c), :], g_s[chunk_rows(c), :]) for c in chunks]
        y_all = jnp.concatenate(ys, axis=0)
        yc = y_all - _gsum(y_all, ones_g) * (1.0 / HD_A)
        var = _gsum(yc * yc, ones_g) * (1.0 / HD_A)
        yn = yc * lax.rsqrt(var + GN_EPS) * lng_ref[...] + lnb_ref[...]
        for j, (c, (bonus, gate)) in enumerate(zip(chunks, extras)):
            out_ref[chunk_rows(c), :] = ((yn[j * C:(j + 1) * C] + bonus) * gate).astype(out_ref.dtype)
        return carry

    if static_trip:
        phase_c_trip(0, 0)
    else:
        lax.fori_loop(0, all_chunks // trip, phase_c_trip, 0)


def _rwkv(proj_r, p, h0, layer, *, seq_len, n_seq, row_off):
    zero_init = h0 is None
    trip = SHORT_SEQ_TRIP if seq_len // CHUNK < CHUNKS_PER_TRIP else CHUNKS_PER_TRIP
    seqs = max(min(LONG_SEQS_PER_STEP, n_seq), trip // (seq_len // CHUNK))
    blk_rows = seqs * seq_len
    n_chunks = blk_rows // CHUNK
    off = row_off // blk_rows
    n_pairs = DA // LANES
    col = lambda cb: (lambda s, hp: (off + s, cb + hp))
    vec = lambda: pl.BlockSpec((1, LANES), lambda s, hp: (0, hp))
    in_specs = [
        pl.BlockSpec((blk_rows, LANES), col(0)),
        pl.BlockSpec((blk_rows, LANES), col(n_pairs)),
        pl.BlockSpec((blk_rows, LANES), col(2 * n_pairs)),
        pl.BlockSpec((blk_rows, 2 * LANES), lambda s, hp: (off + s, 3 * n_pairs // 2)),
        vec(), vec(), vec(),
        pl.BlockSpec((1, 2 * LANES), lambda s, hp: (0, 0)),
        pl.BlockSpec((2, LANES), lambda s, hp: (0, hp)),
        pl.BlockSpec((2, LANES, LANES), lambda s, hp: (0, 0, hp)),
        pl.BlockSpec((2, LANES), lambda s, hp: (0, hp)),
        pl.BlockSpec((2, LANES, LANES), lambda s, hp: (0, 0, hp)),
        pl.BlockSpec((G_LORA, LANES), lambda s, hp: (0, hp)),
        vec(), vec(), vec(), vec(), vec(),
    ]
    args = [proj_r, proj_r, proj_r, proj_r, p["mu_r"], p["mu_k"], p["mu_v"], p["mu_l"],
            p["w0"], p["w_up"], p["a0"], p["a_up"], p["g_up"], p["k_k"], p["k_a"], p["r_k"],
            p["ln_g"], p["ln_b"]]
    if not zero_init:
        in_specs.append(pl.BlockSpec((seqs, 1, 2, 2, HD_A, HD_A), lambda s, hp: (s, layer, 0, hp, 0, 0)))
        args.append(h0)
    blk = (CHUNK, LANES)
    return pl.pallas_call(
        functools.partial(_rwkv_kernel, seq_len=seq_len, n_seqs=seqs, trip=trip, zero_init=zero_init),
        grid=(n_seq // seqs, n_pairs),
        in_specs=in_specs,
        out_specs=[
            pl.BlockSpec((blk_rows, LANES), lambda s, hp: (s, hp)),
            pl.BlockSpec((seqs, 2, 2, HD_A, HD_A), lambda s, hp: (s, 0, hp, 0, 0)),
        ],
        out_shape=[
            jax.ShapeDtypeStruct((n_seq * seq_len, DA), BF16),
            jax.ShapeDtypeStruct((n_seq, 2, H_A, HD_A, HD_A), F32),
        ],
        scratch_shapes=[
            pltpu.VMEM((2, n_chunks) + blk, F32),
            pltpu.VMEM((2, n_chunks) + blk, F32),
            pltpu.VMEM((2, n_chunks) + blk, F32),
            pltpu.VMEM((2, n_chunks) + blk, F32),
            pltpu.VMEM((blk_rows, LANES), F32),
            pltpu.VMEM((blk_rows, LANES), F32),
        ],
        compiler_params=pltpu.CompilerParams(
            dimension_semantics=("arbitrary", "arbitrary"), vmem_limit_bytes=VMEM_LIMIT),
        name="rwkv_ctx" if zero_init else "rwkv_lat",
    )(*args)


def _rope(x, cos, sin):
    lane = lax.broadcasted_iota(jnp.int32, x.shape, 1)
    swapped = jnp.where((lane % 32) < 16, pltpu.roll(x, LANES - 16, 1), pltpu.roll(x, 16, 1))
    return x * cos + swapped * sin


def _slab_rms(x, g):
    return x * lax.rsqrt(jnp.sum(x * x, axis=-1, keepdims=True) * (1.0 / ROPE_DIM) + EPS) * g


def _kv_expand(ckv_n, kr_att, ukv_ref, kng_ref, k_ref, v_ref):
    kv = _dot(ckv_n.astype(BF16), ukv_ref[...])
    kr_b = kr_att.astype(BF16)
    for h in range(H_B):
        kn = _rms(kv[:, h * NOPE:(h + 1) * NOPE], kng_ref[...])
        k_ref[:, 2 * h * LANES:(2 * h + 1) * LANES] = kn.astype(BF16)
        k_ref[:, (2 * h + 1) * LANES:(2 * h + 2) * LANES] = kr_b
    v_ref[...] = kv[:, H_B * NOPE:].astype(BF16)


def _mla_project(pm, is_lat, cqg_ref, uq_ref, qng_ref, qrg_ref, ckvg_ref, ukv_ref, kng_ref, krg_ref,
                 cos_ref, sin_ref, q_ref, k_ref, v_ref, ckv_ref, kr_ref):
    cq = _rms(pm[:, :Q_LORA], cqg_ref[...])
    ckv_n = _rms(pm[:, Q_LORA:Q_LORA + KV_LORA], ckvg_ref[...])
    q = _dot(cq.astype(BF16), uq_ref[...])
    ckv_ref[...] = ckv_n
    kr = _slab_rms(pm[:, Q_LORA + KV_LORA:], krg_ref[...])
    kr_ref[...] = kr[:, :ROPE_DIM]
    _kv_expand(ckv_n, kr, ukv_ref, kng_ref, k_ref, v_ref)
    qrs = []
    for h in range(H_B):
        qn = _rms(q[:, 2 * h * LANES:(2 * h + 1) * LANES], qng_ref[...])
        qrs.append(_slab_rms(q[:, (2 * h + 1) * LANES:(2 * h + 2) * LANES], qrg_ref[...]))
        q_ref[:, 2 * h * LANES:(2 * h + 1) * LANES] = (qn * ATTN_SCALE).astype(BF16)
        q_ref[:, (2 * h + 1) * LANES:(2 * h + 2) * LANES] = (qrs[h] * ATTN_SCALE).astype(BF16)

    @pl.when(is_lat)
    def _():
        cos, sin = cos_ref[...], sin_ref[...]
        kr_rot = _rope(kr, cos, sin).astype(BF16)
        for h in range(H_B):
            k_ref[:, (2 * h + 1) * LANES:(2 * h + 2) * LANES] = kr_rot
            q_ref[:, (2 * h + 1) * LANES:(2 * h + 2) * LANES] = (_rope(qrs[h], cos, sin) * ATTN_SCALE).astype(BF16)


def _mla_cache_kernel(ckv_ref, kr_ref, ukv_ref, kng_ref, k_ref, v_ref):
    _kv_expand(ckv_ref[...], kr_ref[...], ukv_ref, kng_ref, k_ref, v_ref)


def _mla_cache_expand(ckv, kr_slab, p):
    n = ckv.shape[0]
    full = lambda shape: pl.BlockSpec(shape, lambda t: tuple(0 for _ in shape))
    return pl.pallas_call(
        _mla_cache_kernel,
        grid=(1,),
        in_specs=[full((n, KV_LORA)), full((n, LANES)), full((KV_LORA, 2 * H_B * NOPE)), full((1, LANES))],
        out_specs=[full((n, 2 * H_B * LANES)), full((n, H_B * V_DIM))],
        out_shape=[
            jax.ShapeDtypeStruct((n, 2 * H_B * LANES), BF16),
            jax.ShapeDtypeStruct((n, H_B * V_DIM), BF16),
        ],
        compiler_params=pltpu.CompilerParams(
            dimension_semantics=("arbitrary",), vmem_limit_bytes=VMEM_LIMIT),
        name="mla_cache_expand",
    )(ckv, kr_slab, p["ukv_w"], p["kn_g"])


def _attn_kernel(q_ref, k_ref, v_ref, *rest, with_ctx, seq_len, n_seqs):
    if with_ctx:
        k2_ref, v2_ref, o_ref = rest
    else:
        (o_ref,) = rest
    hs = [slice(2 * h * LANES, (2 * h + 2) * LANES) for h in range(H_B)]
    vs = [slice(h * V_DIM, (h + 1) * V_DIM) for h in range(H_B)]
    units = [(b, h) for b in range(n_seqs) for h in range(H_B)]
    qrows = lambda b: slice(b * TM, (b + 1) * TM)
    krows = lambda b: slice(b * seq_len, (b + 1) * seq_len)

    def keys_values(b, h):
        kv = [(k_ref[krows(b), hs[h]], v_ref[krows(b), vs[h]])]
        if with_ctx:
            kv.append((k2_ref[:, hs[h]], v2_ref[:, vs[h]]))
        return kv

    scores = [[_dot(q_ref[qrows(b), hs[h]], k, _NT) for k, _ in keys_values(b, h)] for b, h in units]
    probs, dens = [], []
    for sc in scores:
        m = functools.reduce(jnp.maximum, [jnp.max(s, axis=-1, keepdims=True) for s in sc])
        pr = [jnp.exp(s - m) for s in sc]
        dens.append(functools.reduce(lambda a, b: a + b, [jnp.sum(p, axis=-1, keepdims=True) for p in pr]))
        probs.append([p.astype(BF16) for p in pr])
    for (b, h), pr, den in zip(units, probs, dens):
        o = functools.reduce(lambda a, c: a + c, [_dot(p, v) for p, (_, v) in zip(pr, keys_values(b, h))])
        o_ref[qrows(b), vs[h]] = (o / den).astype(o_ref.dtype)


def _attention(q, k, v, k2=None, v2=None, *, n_batch, seq_len, row_off):
    with_ctx = k2 is not None
    q_tiles = seq_len // TM
    n_seqs = ATTN_SEQS_PER_STEP if q_tiles == 1 else 1
    kw = 2 * H_B * LANES
    q0 = row_off // (n_seqs * TM)
    k0 = row_off // (n_seqs * seq_len)
    in_specs = [
        pl.BlockSpec((n_seqs * TM, kw), lambda b, t: (q0 + b * q_tiles + t, 0)),
        pl.BlockSpec((n_seqs * seq_len, kw), lambda b, t: (k0 + b, 0)),
        pl.BlockSpec((n_seqs * seq_len, H_B * V_DIM), lambda b, t: (k0 + b, 0)),
    ]
    args = [q, k, v]
    if with_ctx:
        in_specs += [
            pl.BlockSpec((PAST_LEN, kw), lambda b, t: (b, 0)),
            pl.BlockSpec((PAST_LEN, H_B * V_DIM), lambda b, t: (b, 0)),
        ]
        args += [k2, v2]
    return pl.pallas_call(
        functools.partial(_attn_kernel, with_ctx=with_ctx, seq_len=seq_len, n_seqs=n_seqs),
        grid=(n_batch // n_seqs, q_tiles),
        in_specs=in_specs,
        out_specs=pl.BlockSpec((n_seqs * TM, H_B * V_DIM), lambda b, t: (b * q_tiles + t, 0)),
        out_shape=jax.ShapeDtypeStruct((n_batch * seq_len, H_B * V_DIM), BF16),
        compiler_params=pltpu.CompilerParams(
            dimension_semantics=("arbitrary", "arbitrary"), vmem_limit_bytes=VMEM_LIMIT),
        name="attn_lat" if with_ctx else "attn_ctx",
    )(*args)


def _pool_bands():
    i = jnp.arange(TM)[:, None]
    e = jnp.arange(POOL_K)[None, :]
    e = jnp.where(e < TM, e, jnp.where(e < TM + HALO, e - TM - HALO, e - HALO))
    return jnp.stack([(e >= i - w // 2) & (e < i + w // 2) for w in POOL_WINDOWS]).astype(BF16)


def _pool_diffs(h, h_prev, h_next, pos0, seq_len, band_ref):
    ext = jnp.concatenate([h, h_prev, h_next, jnp.zeros((POOL_K - TM - 2 * HALO, D_MODEL), F32)], axis=0)
    pos = pos0 + lax.broadcasted_iota(jnp.int32, (TM, 1), 0)
    sums = [_dot_exact_lhs(band_ref[gi], ext[:, gi * GP:(gi + 1) * GP]) for gi in range(len(POOL_WINDOWS))]
    diffs = []
    for gi, win in enumerate(POOL_WINDOWS):
        half = win // 2
        cnt = (jnp.minimum(pos + half, seq_len) - jnp.maximum(pos - half, 0)).astype(F32)
        diffs.append((sums[gi] / cnt - h[:, gi * GP:(gi + 1) * GP]).astype(BF16))
    return diffs


def _tail_kernel(*refs, mixer, n_x, row_off):
    x_refs, rest = refs[:n_x], refs[n_x:]
    if mixer == "proj":
        a0_ref, a1_ref, b0_ref, b1_ref, ow_ref, mod_ref, g2_ref, w1_ref, w3_ref, w2_ref, o_ref = rest
    else:
        xp_ref, xn_ref, g1_ref, band_ref, pw_ref, ps_ref, mod_ref, g2_ref, w1_ref, w3_ref, w2_ref, o_ref = rest
    m = mod_ref[0, 0]
    x = _row_tile(x_refs, TM_FFN)
    if mixer == "proj":
        mix = (_dot(_row_tile((a0_ref, a1_ref), TM_FFN), ow_ref[:DA, :])
               + _dot(_row_tile((b0_ref, b1_ref), TM_FFN), ow_ref[DA:, :]))
    else:
        row0 = row_off + pl.program_id(0) * TM_FFN
        is_lat = row0 >= N_CTX
        seq_len = jnp.where(is_lat, DEC_SEQ, SEQ)
        norm_mod = lambda v: _rms(v, g1_ref[...]) * (1.0 + m[1:2]) + m[0:1]
        h, hp, hn = norm_mod(x), norm_mod(xp_ref[...]), norm_mod(xn_ref[...])
        parts = []
        n_sub = TM_FFN // TM
        for k in range(n_sub):
            r0 = row0 + k * TM
            pos0 = jnp.where(is_lat, (r0 - N_CTX) % DEC_SEQ, r0 % SEQ)
            prev = hp if k == 0 else h[k * TM - HALO:k * TM]
            nxt = hn if k == n_sub - 1 else h[(k + 1) * TM:(k + 1) * TM + HALO]
            prev = jnp.where(pos0 > 0, prev, 0.0)
            nxt = jnp.where(pos0 + TM < seq_len, nxt, 0.0)
            parts.append(_pool_diffs(h[k * TM:(k + 1) * TM], prev, nxt, pos0, seq_len, band_ref))
        mix = jnp.concatenate(
            [_dot(jnp.concatenate([p[gi] for p in parts], axis=0), pw_ref[gi]) for gi in range(len(POOL_WINDOWS))],
            axis=1) * ps_ref[...]
    x1 = x + m[2:3] * mix
    h2 = (_rms(x1, g2_ref[...]) * (1.0 + m[4:5]) + m[3:4]).astype(BF16)
    ups = [(_dot(h2, w1_ref[0, :, lo:hi]), _dot(h2, w3_ref[0, :, lo:hi])) for lo, hi in FF_SPLITS]
    acts = [(_silu(a) * b).astype(BF16) for a, b in ups]
    acc = functools.reduce(lambda p, q: p + q,
                           [_dot(u, w2_ref[0, lo:hi, :]) for u, (lo, hi) in zip(acts, FF_SPLITS)])
    o_ref[...] = x1 + m[5:6] * acc


def _layer_tail(x, mods, l, g2, w1, w3, w2, *, proj=None, pool=None, row_off=0, n_rows=N_TOK):
    per_tile = TM_FFN // TM
    t0 = row_off // TM_FFN
    const = lambda shape: pl.BlockSpec(shape, lambda i: tuple(0 for _ in shape))
    resident = lambda shape: pl.BlockSpec((1,) + shape, lambda i: (l, 0, 0), pipeline_mode=pl.Buffered(1))
    if proj is not None:
        assert row_off == 0 and n_rows == N_TOK
        a_out, b_out, out_w = proj
        x_specs, x_args = _row_specs(x, D_MODEL, TM_FFN)
        a_specs, a_args = _row_specs(a_out, DA, TM_FFN)
        b_specs, b_args = _row_specs(b_out, DA, TM_FFN)
        mix_specs = a_specs + b_specs + [const((D_MODEL, D_MODEL))]
        mix_args = a_args + b_args + [out_w]
        mixer = "proj"
    else:
        g1, pw, ps = pool
        r = TM_FFN // HALO
        last = N_TOK // HALO - 1
        x_specs = [pl.BlockSpec((TM_FFN, D_MODEL), lambda i: (t0 + i, 0))]
        x_args = [x]
        mix_specs = [
            pl.BlockSpec((HALO, D_MODEL), lambda i: (jnp.maximum((t0 + i) * r - 1, 0), 0)),
            pl.BlockSpec((HALO, D_MODEL), lambda i: (jnp.minimum((t0 + i + 1) * r, last), 0)),
            const((1, D_MODEL)), const((len(POOL_WINDOWS), TM, POOL_K)),
            const((len(POOL_WINDOWS), GP, GP)), const((1, D_MODEL)),
        ]
        mix_args = [x, x, g1, _pool_bands(), pw, ps]
        mixer = "pool"
    return pl.pallas_call(
        functools.partial(_tail_kernel, mixer=mixer, n_x=len(x_args), row_off=row_off),
        grid=(n_rows // TM_FFN,),
        in_specs=x_specs + mix_specs + [
            pl.BlockSpec((1, 1, N_MOD, D_MODEL), lambda i: (l, _cond_of_tile((t0 + i) * per_tile), 0, 0)),
            const((1, D_MODEL)),
            resident((D_MODEL, D_FF)), resident((D_MODEL, D_FF)), resident((D_FF, D_MODEL)),
        ],
        out_specs=pl.BlockSpec((TM_FFN, D_MODEL), lambda i: (i, 0)),
        out_shape=jax.ShapeDtypeStruct((n_rows, D_MODEL), F32),
        compiler_params=pltpu.CompilerParams(
            dimension_semantics=("arbitrary",), vmem_limit_bytes=VMEM_LIMIT),
        name="tail_" + mixer,
    )(*x_args, *mix_args, mods, g2, w1, w3, w2)


def _rope_tables():
    pos = jnp.arange(DEC_SEQ)
    freqs = ROPE_BASE ** (-jnp.arange(ROPE_AXIS_FREQS, dtype=F32) / ROPE_AXIS_FREQS)
    ang_row = (pos // GRID_W).astype(F32)[:, None] * freqs
    ang_col = (pos % GRID_W).astype(F32)[:, None] * freqs
    zeros = jnp.zeros((DEC_SEQ, LANES - ROPE_DIM), F32)
    cos = jnp.concatenate([jnp.cos(ang_row)] * 2 + [jnp.cos(ang_col)] * 2 + [zeros], axis=1)
    sin = jnp.concatenate([-jnp.sin(ang_row), jnp.sin(ang_row), -jnp.sin(ang_col), jnp.sin(ang_col), zeros], axis=1)
    return cos, sin


def _slab(v):
    return jnp.pad(v, (0, LANES - v.shape[0]))[None]


def _even_params(i, in_w, out_w, shift_mu, w0, w_up, a0, a_up, g_up, k_k, k_a, r_k, ln_g, ln_b,
                 cq_g, uq_w, ckv_g, ukv_w, qn_g, qr_g, kn_g, kr_g):
    row = lambda v: v[None]
    in_w_pad = jnp.pad(in_w[i], ((0, 0), (0, MLA_PAD - MLA_IN))).astype(BF16)
    mu = shift_mu[i]
    w_up_pad = jnp.pad(w_up[i], ((0, 0), (0, A_LORA), (0, 0)))
    a_up_pad = jnp.pad(a_up[i], ((0, 0), (W_LORA, 0), (0, 0)))
    uq = uq_w[i].reshape(Q_LORA, H_B, NOPE + ROPE_DIM)
    uq = jnp.pad(uq, ((0, 0), (0, 0), (0, 2 * LANES - NOPE - ROPE_DIM))).reshape(Q_LORA, 2 * H_B * LANES)
    ukv = ukv_w[i].reshape(KV_LORA, H_B, NOPE + V_DIM)
    ukv = jnp.concatenate([ukv[:, :, :NOPE].reshape(KV_LORA, -1), ukv[:, :, NOPE:].reshape(KV_LORA, -1)], axis=1)
    rw = dict(mu_r=row(mu[:DA]), mu_k=row(mu[DA:2 * DA]), mu_v=row(mu[2 * DA:3 * DA]), mu_l=row(mu[3 * DA:]),
              w0=w0[i], w_up=w_up_pad, a0=a0[i], a_up=a_up_pad, g_up=g_up[i],
              k_k=row(k_k[i]), k_a=row(k_a[i]), r_k=row(r_k[i].reshape(DA)), ln_g=row(ln_g[i]), ln_b=row(ln_b[i]))
    mla = dict(cq_g=row(cq_g[i]), uq_w=uq.astype(BF16), qn_g=row(qn_g[i]), qr_g=_slab(qr_g[i]),
               ckv_g=row(ckv_g[i]), ukv_w=ukv.astype(BF16), kn_g=row(kn_g[i]), kr_g=_slab(kr_g[i]))
    return in_w_pad, out_w[i].astype(BF16), rw, mla


def kernel(x_prompt, x_sample, c, state_rwkv, cache_mla_ckv, cache_mla_krope, c_ctx,
           norm1_g, norm2_g, ada_w, ada_b, ffn_w1, ffn_w3, ffn_w2,
           in_w, out_w, shift_mu, rwkv_w0, rwkv_w_up, rwkv_a0, rwkv_a_up, rwkv_g_up,
           rwkv_k_k, rwkv_k_a, rwkv_r_k, rwkv_ln_g, rwkv_ln_b,
           mla_cq_g, mla_uq_w, mla_ckv_g, mla_ukv_w, mla_qn_g, mla_qr_g, mla_kn_g, mla_kr_g,
           pool_w, pool_scale):
    x = (x_prompt.reshape(N_CTX, D_MODEL), x_sample.reshape(N_LAT, D_MODEL))
    conds = jnp.concatenate([c_ctx[None], c, jnp.zeros((8 - 1 - DEC_BATCH, D_MODEL), F32)], axis=0)
    mods = _adaln(conds, ada_w, ada_b).reshape(DEPTH, 8, N_MOD, D_MODEL)
    rope_tabs = _rope_tables()
    w1_all, w3_all, w2_all = ffn_w1.astype(BF16), ffn_w3.astype(BF16), ffn_w2.astype(BF16)

    new_s, new_ckv, new_kr = [], [], []
    for l in range(DEPTH):
        n1 = norm1_g[l][None]
        if l % 2 == 0:
            i = l // 2
            in_w_pad, out_w_b, rw, mla = _even_params(
                i, in_w, out_w, shift_mu, rwkv_w0, rwkv_w_up, rwkv_a0, rwkv_a_up, rwkv_g_up,
                rwkv_k_k, rwkv_k_a, rwkv_r_k, rwkv_ln_g, rwkv_ln_b,
                mla_cq_g, mla_uq_w, mla_ckv_g, mla_ukv_w, mla_qn_g, mla_qr_g, mla_kn_g, mla_kr_g)
            proj_r, q_all, k_all, v_all, ckv_all, kr_all = _inproj(x, mods, l, n1, in_w_pad, mla, rope_tabs)
            a_ctx, s_ctx = _rwkv(proj_r, rw, None, i, seq_len=SEQ, n_seq=BATCH, row_off=0)
            a_lat, _ = _rwkv(proj_r, rw, state_rwkv, i, seq_len=DEC_SEQ, n_seq=DEC_BATCH, row_off=N_CTX)
            kr_cache = jnp.pad(cache_mla_krope[:, i].reshape(DEC_BATCH * PAST_LEN, ROPE_DIM),
                               ((0, 0), (0, LANES - ROPE_DIM)))
            k_p, v_p = _mla_cache_expand(cache_mla_ckv[:, i].reshape(DEC_BATCH * PAST_LEN, KV_LORA), kr_cache, mla)
            b_ctx = _attention(q_all, k_all, v_all, n_batch=BATCH, seq_len=SEQ, row_off=0)
            b_lat = _attention(q_all, k_all, v_all, k_p, v_p, n_batch=DEC_BATCH, seq_len=DEC_SEQ, row_off=N_CTX)
            mixer = dict(proj=((a_ctx, a_lat), (b_ctx, b_lat), out_w_b))
            new_s.append(s_ctx)
            new_ckv.append(ckv_all[:N_CTX].reshape(BATCH, SEQ, KV_LORA))
            new_kr.append(kr_all[:N_CTX].reshape(BATCH, SEQ, ROPE_DIM))
        else:
            j = l // 2
            mixer = dict(pool=(n1, pool_w[j].astype(BF16), pool_scale[j][None]))
        tail_args = (mods, l, norm2_g[l][None], w1_all, w3_all, w2_all)
        if l < DEPTH - 1:
            x = _layer_tail(x, *tail_args, **mixer)
        else:
            y_p = _layer_tail(x, *tail_args, **mixer, row_off=0, n_rows=N_CTX).reshape(BATCH, SEQ, D_MODEL)
            y_s = _layer_tail(x, *tail_args, **mixer, row_off=N_CTX, n_rows=N_LAT).reshape(DEC_BATCH, DEC_SEQ, D_MODEL)
    return (y_p, y_s, jnp.stack(new_s, axis=1), jnp.stack(new_ckv, axis=1), jnp.stack(new_kr, axis=1))
```

```python
import functools

import jax
import jax.numpy as jnp
from jax import lax
from jax.experimental import pallas as pl
from jax.experimental.pallas import tpu as pltpu

D_MODEL = 1024
BATCH, SEQ = 32, 256
DEC_BATCH, DEC_SEQ = 2, 2048
DEPTH = 4
PAST_LEN = 256
GRID_W = 64
N_MOD = 6
D_FF = 2816
DA = 512
HD_A = 64
H_A = 8
W_LORA, A_LORA, G_LORA = 64, 64, 128
H_B = 4
NOPE, ROPE_DIM, V_DIM = 128, 64, 128
Q_LORA, KV_LORA = 384, 256
ATTN_SCALE = (NOPE + ROPE_DIM) ** -0.5
ROPE_AXIS_FREQS = ROPE_DIM // 4
ROPE_BASE = 10000.0
RWKV_IN = 3 * DA + W_LORA + A_LORA + G_LORA
MLA_IN = Q_LORA + KV_LORA + ROPE_DIM
MLA_PAD = 768
POOL_WINDOWS = (2, 4, 8, 16)
GP = 256
EPS = 1e-6
GN_EPS = 64e-5

N_CTX = BATCH * SEQ
N_LAT = DEC_BATCH * DEC_SEQ
N_TOK = N_CTX + N_LAT

LANES = 128
TM = 256
N_TILES = N_TOK // TM
CTX_TILES = N_CTX // TM
LAT_TILES_PER_SEQ = DEC_SEQ // TM
CHUNK = 64
CHUNKS_PER_TRIP = 8
LONG_SEQS_PER_STEP = 2
SHORT_SEQ_TRIP = 16
ATTN_SEQS_PER_STEP = 4
HALO = 8
POOL_K = 384
TM_FFN = 512
TM_IN = 512
MXU_WIDTH = 256
FF_SPLITS = ((0, 6 * MXU_WIDTH), (6 * MXU_WIDTH, D_FF))
VMEM_LIMIT = 56 * 1024 * 1024

F32 = jnp.float32
BF16 = jnp.bfloat16


def _cond_of_tile(t):
    return jnp.where(t < CTX_TILES, 0, 1 + (t - CTX_TILES) // LAT_TILES_PER_SEQ)


def _dot(a, b, dims=((1,), (0,))):
    return lax.dot_general(a, b, (dims, ((), ())), preferred_element_type=F32)


_NT = ((1,), (1,))


def _split2(x):
    hi = x.astype(BF16)
    lo = (x - hi.astype(F32)).astype(BF16)
    return hi, lo


def _dot1(a, b, dims=((1,), (0,))):
    return _dot(a.astype(BF16), b.astype(BF16), dims)


def _dot3(a, b, dims=((1,), (0,))):
    assert dims == ((1,), (0,))
    n = a.shape[0]
    a_hi = a.astype(BF16).astype(F32)
    a_both = jnp.concatenate([a_hi, a - a_hi], axis=0).astype(BF16)
    bh, bl = _split2(b)
    top = _dot(a_both, bh)
    return top[:n] + top[n:] + _dot(a_hi.astype(BF16), bl)


def _dot_exact_lhs(a_bf16, b):
    b0, b1 = _split2(b)
    return _dot(a_bf16, b0) + _dot(a_bf16, b1)


def _rms(x, g):
    return x * lax.rsqrt(jnp.mean(x * x, axis=-1, keepdims=True) + EPS) * g


def _sigmoid(x):
    return 1.0 / (1.0 + jnp.exp(-x))


def _silu(x):
    return x * _sigmoid(x)


def _softplus(x):
    return jnp.maximum(x, 0.0) + jnp.log(1.0 + jnp.exp(-jnp.abs(x)))


def _group_ones(n, g):
    i = lax.broadcasted_iota(jnp.int32, (n, n), 0) // g
    j = lax.broadcasted_iota(jnp.int32, (n, n), 1) // g
    return (i == j).astype(BF16)


def _gsum(x, ones):
    hi, lo = _split2(x)
    return _dot(hi, ones) + _dot(lo, ones)


def _adaln_kernel(c_ref, w_ref, b_ref, o_ref):
    o_ref[0] = _dot3(_silu(c_ref[...]), w_ref[0]) + b_ref[0]


def _adaln(conds, ada_w, ada_b):
    tn = 1536
    return pl.pallas_call(
        _adaln_kernel,
        grid=(DEPTH, N_MOD * D_MODEL // tn),
        in_specs=[
            pl.BlockSpec((8, D_MODEL), lambda l, j: (0, 0)),
            pl.BlockSpec((1, D_MODEL, tn), lambda l, j: (l, 0, j)),
            pl.BlockSpec((1, 1, tn), lambda l, j: (l, 0, j)),
        ],
        out_specs=pl.BlockSpec((1, 8, tn), lambda l, j: (l, 0, j)),
        out_shape=jax.ShapeDtypeStruct((DEPTH, 8, N_MOD * D_MODEL), F32),
        compiler_params=pltpu.CompilerParams(
            dimension_semantics=("arbitrary", "arbitrary"), vmem_limit_bytes=VMEM_LIMIT),
        name="adaln",
    )(conds, ada_w, ada_b.reshape(DEPTH, 1, N_MOD * D_MODEL))


def _mod_spec(l):
    return pl.BlockSpec((1, 1, N_MOD, D_MODEL), lambda t, *_: (l, _cond_of_tile(t), 0, 0))


def _row_specs(x, width, tm=TM):
    if not isinstance(x, tuple):
        return [pl.BlockSpec((tm, width), lambda t, *_: (t, 0))], [x]
    ctx_tiles = N_CTX // tm
    return [pl.BlockSpec((tm, width), lambda t, *_: (jnp.minimum(t, ctx_tiles - 1), 0)),
            pl.BlockSpec((tm, width), lambda t, *_: (jnp.maximum(t - ctx_tiles, 0), 0))], list(x)


def _row_tile(refs, tm=TM):
    if len(refs) == 1:
        return refs[0][...]
    return jnp.where(pl.program_id(0) < N_CTX // tm, refs[0][...], refs[1][...])


def _inproj_kernel(*refs, n_x):
    x_refs, (mod_ref, g_ref, w_ref, *mla_refs) = refs[:n_x], refs[n_x:]
    *mla_in, or_ref, q_ref, k_ref, v_ref, ckv_ref, kr_ref = mla_refs
    *mla_w, cos_ref, sin_ref = mla_in
    m = mod_ref[0, 0]
    x = _row_tile(x_refs, TM_IN)
    is_lat = pl.program_id(0) >= N_CTX // TM_IN
    halves = [pl.ds(k * TM, TM) for k in range(TM_IN // TM)]
    ps = [_dot((_rms(x[k * TM:(k + 1) * TM], g_ref[...]) * (1.0 + m[1:2]) + m[0:1]).astype(BF16), w_ref[...])
          for k in range(TM_IN // TM)]
    for rows, p in zip(halves, ps):
        or_ref[rows, :] = p[:, :RWKV_IN]
        _mla_project(p[:, RWKV_IN:], is_lat, *mla_w, cos_ref.at[rows], sin_ref.at[rows],
                     *(r.at[rows] for r in (q_ref, k_ref, v_ref, ckv_ref, kr_ref)))


def _inproj(x, mods, l, g, w, p, rope_tabs):
    x_specs, x_args = _row_specs(x, D_MODEL, TM_IN)
    per_tile = TM_IN // TM
    ctx_tiles, lat_tiles_per_seq = N_CTX // TM_IN, DEC_SEQ // TM_IN
    full = lambda shape: pl.BlockSpec(shape, lambda t: tuple(0 for _ in shape))
    rope_spec = pl.BlockSpec((TM_IN, LANES), lambda t: (jnp.maximum(t - ctx_tiles, 0) % lat_tiles_per_seq, 0))
    rows = lambda w_: pl.BlockSpec((TM_IN, w_), lambda t: (t, 0))
    return pl.pallas_call(
        functools.partial(_inproj_kernel, n_x=len(x_args)),
        grid=(N_TOK // TM_IN,),
        in_specs=x_specs + [
            pl.BlockSpec((1, 1, N_MOD, D_MODEL), lambda t: (l, _cond_of_tile(t * per_tile), 0, 0)),
            full((1, D_MODEL)),
            full((D_MODEL, RWKV_IN + MLA_PAD)),
            full((1, Q_LORA)), full((Q_LORA, 2 * H_B * LANES)), full((1, LANES)), full((1, LANES)),
            full((1, KV_LORA)), full((KV_LORA, 2 * H_B * NOPE)), full((1, LANES)), full((1, LANES)),
            rope_spec, rope_spec,
        ],
        out_specs=[rows(RWKV_IN), rows(2 * H_B * LANES), rows(2 * H_B * LANES), rows(H_B * V_DIM),
                   rows(KV_LORA), rows(ROPE_DIM)],
        out_shape=[
            jax.ShapeDtypeStruct((N_TOK, RWKV_IN), F32),
            jax.ShapeDtypeStruct((N_TOK, 2 * H_B * LANES), BF16),
            jax.ShapeDtypeStruct((N_TOK, 2 * H_B * LANES), BF16),
            jax.ShapeDtypeStruct((N_TOK, H_B * V_DIM), BF16),
            jax.ShapeDtypeStruct((N_TOK, KV_LORA), F32),
            jax.ShapeDtypeStruct((N_TOK, ROPE_DIM), F32),
        ],
        compiler_params=pltpu.CompilerParams(
            dimension_semantics=("arbitrary",), vmem_limit_bytes=VMEM_LIMIT),
        name="inproj",
    )(*x_args, mods, g, w, p["cq_g"], p["uq_w"], p["qn_g"], p["qr_g"], p["ckv_g"], p["ukv_w"], p["kn_g"],
      p["kr_g"], *rope_tabs)


def _rwkv_kernel(pr_ref, pk_ref, pv_ref, pl_ref, mur_ref, muk_ref, muv_ref, mul_ref,
                 w0_ref, wup_ref, a0_ref, aup_ref, gup_ref, kkg_ref, kag_ref, rkg_ref,
                 lng_ref, lnb_ref, *rest, seq_len, n_seqs, trip, zero_init):
    if zero_init:
        h0_ref = None
        out_ref, hfin_ref, m_s, n_s, yq_s, y0_s, g_s, bonus_s = rest
    else:
        h0_ref, out_ref, hfin_ref, m_s, n_s, yq_s, y0_s, g_s, bonus_s = rest
    C = CHUNK
    n_chunks = seq_len // C
    all_chunks = n_seqs * n_chunks
    static_trip = all_chunks == trip
    lane = lax.broadcasted_iota(jnp.int32, (1, LANES), 1)
    m0 = (lane < HD_A).astype(F32)
    m1 = 1.0 - m0
    ones_g = _group_ones(LANES, HD_A)
    row = lax.broadcasted_iota(jnp.int32, (C, 1), 0)
    i2 = lax.broadcasted_iota(jnp.int32, (2 * C, 2 * C), 0)
    j2 = lax.broadcasted_iota(jnp.int32, (2 * C, 2 * C), 1)
    same_head = (i2 // C) == (j2 // C)
    eye2 = (i2 == j2).astype(F32)
    ic = lax.broadcasted_iota(jnp.int32, (C, C), 0)
    jc = lax.broadcasted_iota(jnp.int32, (C, C), 1)
    incl_c = ((jc <= ic).astype(BF16), (jc >= ic).astype(BF16))
    incl_blk = (same_head & (j2 <= i2), same_head & (j2 >= i2))
    strict_blk = (same_head & (j2 < i2), same_head & (j2 > i2))

    def pair_masks(d):
        masks = []
        s = 1
        while s < C:
            later, earlier = ((i2 // s) % 2, (j2 // s) % 2) if d == 0 else ((j2 // s) % 2, (i2 // s) % 2)
            masks.append(((i2 // (2 * s)) == (j2 // (2 * s))) & (later == 1) & (earlier == 0))
            s *= 2
        return masks

    pair_blk = (pair_masks(0), pair_masks(1))

    def stack(x):
        return jnp.concatenate([x * m0, x * m1], axis=0)

    def chunk_rows(c):
        return pl.ds(c * C, C) if isinstance(c, int) else pl.ds(pl.multiple_of(c * C, C), C)

    def shifted(ref, mu, c):
        x = ref[chunk_rows(c), :]
        if isinstance(c, int):
            first, last = c % n_chunks == 0, c % n_chunks == n_chunks - 1
            prev_row = jnp.zeros((1, x.shape[1]), F32) if first else ref[c * C - 1:c * C, :]
            next_row = jnp.zeros((1, x.shape[1]), F32) if last else ref[(c + 1) * C:(c + 1) * C + 1, :]
        else:
            pstart = pl.multiple_of(jnp.maximum(c * C - 8, 0), 8)
            nstart = pl.multiple_of(jnp.minimum((c + 1) * C, n_seqs * seq_len - 8), 8)
            prev_row = jnp.where(c % n_chunks > 0, ref[pl.ds(pstart, 8), :][7:8], 0.0)
            next_row = jnp.where(c % n_chunks < n_chunks - 1, ref[pl.ds(nstart, 8), :][0:1], 0.0)
        prev = jnp.where(row == 0, prev_row, pltpu.roll(x, 1, 0))
        nxt = jnp.where(row == C - 1, next_row, pltpu.roll(x, C - 1, 0))
        return x + mu * (0.5 * (prev + nxt) - x)

    def chain_mats(chains):
        ds = [ch[0] for ch in chains]
        lp = [_dot_exact_lhs(incl_c[d], logw) for d, *_, logw in chains]
        lpc = [x[C - 1:C] if d == 0 else x[0:1] for d, x in zip(ds, lp)]
        st = []
        for (d, r, kd, v, kk, b, logw), x, xc in zip(chains, lp, lpc):
            p_in, p_ex, p_inv, p_end = jnp.exp(x), jnp.exp(x - logw), jnp.exp(-x), jnp.exp(xc - x)
            st.append(dict(qs=stack(r * p_in), a_s=stack(kk * p_ex), khs=stack(kd * p_inv),
                           bs=stack(b * p_inv), kbs=stack(kd * p_end), bbs=stack(b * p_end), vs=stack(v)))
        R = 2 * C
        quad = [_dot1(jnp.concatenate([s["a_s"], s["qs"]], axis=0),
                      jnp.concatenate([s["bs"], s["khs"]], axis=0), _NT) for s in st]
        lab = [jnp.where(strict_blk[d], q[:R, :R], 0.0) for d, q in zip(ds, quad)]
        lak = [jnp.where(strict_blk[d], q[:R, R:], 0.0) for d, q in zip(ds, quad)]
        dqb = [jnp.where(incl_blk[d], q[R:, :R], 0.0) for d, q in zip(ds, quad)]
        dqk = [jnp.where(incl_blk[d], q[R:, R:], 0.0) for d, q in zip(ds, quad)]
        x2 = [_dot1(l, s["vs"]) for l, s in zip(lak, st)]
        tinv = [eye2 - jnp.where(pair_blk[d][0], l, 0.0) for d, l in zip(ds, lab)]
        for lvl in range(1, len(pair_blk[0])):
            half = [_dot1(t, jnp.where(pair_blk[d][lvl], l, 0.0)) for d, t, l in zip(ds, tinv, lab)]
            tinv = [t - _dot1(h, t) for h, t in zip(half, tinv)]
        w12 = [_dot1(t, jnp.concatenate([s["a_s"], x], axis=1)) for t, s, x in zip(tinv, st, x2)]
        zero_blk = jnp.zeros((R, R), F32)
        fin = [_dot1(jnp.concatenate([jnp.concatenate([s["bbs"].T, -s["kbs"].T], axis=1),
                                      jnp.concatenate([qb, -qk], axis=1)], axis=0),
                     jnp.concatenate([w, jnp.concatenate([zero_blk, s["vs"]], axis=1)], axis=0))
               for s, qb, qk, w in zip(st, dqb, dqk, w12)]
        fold = lambda x: x[:C] + x[C:]
        return [tuple(fold(x) for x in (eye2 * jnp.exp(xc) - f[:R, :R], -f[:R, R:], s["qs"] - f[R:, :R], -f[R:, R:]))
                for xc, f, s in zip(lpc, fin, st)]

    def load_chunk(c):
        return (shifted(pr_ref, mur_ref[...], c), shifted(pk_ref, muk_ref[...], c),
                shifted(pv_ref, muv_ref[...], c), shifted(pl_ref, mul_ref[...], c))

    def compute_chunks(loaded):
        rows_of = lambda j: slice(j * C, (j + 1) * C)
        r_all, k_all, v_all, lo_all = (jnp.concatenate([ch[j] for ch in loaded], axis=0) for j in range(4))
        wa = lo_all[:, :LANES]
        gate_all = _dot1(_sigmoid(lo_all[:, LANES:]), gup_ref[...])
        kkr = k_all * kkg_ref[...]
        kk_all = kkr / (jnp.sqrt(_gsum(kkr * kkr, ones_g)) + 1e-12)
        bonus_all = _gsum(r_all * k_all * rkg_ref[...], ones_g) * v_all
        both = lambda ref: jnp.concatenate([ref[0], ref[1]], axis=1)
        both_rows = lambda ref: jnp.concatenate([ref[0:1, :], ref[1:2, :]], axis=1)
        wl = both_rows(w0_ref) + _dot3(jnp.tanh(wa), both(wup_ref))
        logw_all = -jnp.exp(-_softplus(-wl) - 0.5)
        a_all = _sigmoid(both_rows(a0_ref) + _dot3(wa, both(aup_ref)))
        chains = []
        for j, (r, k, v, _) in enumerate(loaded):
            kk = kk_all[rows_of(j)]
            for d in range(2):
                a = a_all[rows_of(j), d * LANES:(d + 1) * LANES]
                logw = logw_all[rows_of(j), d * LANES:(d + 1) * LANES]
                chains.append((d, r, k * (1.0 + (a - 1.0) * kag_ref[...]), v, kk, kk * a, logw))
        mats = chain_mats(chains)
        return [(gate_all[rows_of(j)], bonus_all[rows_of(j)], mats[2 * j:2 * j + 2]) for j in range(len(loaded))]

    def store_chunk(c, gate, bonus, mats):
        g_s[chunk_rows(c), :] = gate
        bonus_s[chunk_rows(c), :] = bonus
        for d in range(2):
            m_s[d, c], n_s[d, c], yq_s[d, c], y0_s[d, c] = mats[d]

    def phase_a_trip(i, carry):
        chunks = [i * trip + u for u in range(trip)]
        results = compute_chunks([load_chunk(c) for c in chunks])
        for c, res in zip(chunks, results):
            store_chunk(c, *res)
        return carry

    if static_trip:
        phase_a_trip(0, 0)
    else:
        lax.fori_loop(0, all_chunks // trip, phase_a_trip, 0)

    def phase_b(i, states):
        first_zero = zero_init and isinstance(i, int) and i == 0
        units = [(q, d, q * n_chunks + (i if d == 0 else n_chunks - 1 - i))
                 for q in range(n_seqs) for d in range(2)]
        if first_zero:
            return tuple(stack(n_s[d, c]) for _, d, c in units)
        new_states = tuple(stack(_dot3(m_s[d, c], states[2 * q + d]) + n_s[d, c]) for q, d, c in units)
        for q, d, c in units:
            y0_s[d, c] = y0_s[d, c] + _dot1(yq_s[d, c], states[2 * q + d])
        return new_states

    zeros_half = jnp.zeros((HD_A, HD_A), F32)
    states = []
    for q in range(n_seqs):
        for d in range(2):
            if zero_init:
                states.append(jnp.zeros((LANES, LANES), F32))
            else:
                top = jnp.concatenate([h0_ref[q, 0, d, 0], zeros_half], axis=1)
                bot = jnp.concatenate([zeros_half, h0_ref[q, 0, d, 1]], axis=1)
                states.append(jnp.concatenate([top, bot], axis=0).T)
    states = tuple(states)
    if static_trip:
        for i in range(n_chunks):
            states = phase_b(i, states)
    else:
        states = lax.fori_loop(0, n_chunks, phase_b, states)
    for q in range(n_seqs):
        for d in range(2):
            s_vk = states[2 * q + d].T
            hfin_ref[q, d, 0] = s_vk[:HD_A, :HD_A]
            hfin_ref[q, d, 1] = s_vk[HD_A:, HD_A:]

    def phase_c_trip(i, carry):
        chunks = [i * trip + u for u in range(trip)]
        ys = [y0_s[0, c] + y0_s[1, c] for c in chunks]
        extras = [(bonus_s[chunk_rows(c), :], g_s[chunk_rows(c), :]) for c in chunks]
        y_all = jnp.concatenate(ys, axis=0)
        yc = y_all - _gsum(y_all, ones_g) * (1.0 / HD_A)
        var = _gsum(yc * yc, ones_g) * (1.0 / HD_A)
        yn = yc * lax.rsqrt(var + GN_EPS) * lng_ref[...] + lnb_ref[...]
        for j, (c, (bonus, gate)) in enumerate(zip(chunks, extras)):
            out_ref[chunk_rows(c), :] = ((yn[j * C:(j + 1) * C] + bonus) * gate).astype(out_ref.dtype)
        return carry

    if static_trip:
        phase_c_trip(0, 0)
    else:
        lax.fori_loop(0, all_chunks // trip, phase_c_trip, 0)


def _rwkv(proj_r, p, h0, layer, *, seq_len, n_seq, row_off):
    zero_init = h0 is None
    trip = SHORT_SEQ_TRIP if seq_len // CHUNK < CHUNKS_PER_TRIP else CHUNKS_PER_TRIP
    seqs = max(min(LONG_SEQS_PER_STEP, n_seq), trip // (seq_len // CHUNK))
    blk_rows = seqs * seq_len
    n_chunks = blk_rows // CHUNK
    off = row_off // blk_rows
    n_pairs = DA // LANES
    col = lambda cb: (lambda s, hp: (off + s, cb + hp))
    vec = lambda: pl.BlockSpec((1, LANES), lambda s, hp: (0, hp))
    in_specs = [
        pl.BlockSpec((blk_rows, LANES), col(0)),
        pl.BlockSpec((blk_rows, LANES), col(n_pairs)),
        pl.BlockSpec((blk_rows, LANES), col(2 * n_pairs)),
        pl.BlockSpec((blk_rows, 2 * LANES), lambda s, hp: (off + s, 3 * n_pairs // 2)),
        vec(), vec(), vec(),
        pl.BlockSpec((1, 2 * LANES), lambda s, hp: (0, 0)),
        pl.BlockSpec((2, LANES), lambda s, hp: (0, hp)),
        pl.BlockSpec((2, LANES, LANES), lambda s, hp: (0, 0, hp)),
        pl.BlockSpec((2, LANES), lambda s, hp: (0, hp)),
        pl.BlockSpec((2, LANES, LANES), lambda s, hp: (0, 0, hp)),
        pl.BlockSpec((G_LORA, LANES), lambda s, hp: (0, hp)),
        vec(), vec(), vec(), vec(), vec(),
    ]
    args = [proj_r, proj_r, proj_r, proj_r, p["mu_r"], p["mu_k"], p["mu_v"], p["mu_l"],
            p["w0"], p["w_up"], p["a0"], p["a_up"], p["g_up"], p["k_k"], p["k_a"], p["r_k"],
            p["ln_g"], p["ln_b"]]
    if not zero_init:
        in_specs.append(pl.BlockSpec((seqs, 1, 2, 2, HD_A, HD_A), lambda s, hp: (s, layer, 0, hp, 0, 0)))
        args.append(h0)
    blk = (CHUNK, LANES)
    return pl.pallas_call(
        functools.partial(_rwkv_kernel, seq_len=seq_len, n_seqs=seqs, trip=trip, zero_init=zero_init),
        grid=(n_seq // seqs, n_pairs),
        in_specs=in_specs,
        out_specs=[
            pl.BlockSpec((blk_rows, LANES), lambda s, hp: (s, hp)),
            pl.BlockSpec((seqs, 2, 2, HD_A, HD_A), lambda s, hp: (s, 0, hp, 0, 0)),
        ],
        out_shape=[
            jax.ShapeDtypeStruct((n_seq * seq_len, DA), BF16),
            jax.ShapeDtypeStruct((n_seq, 2, H_A, HD_A, HD_A), F32),
        ],
        scratch_shapes=[
            pltpu.VMEM((2, n_chunks) + blk, F32),
            pltpu.VMEM((2, n_chunks) + blk, F32),
            pltpu.VMEM((2, n_chunks) + blk, F32),
            pltpu.VMEM((2, n_chunks) + blk, F32),
            pltpu.VMEM((blk_rows, LANES), F32),
            pltpu.VMEM((blk_rows, LANES), F32),
        ],
        compiler_params=pltpu.CompilerParams(
            dimension_semantics=("arbitrary", "arbitrary"), vmem_limit_bytes=VMEM_LIMIT),
        name="rwkv_ctx" if zero_init else "rwkv_lat",
    )(*args)


def _rope(x, cos, sin):
    lane = lax.broadcasted_iota(jnp.int32, x.shape, 1)
    swapped = jnp.where((lane % 32) < 16, pltpu.roll(x, LANES - 16, 1), pltpu.roll(x, 16, 1))
    return x * cos + swapped * sin


def _slab_rms(x, g):
    return x * lax.rsqrt(jnp.sum(x * x, axis=-1, keepdims=True) * (1.0 / ROPE_DIM) + EPS) * g


def _kv_expand(ckv_n, kr_att, ukv_ref, kng_ref, k_ref, v_ref):
    kv = _dot(ckv_n.astype(BF16), ukv_ref[...])
    kr_b = kr_att.astype(BF16)
    for h in range(H_B):
        kn = _rms(kv[:, h * NOPE:(h + 1) * NOPE], kng_ref[...])
        k_ref[:, 2 * h * LANES:(2 * h + 1) * LANES] = kn.astype(BF16)
        k_ref[:, (2 * h + 1) * LANES:(2 * h + 2) * LANES] = kr_b
    v_ref[...] = kv[:, H_B * NOPE:].astype(BF16)


def _mla_project(pm, is_lat, cqg_ref, uq_ref, qng_ref, qrg_ref, ckvg_ref, ukv_ref, kng_ref, krg_ref,
                 cos_ref, sin_ref, q_ref, k_ref, v_ref, ckv_ref, kr_ref):
    cq = _rms(pm[:, :Q_LORA], cqg_ref[...])
    ckv_n = _rms(pm[:, Q_LORA:Q_LORA + KV_LORA], ckvg_ref[...])
    q = _dot(cq.astype(BF16), uq_ref[...])
    ckv_ref[...] = ckv_n
    kr = _slab_rms(pm[:, Q_LORA + KV_LORA:], krg_ref[...])
    kr_ref[...] = kr[:, :ROPE_DIM]
    _kv_expand(ckv_n, kr, ukv_ref, kng_ref, k_ref, v_ref)
    qrs = []
    for h in range(H_B):
        qn = _rms(q[:, 2 * h * LANES:(2 * h + 1) * LANES], qng_ref[...])
        qrs.append(_slab_rms(q[:, (2 * h + 1) * LANES:(2 * h + 2) * LANES], qrg_ref[...]))
        q_ref[:, 2 * h * LANES:(2 * h + 1) * LANES] = (qn * ATTN_SCALE).astype(BF16)
        q_ref[:, (2 * h + 1) * LANES:(2 * h + 2) * LANES] = (qrs[h] * ATTN_SCALE).astype(BF16)

    @pl.when(is_lat)
    def _():
        cos, sin = cos_ref[...], sin_ref[...]
        kr_rot = _rope(kr, cos, sin).astype(BF16)
        for h in range(H_B):
            k_ref[:, (2 * h + 1) * LANES:(2 * h + 2) * LANES] = kr_rot
            q_ref[:, (2 * h + 1) * LANES:(2 * h + 2) * LANES] = (_rope(qrs[h], cos, sin) * ATTN_SCALE).astype(BF16)


def _mla_cache_kernel(ckv_ref, kr_ref, ukv_ref, kng_ref, k_ref, v_ref):
    _kv_expand(ckv_ref[...], kr_ref[...], ukv_ref, kng_ref, k_ref, v_ref)


def _mla_cache_expand(ckv, kr_slab, p):
    n = ckv.shape[0]
    full = lambda shape: pl.BlockSpec(shape, lambda t: tuple(0 for _ in shape))
    return pl.pallas_call(
        _mla_cache_kernel,
        grid=(1,),
        in_specs=[full((n, KV_LORA)), full((n, LANES)), full((KV_LORA, 2 * H_B * NOPE)), full((1, LANES))],
        out_specs=[full((n, 2 * H_B * LANES)), full((n, H_B * V_DIM))],
        out_shape=[
            jax.ShapeDtypeStruct((n, 2 * H_B * LANES), BF16),
            jax.ShapeDtypeStruct((n, H_B * V_DIM), BF16),
        ],
        compiler_params=pltpu.CompilerParams(
            dimension_semantics=("arbitrary",), vmem_limit_bytes=VMEM_LIMIT),
        name="mla_cache_expand",
    )(ckv, kr_slab, p["ukv_w"], p["kn_g"])


def _attn_kernel(q_ref, k_ref, v_ref, *rest, with_ctx, seq_len, n_seqs):
    if with_ctx:
        k2_ref, v2_ref, o_ref = rest
    else:
        (o_ref,) = rest
    hs = [slice(2 * h * LANES, (2 * h + 2) * LANES) for h in range(H_B)]
    vs = [slice(h * V_DIM, (h + 1) * V_DIM) for h in range(H_B)]
    units = [(b, h) for b in range(n_seqs) for h in range(H_B)]
    qrows = lambda b: slice(b * TM, (b + 1) * TM)
    krows = lambda b: slice(b * seq_len, (b + 1) * seq_len)

    def keys_values(b, h):
        kv = [(k_ref[krows(b), hs[h]], v_ref[krows(b), vs[h]])]
        if with_ctx:
            kv.append((k2_ref[:, hs[h]], v2_ref[:, vs[h]]))
        return kv

    scores = [[_dot(q_ref[qrows(b), hs[h]], k, _NT) for k, _ in keys_values(b, h)] for b, h in units]
    probs, dens = [], []
    for sc in scores:
        m = functools.reduce(jnp.maximum, [jnp.max(s, axis=-1, keepdims=True) for s in sc])
        pr = [jnp.exp(s - m) for s in sc]
        dens.append(functools.reduce(lambda a, b: a + b, [jnp.sum(p, axis=-1, keepdims=True) for p in pr]))
        probs.append([p.astype(BF16) for p in pr])
    for (b, h), pr, den in zip(units, probs, dens):
        o = functools.reduce(lambda a, c: a + c, [_dot(p, v) for p, (_, v) in zip(pr, keys_values(b, h))])
        o_ref[qrows(b), vs[h]] = (o / den).astype(o_ref.dtype)


def _attention(q, k, v, k2=None, v2=None, *, n_batch, seq_len, row_off):
    with_ctx = k2 is not None
    q_tiles = seq_len // TM
    n_seqs = ATTN_SEQS_PER_STEP if q_tiles == 1 else 1
    kw = 2 * H_B * LANES
    q0 = row_off // (n_seqs * TM)
    k0 = row_off // (n_seqs * seq_len)
    in_specs = [
        pl.BlockSpec((n_seqs * TM, kw), lambda b, t: (q0 + b * q_tiles + t, 0)),
        pl.BlockSpec((n_seqs * seq_len, kw), lambda b, t: (k0 + b, 0)),
        pl.BlockSpec((n_seqs * seq_len, H_B * V_DIM), lambda b, t: (k0 + b, 0)),
    ]
    args = [q, k, v]
    if with_ctx:
        in_specs += [
            pl.BlockSpec((PAST_LEN, kw), lambda b, t: (b, 0)),
            pl.BlockSpec((PAST_LEN, H_B * V_DIM), lambda b, t: (b, 0)),
        ]
        args += [k2, v2]
    return pl.pallas_call(
        functools.partial(_attn_kernel, with_ctx=with_ctx, seq_len=seq_len, n_seqs=n_seqs),
        grid=(n_batch // n_seqs, q_tiles),
        in_specs=in_specs,
        out_specs=pl.BlockSpec((n_seqs * TM, H_B * V_DIM), lambda b, t: (b * q_tiles + t, 0)),
        out_shape=jax.ShapeDtypeStruct((n_batch * seq_len, H_B * V_DIM), BF16),
        compiler_params=pltpu.CompilerParams(
            dimension_semantics=("arbitrary", "arbitrary"), vmem_limit_bytes=VMEM_LIMIT),
        name="attn_lat" if with_ctx else "attn_ctx",
    )(*args)


def _pool_bands():
    i = jnp.arange(TM)[:, None]
    e = jnp.arange(POOL_K)[None, :]
    e = jnp.where(e < TM, e, jnp.where(e < TM + HALO, e - TM - HALO, e - HALO))
    return jnp.stack([(e >= i - w // 2) & (e < i + w // 2) for w in POOL_WINDOWS]).astype(BF16)


def _pool_diffs(h, h_prev, h_next, pos0, seq_len, band_ref):
    ext = jnp.concatenate([h, h_prev, h_next, jnp.zeros((POOL_K - TM - 2 * HALO, D_MODEL), F32)], axis=0)
    pos = pos0 + lax.broadcasted_iota(jnp.int32, (TM, 1), 0)
    sums = [_dot_exact_lhs(band_ref[gi], ext[:, gi * GP:(gi + 1) * GP]) for gi in range(len(POOL_WINDOWS))]
    diffs = []
    for gi, win in enumerate(POOL_WINDOWS):
        half = win // 2
        cnt = (jnp.minimum(pos + half, seq_len) - jnp.maximum(pos - half, 0)).astype(F32)
        diffs.append((sums[gi] / cnt - h[:, gi * GP:(gi + 1) * GP]).astype(BF16))
    return diffs


def _tail_kernel(*refs, mixer, n_x, row_off):
    x_refs, rest = refs[:n_x], refs[n_x:]
    if mixer == "proj":
        a0_ref, a1_ref, b0_ref, b1_ref, ow_ref, mod_ref, g2_ref, w1_ref, w3_ref, w2_ref, o_ref = rest
    else:
        xp_ref, xn_ref, g1_ref, band_ref, pw_ref, ps_ref, mod_ref, g2_ref, w1_ref, w3_ref, w2_ref, o_ref = rest
    m = mod_ref[0, 0]
    x = _row_tile(x_refs, TM_FFN)
    if mixer == "proj":
        mix = (_dot(_row_tile((a0_ref, a1_ref), TM_FFN), ow_ref[:DA, :])
               + _dot(_row_tile((b0_ref, b1_ref), TM_FFN), ow_ref[DA:, :]))
    else:
        row0 = row_off + pl.program_id(0) * TM_FFN
        is_lat = row0 >= N_CTX
        seq_len = jnp.where(is_lat, DEC_SEQ, SEQ)
        norm_mod = lambda v: _rms(v, g1_ref[...]) * (1.0 + m[1:2]) + m[0:1]
        h, hp, hn = norm_mod(x), norm_mod(xp_ref[...]), norm_mod(xn_ref[...])
        parts = []
        n_sub = TM_FFN // TM
        for k in range(n_sub):
            r0 = row0 + k * TM
            pos0 = jnp.where(is_lat, (r0 - N_CTX) % DEC_SEQ, r0 % SEQ)
            prev = hp if k == 0 else h[k * TM - HALO:k * TM]
            nxt = hn if k == n_sub - 1 else h[(k + 1) * TM:(k + 1) * TM + HALO]
            prev = jnp.where(pos0 > 0, prev, 0.0)
            nxt = jnp.where(pos0 + TM < seq_len, nxt, 0.0)
            parts.append(_pool_diffs(h[k * TM:(k + 1) * TM], prev, nxt, pos0, seq_len, band_ref))
        mix = jnp.concatenate(
            [_dot(jnp.concatenate([p[gi] for p in parts], axis=0), pw_ref[gi]) for gi in range(len(POOL_WINDOWS))],
            axis=1) * ps_ref[...]
    x1 = x + m[2:3] * mix
    h2 = (_rms(x1, g2_ref[...]) * (1.0 + m[4:5]) + m[3:4]).astype(BF16)
    ups = [(_dot(h2, w1_ref[0, :, lo:hi]), _dot(h2, w3_ref[0, :, lo:hi])) for lo, hi in FF_SPLITS]
    acts = [(_silu(a) * b).astype(BF16) for a, b in ups]
    acc = functools.reduce(lambda p, q: p + q,
                           [_dot(u, w2_ref[0, lo:hi, :]) for u, (lo, hi) in zip(acts, FF_SPLITS)])
    o_ref[...] = x1 + m[5:6] * acc


def _layer_tail(x, mods, l, g2, w1, w3, w2, *, proj=None, pool=None, row_off=0, n_rows=N_TOK):
    per_tile = TM_FFN // TM
    t0 = row_off // TM_FFN
    const = lambda shape: pl.BlockSpec(shape, lambda i: tuple(0 for _ in shape))
    resident = lambda shape: pl.BlockSpec((1,) + shape, lambda i: (l, 0, 0), pipeline_mode=pl.Buffered(1))
    if proj is not None:
        assert row_off == 0 and n_rows == N_TOK
        a_out, b_out, out_w = proj
        x_specs, x_args = _row_specs(x, D_MODEL, TM_FFN)
        a_specs, a_args = _row_specs(a_out, DA, TM_FFN)
        b_specs, b_args = _row_specs(b_out, DA, TM_FFN)
        mix_specs = a_specs + b_specs + [const((D_MODEL, D_MODEL))]
        mix_args = a_args + b_args + [out_w]
        mixer = "proj"
    else:
        g1, pw, ps = pool
        r = TM_FFN // HALO
        last = N_TOK // HALO - 1
        x_specs = [pl.BlockSpec((TM_FFN, D_MODEL), lambda i: (t0 + i, 0))]
        x_args = [x]
        mix_specs = [
            pl.BlockSpec((HALO, D_MODEL), lambda i: (jnp.maximum((t0 + i) * r - 1, 0), 0)),
            pl.BlockSpec((HALO, D_MODEL), lambda i: (jnp.minimum((t0 + i + 1) * r, last), 0)),
            const((1, D_MODEL)), const((len(POOL_WINDOWS), TM, POOL_K)),
            const((len(POOL_WINDOWS), GP, GP)), const((1, D_MODEL)),
        ]
        mix_args = [x, x, g1, _pool_bands(), pw, ps]
        mixer = "pool"
    return pl.pallas_call(
        functools.partial(_tail_kernel, mixer=mixer, n_x=len(x_args), row_off=row_off),
        grid=(n_rows // TM_FFN,),
        in_specs=x_specs + mix_specs + [
            pl.BlockSpec((1, 1, N_MOD, D_MODEL), lambda i: (l, _cond_of_tile((t0 + i) * per_tile), 0, 0)),
            const((1, D_MODEL)),
            resident((D_MODEL, D_FF)), resident((D_MODEL, D_FF)), resident((D_FF, D_MODEL)),
        ],
        out_specs=pl.BlockSpec((TM_FFN, D_MODEL), lambda i: (i, 0)),
        out_shape=jax.ShapeDtypeStruct((n_rows, D_MODEL), F32),
        compiler_params=pltpu.CompilerParams(
            dimension_semantics=("arbitrary",), vmem_limit_bytes=VMEM_LIMIT),
        name="tail_" + mixer,
    )(*x_args, *mix_args, mods, g2, w1, w3, w2)


def _rope_tables():
    pos = jnp.arange(DEC_SEQ)
    freqs = ROPE_BASE ** (-jnp.arange(ROPE_AXIS_FREQS, dtype=F32) / ROPE_AXIS_FREQS)
    ang_row = (pos // GRID_W).astype(F32)[:, None] * freqs
    ang_col = (pos % GRID_W).astype(F32)[:, None] * freqs
    zeros = jnp.zeros((DEC_SEQ, LANES - ROPE_DIM), F32)
    cos = jnp.concatenate([jnp.cos(ang_row)] * 2 + [jnp.cos(ang_col)] * 2 + [zeros], axis=1)
    sin = jnp.concatenate([-jnp.sin(ang_row), jnp.sin(ang_row), -jnp.sin(ang_col), jnp.sin(ang_col), zeros], axis=1)
    return cos, sin


def _slab(v):
    return jnp.pad(v, (0, LANES - v.shape[0]))[None]


def _even_params(i, in_w, out_w, shift_mu, w0, w_up, a0, a_up, g_up, k_k, k_a, r_k, ln_g, ln_b,
                 cq_g, uq_w, ckv_g, ukv_w, qn_g, qr_g, kn_g, kr_g):
    row = lambda v: v[None]
    in_w_pad = jnp.pad(in_w[i], ((0, 0), (0, MLA_PAD - MLA_IN))).astype(BF16)
    mu = shift_mu[i]
    w_up_pad = jnp.pad(w_up[i], ((0, 0), (0, A_LORA), (0, 0)))
    a_up_pad = jnp.pad(a_up[i], ((0, 0), (W_LORA, 0), (0, 0)))
    uq = uq_w[i].reshape(Q_LORA, H_B, NOPE + ROPE_DIM)
    uq = jnp.pad(uq, ((0, 0), (0, 0), (0, 2 * LANES - NOPE - ROPE_DIM))).reshape(Q_LORA, 2 * H_B * LANES)
    ukv = ukv_w[i].reshape(KV_LORA, H_B, NOPE + V_DIM)
    ukv = jnp.concatenate([ukv[:, :, :NOPE].reshape(KV_LORA, -1), ukv[:, :, NOPE:].reshape(KV_LORA, -1)], axis=1)
    rw = dict(mu_r=row(mu[:DA]), mu_k=row(mu[DA:2 * DA]), mu_v=row(mu[2 * DA:3 * DA]), mu_l=row(mu[3 * DA:]),
              w0=w0[i], w_up=w_up_pad, a0=a0[i], a_up=a_up_pad, g_up=g_up[i],
              k_k=row(k_k[i]), k_a=row(k_a[i]), r_k=row(r_k[i].reshape(DA)), ln_g=row(ln_g[i]), ln_b=row(ln_b[i]))
    mla = dict(cq_g=row(cq_g[i]), uq_w=uq.astype(BF16), qn_g=row(qn_g[i]), qr_g=_slab(qr_g[i]),
               ckv_g=row(ckv_g[i]), ukv_w=ukv.astype(BF16), kn_g=row(kn_g[i]), kr_g=_slab(kr_g[i]))
    return in_w_pad, out_w[i].astype(BF16), rw, mla


def kernel(x_prompt, x_sample, c, state_rwkv, cache_mla_ckv, cache_mla_krope, c_ctx,
           norm1_g, norm2_g, ada_w, ada_b, ffn_w1, ffn_w3, ffn_w2,
           in_w, out_w, shift_mu, rwkv_w0, rwkv_w_up, rwkv_a0, rwkv_a_up, rwkv_g_up,
           rwkv_k_k, rwkv_k_a, rwkv_r_k, rwkv_ln_g, rwkv_ln_b,
           mla_cq_g, mla_uq_w, mla_ckv_g, mla_ukv_w, mla_qn_g, mla_qr_g, mla_kn_g, mla_kr_g,
           pool_w, pool_scale):
    x = (x_prompt.reshape(N_CTX, D_MODEL), x_sample.reshape(N_LAT, D_MODEL))
    conds = jnp.concatenate([c_ctx[None], c, jnp.zeros((8 - 1 - DEC_BATCH, D_MODEL), F32)], axis=0)
    mods = _adaln(conds, ada_w, ada_b).reshape(DEPTH, 8, N_MOD, D_MODEL)
    rope_tabs = _rope_tables()
    w1_all, w3_all, w2_all = ffn_w1.astype(BF16), ffn_w3.astype(BF16), ffn_w2.astype(BF16)

    new_s, new_ckv, new_kr = [], [], []
    for l in range(DEPTH):
        n1 = norm1_g[l][None]
        if l % 2 == 0:
            i = l // 2
            in_w_pad, out_w_b, rw, mla = _even_params(
                i, in_w, out_w, shift_mu, rwkv_w0, rwkv_w_up, rwkv_a0, rwkv_a_up, rwkv_g_up,
                rwkv_k_k, rwkv_k_a, rwkv_r_k, rwkv_ln_g, rwkv_ln_b,
                mla_cq_g, mla_uq_w, mla_ckv_g, mla_ukv_w, mla_qn_g, mla_qr_g, mla_kn_g, mla_kr_g)
            proj_r, q_all, k_all, v_all, ckv_all, kr_all = _inproj(x, mods, l, n1, in_w_pad, mla, rope_tabs)
            a_ctx, s_ctx = _rwkv(proj_r, rw, None, i, seq_len=SEQ, n_seq=BATCH, row_off=0)
            a_lat, _ = _rwkv(proj_r, rw, state_rwkv, i, seq_len=DEC_SEQ, n_seq=DEC_BATCH, row_off=N_CTX)
            kr_cache = jnp.pad(cache_mla_krope[:, i].reshape(DEC_BATCH * PAST_LEN, ROPE_DIM),
                               ((0, 0), (0, LANES - ROPE_DIM)))
            k_p, v_p = _mla_cache_expand(cache_mla_ckv[:, i].reshape(DEC_BATCH * PAST_LEN, KV_LORA), kr_cache, mla)
            b_ctx = _attention(q_all, k_all, v_all, n_batch=BATCH, seq_len=SEQ, row_off=0)
            b_lat = _attention(q_all, k_all, v_all, k_p, v_p, n_batch=DEC_BATCH, seq_len=DEC_SEQ, row_off=N_CTX)
            mixer = dict(proj=((a_ctx, a_lat), (b_ctx, b_lat), out_w_b))
            new_s.append(s_ctx)
            new_ckv.append(ckv_all[:N_CTX].reshape(BATCH, SEQ, KV_LORA))
            new_kr.append(kr_all[:N_CTX].reshape(BATCH, SEQ, ROPE_DIM))
        else:
            j = l // 2
            mixer = dict(pool=(n1, pool_w[j].astype(BF16), pool_scale[j][None]))
        tail_args = (mods, l, norm2_g[l][None], w1_all, w3_all, w2_all)
        if l < DEPTH - 1:
            x = _layer_tail(x, *tail_args, **mixer)
        else:
            y_p = _layer_tail(x, *tail_args, **mixer, row_off=0, n_rows=N_CTX).reshape(BATCH, SEQ, D_MODEL)
            y_s = _layer_tail(x, *tail_args, **mixer, row_off=N_CTX, n_rows=N_LAT).reshape(DEC_BATCH, DEC_SEQ, D_MODEL)
    return (y_p, y_s, jnp.stack(new_s, axis=1), jnp.stack(new_ckv, axis=1), jnp.stack(new_kr, axis=1))
```

```python
import functools

import jax
import jax.numpy as jnp
from jax import lax
from jax.experimental import pallas as pl
from jax.experimental.pallas import tpu as pltpu

D_MODEL = 1024
BATCH, SEQ = 32, 256
DEC_BATCH, DEC_SEQ = 2, 2048
DEPTH = 4
PAST_LEN = 256
GRID_W = 64
N_MOD = 6
D_FF = 2816
DA = 512
HD_A = 64
H_A = 8
W_LORA, A_LORA, G_LORA = 64, 64, 128
H_B = 4
NOPE, ROPE_DIM, V_DIM = 128, 64, 128
Q_LORA, KV_LORA = 384, 256
ATTN_SCALE = (NOPE + ROPE_DIM) ** -0.5
ROPE_AXIS_FREQS = ROPE_DIM // 4
ROPE_BASE = 10000.0
RWKV_IN = 3 * DA + W_LORA + A_LORA + G_LORA
MLA_IN = Q_LORA + KV_LORA + ROPE_DIM
MLA_PAD = 768
POOL_WINDOWS = (2, 4, 8, 16)
GP = 256
EPS = 1e-6
GN_EPS = 64e-5

N_CTX = BATCH * SEQ
N_LAT = DEC_BATCH * DEC_SEQ
N_TOK = N_CTX + N_LAT

LANES = 128
TM = 256
N_TILES = N_TOK // TM
CTX_TILES = N_CTX // TM
LAT_TILES_PER_SEQ = DEC_SEQ // TM
CHUNK = 64
CHUNKS_PER_TRIP = 8
LONG_SEQS_PER_STEP = 2
SHORT_SEQ_TRIP = 16
ATTN_SEQS_PER_STEP = 4
HALO = 8
POOL_K = 384
TM_FFN = 512
TM_IN = 512
MXU_WIDTH = 256
FF_SPLITS = ((0, 6 * MXU_WIDTH), (6 * MXU_WIDTH, D_FF))
VMEM_LIMIT = 56 * 1024 * 1024

F32 = jnp.float32
BF16 = jnp.bfloat16


def _cond_of_tile(t):
    return jnp.where(t < CTX_TILES, 0, 1 + (t - CTX_TILES) // LAT_TILES_PER_SEQ)


def _dot(a, b, dims=((1,), (0,))):
    return lax.dot_general(a, b, (dims, ((), ())), preferred_element_type=F32)


_NT = ((1,), (1,))


def _split2(x):
    hi = x.astype(BF16)
    lo = (x - hi.astype(F32)).astype(BF16)
    return hi, lo


def _dot1(a, b, dims=((1,), (0,))):
    return _dot(a.astype(BF16), b.astype(BF16), dims)


def _dot3(a, b, dims=((1,), (0,))):
    assert dims == ((1,), (0,))
    n = a.shape[0]
    a_hi = a.astype(BF16).astype(F32)
    a_both = jnp.concatenate([a_hi, a - a_hi], axis=0).astype(BF16)
    bh, bl = _split2(b)
    top = _dot(a_both, bh)
    return top[:n] + top[n:] + _dot(a_hi.astype(BF16), bl)


def _dot_exact_lhs(a_bf16, b):
    b0, b1 = _split2(b)
    return _dot(a_bf16, b0) + _dot(a_bf16, b1)


def _rms(x, g):
    return x * lax.rsqrt(jnp.mean(x * x, axis=-1, keepdims=True) + EPS) * g


def _sigmoid(x):
    return 1.0 / (1.0 + jnp.exp(-x))


def _silu(x):
    return x * _sigmoid(x)


def _softplus(x):
    return jnp.maximum(x, 0.0) + jnp.log(1.0 + jnp.exp(-jnp.abs(x)))


def _group_ones(n, g):
    i = lax.broadcasted_iota(jnp.int32, (n, n), 0) // g
    j = lax.broadcasted_iota(jnp.int32, (n, n), 1) // g
    return (i == j).astype(BF16)


def _gsum(x, ones):
    hi, lo = _split2(x)
    return _dot(hi, ones) + _dot(lo, ones)


def _adaln_kernel(c_ref, w_ref, b_ref, o_ref):
    o_ref[0] = _dot3(_silu(c_ref[...]), w_ref[0]) + b_ref[0]


def _adaln(conds, ada_w, ada_b):
    tn = 2048
    return pl.pallas_call(
        _adaln_kernel,
        grid=(DEPTH, N_MOD * D_MODEL // tn),
        in_specs=[
            pl.BlockSpec((8, D_MODEL), lambda l, j: (0, 0)),
            pl.BlockSpec((1, D_MODEL, tn), lambda l, j: (l, 0, j)),
            pl.BlockSpec((1, 1, tn), lambda l, j: (l, 0, j)),
        ],
        out_specs=pl.BlockSpec((1, 8, tn), lambda l, j: (l, 0, j)),
        out_shape=jax.ShapeDtypeStruct((DEPTH, 8, N_MOD * D_MODEL), F32),
        compiler_params=pltpu.CompilerParams(
            dimension_semantics=("arbitrary", "arbitrary"), vmem_limit_bytes=VMEM_LIMIT),
        name="adaln",
    )(conds, ada_w, ada_b.reshape(DEPTH, 1, N_MOD * D_MODEL))


def _mod_spec(l):
    return pl.BlockSpec((1, 1, N_MOD, D_MODEL), lambda t, *_: (l, _cond_of_tile(t), 0, 0))


def _row_specs(x, width, tm=TM):
    if not isinstance(x, tuple):
        return [pl.BlockSpec((tm, width), lambda t, *_: (t, 0))], [x]
    ctx_tiles = N_CTX // tm
    return [pl.BlockSpec((tm, width), lambda t, *_: (jnp.minimum(t, ctx_tiles - 1), 0)),
            pl.BlockSpec((tm, width), lambda t, *_: (jnp.maximum(t - ctx_tiles, 0), 0))], list(x)


def _row_tile(refs, tm=TM):
    if len(refs) == 1:
        return refs[0][...]
    return jnp.where(pl.program_id(0) < N_CTX // tm, refs[0][...], refs[1][...])


def _inproj_kernel(*refs, n_x):
    x_refs, (mod_ref, g_ref, w_ref, *mla_refs) = refs[:n_x], refs[n_x:]
    *mla_in, or_ref, q_ref, k_ref, v_ref, ckv_ref, kr_ref = mla_refs
    *mla_w, cos_ref, sin_ref = mla_in
    m = mod_ref[0, 0]
    x = _row_tile(x_refs, TM_IN)
    is_lat = pl.program_id(0) >= N_CTX // TM_IN
    halves = [pl.ds(k * TM, TM) for k in range(TM_IN // TM)]
    ps = [_dot((_rms(x[k * TM:(k + 1) * TM], g_ref[...]) * (1.0 + m[1:2]) + m[0:1]).astype(BF16), w_ref[...])
          for k in range(TM_IN // TM)]
    for rows, p in zip(halves, ps):
        or_ref[rows, :] = p[:, :RWKV_IN]
        _mla_project(p[:, RWKV_IN:], is_lat, *mla_w, cos_ref.at[rows], sin_ref.at[rows],
                     *(r.at[rows] for r in (q_ref, k_ref, v_ref, ckv_ref, kr_ref)))


def _inproj(x, mods, l, g, w, p, rope_tabs):
    x_specs, x_args = _row_specs(x, D_MODEL, TM_IN)
    per_tile = TM_IN // TM
    ctx_tiles, lat_tiles_per_seq = N_CTX // TM_IN, DEC_SEQ // TM_IN
    full = lambda shape: pl.BlockSpec(shape, lambda t: tuple(0 for _ in shape))
    rope_spec = pl.BlockSpec((TM_IN, LANES), lambda t: (jnp.maximum(t - ctx_tiles, 0) % lat_tiles_per_seq, 0))
    rows = lambda w_: pl.BlockSpec((TM_IN, w_), lambda t: (t, 0))
    return pl.pallas_call(
        functools.partial(_inproj_kernel, n_x=len(x_args)),
        grid=(N_TOK // TM_IN,),
        in_specs=x_specs + [
            pl.BlockSpec((1, 1, N_MOD, D_MODEL), lambda t: (l, _cond_of_tile(t * per_tile), 0, 0)),
            full((1, D_MODEL)),
            full((D_MODEL, RWKV_IN + MLA_PAD)),
            full((1, Q_LORA)), full((Q_LORA, 2 * H_B * LANES)), full((1, LANES)), full((1, LANES)),
            full((1, KV_LORA)), full((KV_LORA, 2 * H_B * NOPE)), full((1, LANES)), full((1, LANES)),
            rope_spec, rope_spec,
        ],
        out_specs=[rows(RWKV_IN), rows(2 * H_B * LANES), rows(2 * H_B * LANES), rows(H_B * V_DIM),
                   rows(KV_LORA), rows(ROPE_DIM)],
        out_shape=[
            jax.ShapeDtypeStruct((N_TOK, RWKV_IN), F32),
            jax.ShapeDtypeStruct((N_TOK, 2 * H_B * LANES), BF16),
            jax.ShapeDtypeStruct((N_TOK, 2 * H_B * LANES), BF16),
            jax.ShapeDtypeStruct((N_TOK, H_B * V_DIM), BF16),
            jax.ShapeDtypeStruct((N_TOK, KV_LORA), F32),
            jax.ShapeDtypeStruct((N_TOK, ROPE_DIM), F32),
        ],
        compiler_params=pltpu.CompilerParams(
            dimension_semantics=("arbitrary",), vmem_limit_bytes=VMEM_LIMIT),
        name="inproj",
    )(*x_args, mods, g, w, p["cq_g"], p["uq_w"], p["qn_g"], p["qr_g"], p["ckv_g"], p["ukv_w"], p["kn_g"],
      p["kr_g"], *rope_tabs)


def _rwkv_kernel(pr_ref, pk_ref, pv_ref, pl_ref, mur_ref, muk_ref, muv_ref, mul_ref,
                 w0_ref, wup_ref, a0_ref, aup_ref, gup_ref, kkg_ref, kag_ref, rkg_ref,
                 lng_ref, lnb_ref, *rest, seq_len, n_seqs, trip, zero_init):
    if zero_init:
        h0_ref = None
        out_ref, hfin_ref, m_s, n_s, yq_s, y0_s, g_s, bonus_s = rest
    else:
        h0_ref, out_ref, hfin_ref, m_s, n_s, yq_s, y0_s, g_s, bonus_s = rest
    C = CHUNK
    n_chunks = seq_len // C
    all_chunks = n_seqs * n_chunks
    static_trip = all_chunks == trip
    lane = lax.broadcasted_iota(jnp.int32, (1, LANES), 1)
    m0 = (lane < HD_A).astype(F32)
    m1 = 1.0 - m0
    ones_g = _group_ones(LANES, HD_A)
    row = lax.broadcasted_iota(jnp.int32, (C, 1), 0)
    i2 = lax.broadcasted_iota(jnp.int32, (2 * C, 2 * C), 0)
    j2 = lax.broadcasted_iota(jnp.int32, (2 * C, 2 * C), 1)
    same_head = (i2 // C) == (j2 // C)
    eye2 = (i2 == j2).astype(F32)
    ic = lax.broadcasted_iota(jnp.int32, (C, C), 0)
    jc = lax.broadcasted_iota(jnp.int32, (C, C), 1)
    incl_c = ((jc <= ic).astype(BF16), (jc >= ic).astype(BF16))
    incl_blk = (same_head & (j2 <= i2), same_head & (j2 >= i2))
    strict_blk = (same_head & (j2 < i2), same_head & (j2 > i2))

    def pair_masks(d):
        masks = []
        s = 1
        while s < C:
            later, earlier = ((i2 // s) % 2, (j2 // s) % 2) if d == 0 else ((j2 // s) % 2, (i2 // s) % 2)
            masks.append(((i2 // (2 * s)) == (j2 // (2 * s))) & (later == 1) & (earlier == 0))
            s *= 2
        return masks

    pair_blk = (pair_masks(0), pair_masks(1))

    def stack(x):
        return jnp.concatenate([x * m0, x * m1], axis=0)

    def chunk_rows(c):
        return pl.ds(c * C, C) if isinstance(c, int) else pl.ds(pl.multiple_of(c * C, C), C)

    def shifted(ref, mu, c):
        x = ref[chunk_rows(c), :]
        if isinstance(c, int):
            first, last = c % n_chunks == 0, c % n_chunks == n_chunks - 1
            prev_row = jnp.zeros((1, x.shape[1]), F32) if first else ref[c * C - 1:c * C, :]
            next_row = jnp.zeros((1, x.shape[1]), F32) if last else ref[(c + 1) * C:(c + 1) * C + 1, :]
        else:
            pstart = pl.multiple_of(jnp.maximum(c * C - 8, 0), 8)
            nstart = pl.multiple_of(jnp.minimum((c + 1) * C, n_seqs * seq_len - 8), 8)
            prev_row = jnp.where(c % n_chunks > 0, ref[pl.ds(pstart, 8), :][7:8], 0.0)
            next_row = jnp.where(c % n_chunks < n_chunks - 1, ref[pl.ds(nstart, 8), :][0:1], 0.0)
        prev = jnp.where(row == 0, prev_row, pltpu.roll(x, 1, 0))
        nxt = jnp.where(row == C - 1, next_row, pltpu.roll(x, C - 1, 0))
        return x + mu * (0.5 * (prev + nxt) - x)

    def chain_mats(chains):
        ds = [ch[0] for ch in chains]
        lp = [_dot_exact_lhs(incl_c[d], logw) for d, *_, logw in chains]
        lpc = [x[C - 1:C] if d == 0 else x[0:1] for d, x in zip(ds, lp)]
        st = []
        for (d, r, kd, v, kk, b, logw), x, xc in zip(chains, lp, lpc):
            p_in, p_ex, p_inv, p_end = jnp.exp(x), jnp.exp(x - logw), jnp.exp(-x), jnp.exp(xc - x)
            st.append(dict(qs=stack(r * p_in), a_s=stack(kk * p_ex), khs=stack(kd * p_inv),
                           bs=stack(b * p_inv), kbs=stack(kd * p_end), bbs=stack(b * p_end), vs=stack(v)))
        R = 2 * C
        quad = [_dot1(jnp.concatenate([s["a_s"], s["qs"]], axis=0),
                      jnp.concatenate([s["bs"], s["khs"]], axis=0), _NT) for s in st]
        lab = [jnp.where(strict_blk[d], q[:R, :R], 0.0) for d, q in zip(ds, quad)]
        lak = [jnp.where(strict_blk[d], q[:R, R:], 0.0) for d, q in zip(ds, quad)]
        dqb = [jnp.where(incl_blk[d], q[R:, :R], 0.0) for d, q in zip(ds, quad)]
        dqk = [jnp.where(incl_blk[d], q[R:, R:], 0.0) for d, q in zip(ds, quad)]
        x2 = [_dot1(l, s["vs"]) for l, s in zip(lak, st)]
        tinv = [eye2 - jnp.where(pair_blk[d][0], l, 0.0) for d, l in zip(ds, lab)]
        for lvl in range(1, len(pair_blk[0])):
            half = [_dot1(t, jnp.where(pair_blk[d][lvl], l, 0.0)) for d, t, l in zip(ds, tinv, lab)]
            tinv = [t - _dot1(h, t) for h, t in zip(half, tinv)]
        w12 = [_dot1(t, jnp.concatenate([s["a_s"], x], axis=1)) for t, s, x in zip(tinv, st, x2)]
        zero_blk = jnp.zeros((R, R), F32)
        fin = [_dot1(jnp.concatenate([jnp.concatenate([s["bbs"].T, -s["kbs"].T], axis=1),
                                      jnp.concatenate([qb, -qk], axis=1)], axis=0),
                     jnp.concatenate([w, jnp.concatenate([zero_blk, s["vs"]], axis=1)], axis=0))
               for s, qb, qk, w in zip(st, dqb, dqk, w12)]
        fold = lambda x: x[:C] + x[C:]
        return [tuple(fold(x) for x in (eye2 * jnp.exp(xc) - f[:R, :R], -f[:R, R:], s["qs"] - f[R:, :R], -f[R:, R:]))
                for xc, f, s in zip(lpc, fin, st)]

    def load_chunk(c):
        return (shifted(pr_ref, mur_ref[...], c), shifted(pk_ref, muk_ref[...], c),
                shifted(pv_ref, muv_ref[...], c), shifted(pl_ref, mul_ref[...], c))

    def compute_chunks(loaded):
        rows_of = lambda j: slice(j * C, (j + 1) * C)
        r_all, k_all, v_all, lo_all = (jnp.concatenate([ch[j] for ch in loaded], axis=0) for j in range(4))
        wa = lo_all[:, :LANES]
        gate_all = _dot1(_sigmoid(lo_all[:, LANES:]), gup_ref[...])
        kkr = k_all * kkg_ref[...]
        kk_all = kkr / (jnp.sqrt(_gsum(kkr * kkr, ones_g)) + 1e-12)
        bonus_all = _gsum(r_all * k_all * rkg_ref[...], ones_g) * v_all
        both = lambda ref: jnp.concatenate([ref[0], ref[1]], axis=1)
        both_rows = lambda ref: jnp.concatenate([ref[0:1, :], ref[1:2, :]], axis=1)
        wl = both_rows(w0_ref) + _dot3(jnp.tanh(wa), both(wup_ref))
        logw_all = -jnp.exp(-_softplus(-wl) - 0.5)
        a_all = _sigmoid(both_rows(a0_ref) + _dot3(wa, both(aup_ref)))
        chains = []
        for j, (r, k, v, _) in enumerate(loaded):
            kk = kk_all[rows_of(j)]
            for d in range(2):
                a = a_all[rows_of(j), d * LANES:(d + 1) * LANES]
                logw = logw_all[rows_of(j), d * LANES:(d + 1) * LANES]
                chains.append((d, r, k * (1.0 + (a - 1.0) * kag_ref[...]), v, kk, kk * a, logw))
        mats = chain_mats(chains)
        return [(gate_all[rows_of(j)], bonus_all[rows_of(j)], mats[2 * j:2 * j + 2]) for j in range(len(loaded))]

    def store_chunk(c, gate, bonus, mats):
        g_s[chunk_rows(c), :] = gate
        bonus_s[chunk_rows(c), :] = bonus
        for d in range(2):
            m_s[d, c], n_s[d, c], yq_s[d, c], y0_s[d, c] = mats[d]

    def phase_a_trip(i, carry):
        chunks = [i * trip + u for u in range(trip)]
        results = compute_chunks([load_chunk(c) for c in chunks])
        for c, res in zip(chunks, results):
            store_chunk(c, *res)
        return carry

    if static_trip:
        phase_a_trip(0, 0)
    else:
        lax.fori_loop(0, all_chunks // trip, phase_a_trip, 0)

    def phase_b(i, states):
        first_zero = zero_init and isinstance(i, int) and i == 0
        units = [(q, d, q * n_chunks + (i if d == 0 else n_chunks - 1 - i))
                 for q in range(n_seqs) for d in range(2)]
        if first_zero:
            return tuple(stack(n_s[d, c]) for _, d, c in units)
        new_states = tuple(stack(_dot3(m_s[d, c], states[2 * q + d]) + n_s[d, c]) for q, d, c in units)
        for q, d, c in units:
            y0_s[d, c] = y0_s[d, c] + _dot1(yq_s[d, c], states[2 * q + d])
        return new_states

    zeros_half = jnp.zeros((HD_A, HD_A), F32)
    states = []
    for q in range(n_seqs):
        for d in range(2):
            if zero_init:
                states.append(jnp.zeros((LANES, LANES), F32))
            else:
                top = jnp.concatenate([h0_ref[q, 0, d, 0], zeros_half], axis=1)
                bot = jnp.concatenate([zeros_half, h0_ref[q, 0, d, 1]], axis=1)
                states.append(jnp.concatenate([top, bot], axis=0).T)
    states = tuple(states)
    if static_trip:
        for i in range(n_chunks):
            states = phase_b(i, states)
    else:
        states = lax.fori_loop(0, n_chunks, phase_b, states)
    for q in range(n_seqs):
        for d in range(2):
            s_vk = states[2 * q + d].T
            hfin_ref[q, d, 0] = s_vk[:HD_A, :HD_A]
            hfin_ref[q, d, 1] = s_vk[HD_A:, HD_A:]

    def phase_c_trip(i, carry):
        chunks = [i * trip + u for u in range(trip)]
        ys = [y0_s[0, c] + y0_s[1, c] for c in chunks]
        extras = [(bonus_s[chunk_rows(c), :], g_s[chunk_rows(c), :]) for c in chunks]
        y_all = jnp.concatenate(ys, axis=0)
        yc = y_all - _gsum(y_all, ones_g) * (1.0 / HD_A)
        var = _gsum(yc * yc, ones_g) * (1.0 / HD_A)
        yn = yc * lax.rsqrt(var + GN_EPS) * lng_ref[...] + lnb_ref[...]
        for j, (c, (bonus, gate)) in enumerate(zip(chunks, extras)):
            out_ref[chunk_rows(c), :] = ((yn[j * C:(j + 1) * C] + bonus) * gate).astype(out_ref.dtype)
        return carry

    if static_trip:
        phase_c_trip(0, 0)
    else:
        lax.fori_loop(0, all_chunks // trip, phase_c_trip, 0)


def _rwkv(proj_r, p, h0, layer, *, seq_len, n_seq, row_off):
    zero_init = h0 is None
    trip = SHORT_SEQ_TRIP if seq_len // CHUNK < CHUNKS_PER_TRIP else CHUNKS_PER_TRIP
    seqs = max(min(LONG_SEQS_PER_STEP, n_seq), trip // (seq_len // CHUNK))
    blk_rows = seqs * seq_len
    n_chunks = blk_rows // CHUNK
    off = row_off // blk_rows
    n_pairs = DA // LANES
    col = lambda cb: (lambda s, hp: (off + s, cb + hp))
    vec = lambda: pl.BlockSpec((1, LANES), lambda s, hp: (0, hp))
    in_specs = [
        pl.BlockSpec((blk_rows, LANES), col(0)),
        pl.BlockSpec((blk_rows, LANES), col(n_pairs)),
        pl.BlockSpec((blk_rows, LANES), col(2 * n_pairs)),
        pl.BlockSpec((blk_rows, 2 * LANES), lambda s, hp: (off + s, 3 * n_pairs // 2)),
        vec(), vec(), vec(),
        pl.BlockSpec((1, 2 * LANES), lambda s, hp: (0, 0)),
        pl.BlockSpec((2, LANES), lambda s, hp: (0, hp)),
        pl.BlockSpec((2, LANES, LANES), lambda s, hp: (0, 0, hp)),
        pl.BlockSpec((2, LANES), lambda s, hp: (0, hp)),
        pl.BlockSpec((2, LANES, LANES), lambda s, hp: (0, 0, hp)),
        pl.BlockSpec((G_LORA, LANES), lambda s, hp: (0, hp)),
        vec(), vec(), vec(), vec(), vec(),
    ]
    args = [proj_r, proj_r, proj_r, proj_r, p["mu_r"], p["mu_k"], p["mu_v"], p["mu_l"],
            p["w0"], p["w_up"], p["a0"], p["a_up"], p["g_up"], p["k_k"], p["k_a"], p["r_k"],
            p["ln_g"], p["ln_b"]]
    if not zero_init:
        in_specs.append(pl.BlockSpec((seqs, 1, 2, 2, HD_A, HD_A), lambda s, hp: (s, layer, 0, hp, 0, 0)))
        args.append(h0)
    blk = (CHUNK, LANES)
    return pl.pallas_call(
        functools.partial(_rwkv_kernel, seq_len=seq_len, n_seqs=seqs, trip=trip, zero_init=zero_init),
        grid=(n_seq // seqs, n_pairs),
        in_specs=in_specs,
        out_specs=[
            pl.BlockSpec((blk_rows, LANES), lambda s, hp: (s, hp)),
            pl.BlockSpec((seqs, 2, 2, HD_A, HD_A), lambda s, hp: (s, 0, hp, 0, 0)),
        ],
        out_shape=[
            jax.ShapeDtypeStruct((n_seq * seq_len, DA), BF16),
            jax.ShapeDtypeStruct((n_seq, 2, H_A, HD_A, HD_A), F32),
        ],
        scratch_shapes=[
            pltpu.VMEM((2, n_chunks) + blk, F32),
            pltpu.VMEM((2, n_chunks) + blk, F32),
            pltpu.VMEM((2, n_chunks) + blk, F32),
            pltpu.VMEM((2, n_chunks) + blk, F32),
            pltpu.VMEM((blk_rows, LANES), F32),
            pltpu.VMEM((blk_rows, LANES), F32),
        ],
        compiler_params=pltpu.CompilerParams(
            dimension_semantics=("arbitrary", "arbitrary"), vmem_limit_bytes=VMEM_LIMIT),
        name="rwkv_ctx" if zero_init else "rwkv_lat",
    )(*args)


def _rope(x, cos, sin):
    lane = lax.broadcasted_iota(jnp.int32, x.shape, 1)
    swapped = jnp.where((lane % 32) < 16, pltpu.roll(x, LANES - 16, 1), pltpu.roll(x, 16, 1))
    return x * cos + swapped * sin


def _slab_rms(x, g):
    return x * lax.rsqrt(jnp.sum(x * x, axis=-1, keepdims=True) * (1.0 / ROPE_DIM) + EPS) * g


def _kv_expand(ckv_n, kr_att, ukv_ref, kng_ref, k_ref, v_ref):
    kv = _dot(ckv_n.astype(BF16), ukv_ref[...])
    kr_b = kr_att.astype(BF16)
    for h in range(H_B):
        kn = _rms(kv[:, h * NOPE:(h + 1) * NOPE], kng_ref[...])
        k_ref[:, 2 * h * LANES:(2 * h + 1) * LANES] = kn.astype(BF16)
        k_ref[:, (2 * h + 1) * LANES:(2 * h + 2) * LANES] = kr_b
    v_ref[...] = kv[:, H_B * NOPE:].astype(BF16)


def _mla_project(pm, is_lat, cqg_ref, uq_ref, qng_ref, qrg_ref, ckvg_ref, ukv_ref, kng_ref, krg_ref,
                 cos_ref, sin_ref, q_ref, k_ref, v_ref, ckv_ref, kr_ref):
    cq = _rms(pm[:, :Q_LORA], cqg_ref[...])
    ckv_n = _rms(pm[:, Q_LORA:Q_LORA + KV_LORA], ckvg_ref[...])
    q = _dot(cq.astype(BF16), uq_ref[...])
    ckv_ref[...] = ckv_n
    kr = _slab_rms(pm[:, Q_LORA + KV_LORA:], krg_ref[...])
    kr_ref[...] = kr[:, :ROPE_DIM]
    _kv_expand(ckv_n, kr, ukv_ref, kng_ref, k_ref, v_ref)
    qrs = []
    for h in range(H_B):
        qn = _rms(q[:, 2 * h * LANES:(2 * h + 1) * LANES], qng_ref[...])
        qrs.append(_slab_rms(q[:, (2 * h + 1) * LANES:(2 * h + 2) * LANES], qrg_ref[...]))
        q_ref[:, 2 * h * LANES:(2 * h + 1) * LANES] = (qn * ATTN_SCALE).astype(BF16)
        q_ref[:, (2 * h + 1) * LANES:(2 * h + 2) * LANES] = (qrs[h] * ATTN_SCALE).astype(BF16)

    @pl.when(is_lat)
    def _():
        cos, sin = cos_ref[...], sin_ref[...]
        kr_rot = _rope(kr, cos, sin).astype(BF16)
        for h in range(H_B):
            k_ref[:, (2 * h + 1) * LANES:(2 * h + 2) * LANES] = kr_rot
            q_ref[:, (2 * h + 1) * LANES:(2 * h + 2) * LANES] = (_rope(qrs[h], cos, sin) * ATTN_SCALE).astype(BF16)


def _mla_cache_kernel(ckv_ref, kr_ref, ukv_ref, kng_ref, k_ref, v_ref):
    _kv_expand(ckv_ref[...], kr_ref[...], ukv_ref, kng_ref, k_ref, v_ref)


def _mla_cache_expand(ckv, kr_slab, p):
    n = ckv.shape[0]
    full = lambda shape: pl.BlockSpec(shape, lambda t: tuple(0 for _ in shape))
    return pl.pallas_call(
        _mla_cache_kernel,
        grid=(1,),
        in_specs=[full((n, KV_LORA)), full((n, LANES)), full((KV_LORA, 2 * H_B * NOPE)), full((1, LANES))],
        out_specs=[full((n, 2 * H_B * LANES)), full((n, H_B * V_DIM))],
        out_shape=[
            jax.ShapeDtypeStruct((n, 2 * H_B * LANES), BF16),
            jax.ShapeDtypeStruct((n, H_B * V_DIM), BF16),
        ],
        compiler_params=pltpu.CompilerParams(
            dimension_semantics=("arbitrary",), vmem_limit_bytes=VMEM_LIMIT),
        name="mla_cache_expand",
    )(ckv, kr_slab, p["ukv_w"], p["kn_g"])


def _attn_kernel(q_ref, k_ref, v_ref, *rest, with_ctx, seq_len, n_seqs):
    if with_ctx:
        k2_ref, v2_ref, o_ref = rest
    else:
        (o_ref,) = rest
    hs = [slice(2 * h * LANES, (2 * h + 2) * LANES) for h in range(H_B)]
    vs = [slice(h * V_DIM, (h + 1) * V_DIM) for h in range(H_B)]
    units = [(b, h) for b in range(n_seqs) for h in range(H_B)]
    qrows = lambda b: slice(b * TM, (b + 1) * TM)
    krows = lambda b: slice(b * seq_len, (b + 1) * seq_len)

    def keys_values(b, h):
        kv = [(k_ref[krows(b), hs[h]], v_ref[krows(b), vs[h]])]
        if with_ctx:
            kv.append((k2_ref[:, hs[h]], v2_ref[:, vs[h]]))
        return kv

    scores = [[_dot(q_ref[qrows(b), hs[h]], k, _NT) for k, _ in keys_values(b, h)] for b, h in units]
    probs, dens = [], []
    for sc in scores:
        m = functools.reduce(jnp.maximum, [jnp.max(s, axis=-1, keepdims=True) for s in sc])
        pr = [jnp.exp(s - m) for s in sc]
        dens.append(functools.reduce(lambda a, b: a + b, [jnp.sum(p, axis=-1, keepdims=True) for p in pr]))
        probs.append([p.astype(BF16) for p in pr])
    for (b, h), pr, den in zip(units, probs, dens):
        o = functools.reduce(lambda a, c: a + c, [_dot(p, v) for p, (_, v) in zip(pr, keys_values(b, h))])
        o_ref[qrows(b), vs[h]] = (o / den).astype(o_ref.dtype)


def _attention(q, k, v, k2=None, v2=None, *, n_batch, seq_len, row_off):
    with_ctx = k2 is not None
    q_tiles = seq_len // TM
    n_seqs = ATTN_SEQS_PER_STEP if q_tiles == 1 else 1
    kw = 2 * H_B * LANES
    q0 = row_off // (n_seqs * TM)
    k0 = row_off // (n_seqs * seq_len)
    in_specs = [
        pl.BlockSpec((n_seqs * TM, kw), lambda b, t: (q0 + b * q_tiles + t, 0)),
        pl.BlockSpec((n_seqs * seq_len, kw), lambda b, t: (k0 + b, 0)),
        pl.BlockSpec((n_seqs * seq_len, H_B * V_DIM), lambda b, t: (k0 + b, 0)),
    ]
    args = [q, k, v]
    if with_ctx:
        in_specs += [
            pl.BlockSpec((PAST_LEN, kw), lambda b, t: (b, 0)),
            pl.BlockSpec((PAST_LEN, H_B * V_DIM), lambda b, t: (b, 0)),
        ]
        args += [k2, v2]
    return pl.pallas_call(
        functools.partial(_attn_kernel, with_ctx=with_ctx, seq_len=seq_len, n_seqs=n_seqs),
        grid=(n_batch // n_seqs, q_tiles),
        in_specs=in_specs,
        out_specs=pl.BlockSpec((n_seqs * TM, H_B * V_DIM), lambda b, t: (b * q_tiles + t, 0)),
        out_shape=jax.ShapeDtypeStruct((n_batch * seq_len, H_B * V_DIM), BF16),
        compiler_params=pltpu.CompilerParams(
            dimension_semantics=("arbitrary", "arbitrary"), vmem_limit_bytes=VMEM_LIMIT),
        name="attn_lat" if with_ctx else "attn_ctx",
    )(*args)


def _pool_bands():
    i = jnp.arange(TM)[:, None]
    e = jnp.arange(POOL_K)[None, :]
    e = jnp.where(e < TM, e, jnp.where(e < TM + HALO, e - TM - HALO, e - HALO))
    return jnp.stack([(e >= i - w // 2) & (e < i + w // 2) for w in POOL_WINDOWS]).astype(BF16)


def _pool_diffs(h, h_prev, h_next, pos0, seq_len, band_ref):
    ext = jnp.concatenate([h, h_prev, h_next, jnp.zeros((POOL_K - TM - 2 * HALO, D_MODEL), F32)], axis=0)
    pos = pos0 + lax.broadcasted_iota(jnp.int32, (TM, 1), 0)
    sums = [_dot_exact_lhs(band_ref[gi], ext[:, gi * GP:(gi + 1) * GP]) for gi in range(len(POOL_WINDOWS))]
    diffs = []
    for gi, win in enumerate(POOL_WINDOWS):
        half = win // 2
        cnt = (jnp.minimum(pos + half, seq_len) - jnp.maximum(pos - half, 0)).astype(F32)
        diffs.append((sums[gi] / cnt - h[:, gi * GP:(gi + 1) * GP]).astype(BF16))
    return diffs


def _tail_kernel(*refs, mixer, n_x, row_off):
    x_refs, rest = refs[:n_x], refs[n_x:]
    if mixer == "proj":
        a0_ref, a1_ref, b0_ref, b1_ref, ow_ref, mod_ref, g2_ref, w1_ref, w3_ref, w2_ref, o_ref = rest
    else:
        xp_ref, xn_ref, g1_ref, band_ref, pw_ref, ps_ref, mod_ref, g2_ref, w1_ref, w3_ref, w2_ref, o_ref = rest
    m = mod_ref[0, 0]
    x = _row_tile(x_refs, TM_FFN)
    if mixer == "proj":
        mix = (_dot(_row_tile((a0_ref, a1_ref), TM_FFN), ow_ref[:DA, :])
               + _dot(_row_tile((b0_ref, b1_ref), TM_FFN), ow_ref[DA:, :]))
    else:
        row0 = row_off + pl.program_id(0) * TM_FFN
        is_lat = row0 >= N_CTX
        seq_len = jnp.where(is_lat, DEC_SEQ, SEQ)
        norm_mod = lambda v: _rms(v, g1_ref[...]) * (1.0 + m[1:2]) + m[0:1]
        h, hp, hn = norm_mod(x), norm_mod(xp_ref[...]), norm_mod(xn_ref[...])
        parts = []
        n_sub = TM_FFN // TM
        for k in range(n_sub):
            r0 = row0 + k * TM
            pos0 = jnp.where(is_lat, (r0 - N_CTX) % DEC_SEQ, r0 % SEQ)
            prev = hp if k == 0 else h[k * TM - HALO:k * TM]
            nxt = hn if k == n_sub - 1 else h[(k + 1) * TM:(k + 1) * TM + HALO]
            prev = jnp.where(pos0 > 0, prev, 0.0)
            nxt = jnp.where(pos0 + TM < seq_len, nxt, 0.0)
            parts.append(_pool_diffs(h[k * TM:(k + 1) * TM], prev, nxt, pos0, seq_len, band_ref))
        mix = jnp.concatenate(
            [_dot(jnp.concatenate([p[gi] for p in parts], axis=0), pw_ref[gi]) for gi in range(len(POOL_WINDOWS))],
            axis=1) * ps_ref[...]
    x1 = x + m[2:3] * mix
    h2 = (_rms(x1, g2_ref[...]) * (1.0 + m[4:5]) + m[3:4]).astype(BF16)
    ups = [(_dot(h2, w1_ref[0, :, lo:hi]), _dot(h2, w3_ref[0, :, lo:hi])) for lo, hi in FF_SPLITS]
    acts = [(_silu(a) * b).astype(BF16) for a, b in ups]
    acc = functools.reduce(lambda p, q: p + q,
                           [_dot(u, w2_ref[0, lo:hi, :]) for u, (lo, hi) in zip(acts, FF_SPLITS)])
    o_ref[...] = x1 + m[5:6] * acc


def _layer_tail(x, mods, l, g2, w1, w3, w2, *, proj=None, pool=None, row_off=0, n_rows=N_TOK):
    per_tile = TM_FFN // TM
    t0 = row_off // TM_FFN
    const = lambda shape: pl.BlockSpec(shape, lambda i: tuple(0 for _ in shape))
    resident = lambda shape: pl.BlockSpec((1,) + shape, lambda i: (l, 0, 0), pipeline_mode=pl.Buffered(1))
    if proj is not None:
        assert row_off == 0 and n_rows == N_TOK
        a_out, b_out, out_w = proj
        x_specs, x_args = _row_specs(x, D_MODEL, TM_FFN)
        a_specs, a_args = _row_specs(a_out, DA, TM_FFN)
        b_specs, b_args = _row_specs(b_out, DA, TM_FFN)
        mix_specs = a_specs + b_specs + [const((D_MODEL, D_MODEL))]
        mix_args = a_args + b_args + [out_w]
        mixer = "proj"
    else:
        g1, pw, ps = pool
        r = TM_FFN // HALO
        last = N_TOK // HALO - 1
        x_specs = [pl.BlockSpec((TM_FFN, D_MODEL), lambda i: (t0 + i, 0))]
        x_args = [x]
        mix_specs = [
            pl.BlockSpec((HALO, D_MODEL), lambda i: (jnp.maximum((t0 + i) * r - 1, 0), 0)),
            pl.BlockSpec((HALO, D_MODEL), lambda i: (jnp.minimum((t0 + i + 1) * r, last), 0)),
            const((1, D_MODEL)), const((len(POOL_WINDOWS), TM, POOL_K)),
            const((len(POOL_WINDOWS), GP, GP)), const((1, D_MODEL)),
        ]
        mix_args = [x, x, g1, _pool_bands(), pw, ps]
        mixer = "pool"
    return pl.pallas_call(
        functools.partial(_tail_kernel, mixer=mixer, n_x=len(x_args), row_off=row_off),
        grid=(n_rows // TM_FFN,),
        in_specs=x_specs + mix_specs + [
            pl.BlockSpec((1, 1, N_MOD, D_MODEL), lambda i: (l, _cond_of_tile((t0 + i) * per_tile), 0, 0)),
            const((1, D_MODEL)),
            resident((D_MODEL, D_FF)), resident((D_MODEL, D_FF)), resident((D_FF, D_MODEL)),
        ],
        out_specs=pl.BlockSpec((TM_FFN, D_MODEL), lambda i: (i, 0)),
        out_shape=jax.ShapeDtypeStruct((n_rows, D_MODEL), F32),
        compiler_params=pltpu.CompilerParams(
            dimension_semantics=("arbitrary",), vmem_limit_bytes=VMEM_LIMIT),
        name="tail_" + mixer,
    )(*x_args, *mix_args, mods, g2, w1, w3, w2)


def _rope_tables():
    pos = jnp.arange(DEC_SEQ)
    freqs = ROPE_BASE ** (-jnp.arange(ROPE_AXIS_FREQS, dtype=F32) / ROPE_AXIS_FREQS)
    ang_row = (pos // GRID_W).astype(F32)[:, None] * freqs
    ang_col = (pos % GRID_W).astype(F32)[:, None] * freqs
    zeros = jnp.zeros((DEC_SEQ, LANES - ROPE_DIM), F32)
    cos = jnp.concatenate([jnp.cos(ang_row)] * 2 + [jnp.cos(ang_col)] * 2 + [zeros], axis=1)
    sin = jnp.concatenate([-jnp.sin(ang_row), jnp.sin(ang_row), -jnp.sin(ang_col), jnp.sin(ang_col), zeros], axis=1)
    return cos, sin


def _slab(v):
    return jnp.pad(v, (0, LANES - v.shape[0]))[None]


def _even_params(i, in_w, out_w, shift_mu, w0, w_up, a0, a_up, g_up, k_k, k_a, r_k, ln_g, ln_b,
                 cq_g, uq_w, ckv_g, ukv_w, qn_g, qr_g, kn_g, kr_g):
    row = lambda v: v[None]
    in_w_pad = jnp.pad(in_w[i], ((0, 0), (0, MLA_PAD - MLA_IN))).astype(BF16)
    mu = shift_mu[i]
    w_up_pad = jnp.pad(w_up[i], ((0, 0), (0, A_LORA), (0, 0)))
    a_up_pad = jnp.pad(a_up[i], ((0, 0), (W_LORA, 0), (0, 0)))
    uq = uq_w[i].reshape(Q_LORA, H_B, NOPE + ROPE_DIM)
    uq = jnp.pad(uq, ((0, 0), (0, 0), (0, 2 * LANES - NOPE - ROPE_DIM))).reshape(Q_LORA, 2 * H_B * LANES)
    ukv = ukv_w[i].reshape(KV_LORA, H_B, NOPE + V_DIM)
    ukv = jnp.concatenate([ukv[:, :, :NOPE].reshape(KV_LORA, -1), ukv[:, :, NOPE:].reshape(KV_LORA, -1)], axis=1)
    rw = dict(mu_r=row(mu[:DA]), mu_k=row(mu[DA:2 * DA]), mu_v=row(mu[2 * DA:3 * DA]), mu_l=row(mu[3 * DA:]),
              w0=w0[i], w_up=w_up_pad, a0=a0[i], a_up=a_up_pad, g_up=g_up[i],
              k_k=row(k_k[i]), k_a=row(k_a[i]), r_k=row(r_k[i].reshape(DA)), ln_g=row(ln_g[i]), ln_b=row(ln_b[i]))
    mla = dict(cq_g=row(cq_g[i]), uq_w=uq.astype(BF16), qn_g=row(qn_g[i]), qr_g=_slab(qr_g[i]),
               ckv_g=row(ckv_g[i]), ukv_w=ukv.astype(BF16), kn_g=row(kn_g[i]), kr_g=_slab(kr_g[i]))
    return in_w_pad, out_w[i].astype(BF16), rw, mla


def kernel(x_prompt, x_sample, c, state_rwkv, cache_mla_ckv, cache_mla_krope, c_ctx,
           norm1_g, norm2_g, ada_w, ada_b, ffn_w1, ffn_w3, ffn_w2,
           in_w, out_w, shift_mu, rwkv_w0, rwkv_w_up, rwkv_a0, rwkv_a_up, rwkv_g_up,
           rwkv_k_k, rwkv_k_a, rwkv_r_k, rwkv_ln_g, rwkv_ln_b,
           mla_cq_g, mla_uq_w, mla_ckv_g, mla_ukv_w, mla_qn_g, mla_qr_g, mla_kn_g, mla_kr_g,
           pool_w, pool_scale):
    x = (x_prompt.reshape(N_CTX, D_MODEL), x_sample.reshape(N_LAT, D_MODEL))
    conds = jnp.concatenate([c_ctx[None], c, jnp.zeros((8 - 1 - DEC_BATCH, D_MODEL), F32)], axis=0)
    mods = _adaln(conds, ada_w, ada_b).reshape(DEPTH, 8, N_MOD, D_MODEL)
    rope_tabs = _rope_tables()
    w1_all, w3_all, w2_all = ffn_w1.astype(BF16), ffn_w3.astype(BF16), ffn_w2.astype(BF16)

    new_s, new_ckv, new_kr = [], [], []
    for l in range(DEPTH):
        n1 = norm1_g[l][None]
        if l % 2 == 0:
            i = l // 2
            in_w_pad, out_w_b, rw, mla = _even_params(
                i, in_w, out_w, shift_mu, rwkv_w0, rwkv_w_up, rwkv_a0, rwkv_a_up, rwkv_g_up,
                rwkv_k_k, rwkv_k_a, rwkv_r_k, rwkv_ln_g, rwkv_ln_b,
                mla_cq_g, mla_uq_w, mla_ckv_g, mla_ukv_w, mla_qn_g, mla_qr_g, mla_kn_g, mla_kr_g)
            proj_r, q_all, k_all, v_all, ckv_all, kr_all = _inproj(x, mods, l, n1, in_w_pad, mla, rope_tabs)
            a_ctx, s_ctx = _rwkv(proj_r, rw, None, i, seq_len=SEQ, n_seq=BATCH, row_off=0)
            a_lat, _ = _rwkv(proj_r, rw, state_rwkv, i, seq_len=DEC_SEQ, n_seq=DEC_BATCH, row_off=N_CTX)
            kr_cache = jnp.pad(cache_mla_krope[:, i].reshape(DEC_BATCH * PAST_LEN, ROPE_DIM),
                               ((0, 0), (0, LANES - ROPE_DIM)))
            k_p, v_p = _mla_cache_expand(cache_mla_ckv[:, i].reshape(DEC_BATCH * PAST_LEN, KV_LORA), kr_cache, mla)
            b_ctx = _attention(q_all, k_all, v_all, n_batch=BATCH, seq_len=SEQ, row_off=0)
            b_lat = _attention(q_all, k_all, v_all, k_p, v_p, n_batch=DEC_BATCH, seq_len=DEC_SEQ, row_off=N_CTX)
            mixer = dict(proj=((a_ctx, a_lat), (b_ctx, b_lat), out_w_b))
            new_s.append(s_ctx)
            new_ckv.append(ckv_all[:N_CTX].reshape(BATCH, SEQ, KV_LORA))
            new_kr.append(kr_all[:N_CTX].reshape(BATCH, SEQ, ROPE_DIM))
        else:
            j = l // 2
            mixer = dict(pool=(n1, pool_w[j].astype(BF16), pool_scale[j][None]))
        tail_args = (mods, l, norm2_g[l][None], w1_all, w3_all, w2_all)
        if l < DEPTH - 1:
            x = _layer_tail(x, *tail_args, **mixer)
        else:
            y_p = _layer_tail(x, *tail_args, **mixer, row_off=0, n_rows=N_CTX).reshape(BATCH, SEQ, D_MODEL)
            y_s = _layer_tail(x, *tail_args, **mixer, row_off=N_CTX, n_rows=N_LAT).reshape(DEC_BATCH, DEC_SEQ, D_MODEL)
    return (y_p, y_s, jnp.stack(new_s, axis=1), jnp.stack(new_ckv, axis=1), jnp.stack(new_kr, axis=1))
```
